```python
import jax
import jax.numpy as jnp
from jax import lax

D_MODEL = 1024
BATCH = 8
SEQ = 4096
DEPTH = 4

CTX_LEN = 256
GRID_W = 64
N_MIXERS = 2
N_NA_LAYERS = (DEPTH + N_MIXERS - 1) // N_MIXERS
N_LRU_LAYERS = DEPTH // N_MIXERS
N_HEADS = 16
HEAD_DIM = D_MODEL // N_HEADS
WIN_ROWS = 8
WIN_COLS = 16
Q_BLOCK_W = 16
K_BLOCK_W = Q_BLOCK_W + WIN_COLS
D_RNN = D_MODEL
LRU_BLOCKS = 16
LRU_BLOCK_W = D_RNN // LRU_BLOCKS
LRU_CONV_W = 4
LRU_C = 8.0
D_FF = 3 * D_MODEL
FFN_CONV_W = 3
N_MODS = 6
EPS = 1e-6

kernel_name = 'hybrid_na_rglru_diffusion_block'


def rms_norm(t, gain):
    tf = t.astype(jnp.float32)
    y = tf * lax.rsqrt(jnp.mean(tf * tf, axis=-1, keepdims=True) + EPS)
    return (y * gain.astype(jnp.float32)).astype(t.dtype)


def modulate(t, gain, shift, scale):
    return rms_norm(t, gain) * (1.0 + scale[:, None, :]) + shift[:, None, :]


def depthwise_conv_centred(t, w, b):
    width, length = w.shape[0], t.shape[1]
    left = width // 2
    tp = jnp.pad(t, ((0, 0), (left, width - 1 - left), (0, 0)))
    y = b + tp[:, 0:length] * w[0]
    for k in range(1, width):
        y = y + tp[:, k:k + length] * w[k]
    return y


def neighbourhood_attention(q, k, v, kc, vc, rpb):
    B, S, H, Dh = q.shape
    rows = S // GRID_W
    wr = min(WIN_ROWS, rows)
    n_cb = GRID_W // Q_BLOCK_W
    scale = Dh ** -0.5
    qg = q.reshape(B, rows, GRID_W, H, Dh)
    kg = k.reshape(B, rows, GRID_W, H, Dh)
    vg = v.reshape(B, rows, GRID_W, H, Dh)
    n_lat = wr * K_BLOCK_W

    def attend_block(idx):
        r = idx // n_cb
        q0 = (idx % n_cb) * Q_BLOCK_W
        rs = jnp.clip(r - WIN_ROWS // 2, 0, rows - wr)
        ks = jnp.clip(q0 - WIN_COLS // 2, 0, GRID_W - K_BLOCK_W)
        qb = lax.dynamic_slice(qg, (0, r, q0, 0, 0), (B, 1, Q_BLOCK_W, H, Dh))[:, 0]
        kb = lax.dynamic_slice(kg, (0, rs, ks, 0, 0), (B, wr, K_BLOCK_W, H, Dh)).reshape(B, n_lat, H, Dh)
        vb = lax.dynamic_slice(vg, (0, rs, ks, 0, 0), (B, wr, K_BLOCK_W, H, Dh)).reshape(B, n_lat, H, Dh)
        q_col = q0 + jnp.arange(Q_BLOCK_W)
        win_start = jnp.clip(q_col - WIN_COLS // 2, 0, GRID_W - WIN_COLS)
        k_col = ks + jnp.arange(K_BLOCK_W)
        col_ok = (k_col[None, :] >= win_start[:, None]) & (k_col[None, :] < win_start[:, None] + WIN_COLS)
        mask = jnp.broadcast_to(col_ok[:, None, :], (Q_BLOCK_W, wr, K_BLOCK_W)).reshape(Q_BLOCK_W, n_lat)
        d_row = rs + jnp.arange(wr) - r + (WIN_ROWS - 1)
        d_col = jnp.clip(k_col[None, :] - q_col[:, None] + (WIN_COLS - 1), 0, 2 * WIN_COLS - 2)
        bias = rpb[:, d_row[None, :, None], d_col[:, None, :]].reshape(H, Q_BLOCK_W, n_lat)
        s_lat = jnp.einsum('bqhd,bkhd->bhqk', qb, kb).astype(jnp.float32) * scale + bias.astype(jnp.float32)[None]
        s_lat = jnp.where(mask[None, None], s_lat, -jnp.inf)
        s_ctx = jnp.einsum('bqhd,bkhd->bhqk', qb, kc).astype(jnp.float32) * scale
        p = jax.nn.softmax(jnp.concatenate([s_lat, s_ctx], axis=-1), axis=-1).astype(v.dtype)
        return (jnp.einsum('bhqk,bkhd->bqhd', p[..., :n_lat], vb)
                + jnp.einsum('bhqk,bkhd->bqhd', p[..., n_lat:], vc))

    out = lax.map(attend_block, jnp.arange(rows * n_cb))
    out = out.reshape(rows, n_cb, B, Q_BLOCK_W, H, Dh).transpose(2, 0, 1, 3, 4, 5)
    return out.reshape(B, S, H * Dh)


def context_attention(qc, kc, vc):
    s = jnp.einsum('bqhd,bkhd->bhqk', qc, kc).astype(jnp.float32) * (qc.shape[-1] ** -0.5)
    p = jax.nn.softmax(s, axis=-1).astype(vc.dtype)
    o = jnp.einsum('bhqk,bkhd->bqhd', p, vc)
    return o.reshape(qc.shape[0], qc.shape[1], -1)


def na_mixer(h, hc, w_qkv, q_gain, k_gain, rpb, w_out, with_ctx):
    B, S, D = h.shape
    qkv = (h @ w_qkv).reshape(B, S, 3, N_HEADS, HEAD_DIM)
    q = rms_norm(qkv[:, :, 0], q_gain)
    k = rms_norm(qkv[:, :, 1], k_gain)
    v = qkv[:, :, 2]
    C = hc.shape[1]
    if with_ctx:
        qkv_c = (hc @ w_qkv).reshape(B, C, 3, N_HEADS, HEAD_DIM)
        qc = rms_norm(qkv_c[:, :, 0], q_gain)
        kc = rms_norm(qkv_c[:, :, 1], k_gain)
        vc = qkv_c[:, :, 2]
    else:
        kv_c = (hc @ w_qkv[:, D:]).reshape(B, C, 2, N_HEADS, HEAD_DIM)
        kc = rms_norm(kv_c[:, :, 0], k_gain)
        vc = kv_c[:, :, 1]
    y = neighbourhood_attention(q, k, v, kc, vc, rpb) @ w_out
    yc = context_attention(qc, kc, vc) @ w_out if with_ctx else None
    return y, yc


def block_diag_linear(t, w, b):
    tb = t.reshape(t.shape[:-1] + (LRU_BLOCKS, LRU_BLOCK_W))
    return jnp.einsum('blnd,nde->blne', tb, w).reshape(t.shape) + b


def _lru_combine(left, right):
    a_l, b_l = left
    a_r, b_r = right
    return a_l * a_r, a_r * b_l + b_r


def rglru(xr, wa, ba, wx, bx, lam, h0, reverse):
    xf = xr.astype(jnp.float32)
    r = jax.nn.sigmoid(block_diag_linear(xf, wa.astype(jnp.float32), ba.astype(jnp.float32)))
    i = jax.nn.sigmoid(block_diag_linear(xf, wx.astype(jnp.float32), bx.astype(jnp.float32)))
    log_a = -LRU_C * r * jax.nn.softplus(-lam.astype(jnp.float32))
    a = jnp.exp(log_a)
    b = jnp.sqrt(-jnp.expm1(2.0 * log_a)) * (i * xf)
    if h0 is not None:
        first = -1 if reverse else 0
        b = b.at[:, first].add(a[:, first] * h0.astype(jnp.float32))
    _, hs = lax.associative_scan(_lru_combine, (a, b), axis=1, reverse=reverse)
    return hs.astype(xr.dtype)


def lru_mixer(h, hc, w_in, conv_w, conv_b, ga_w, ga_b, gx_w, gx_b, lam, w_out, with_ctx):
    gate, rec = jnp.split(h @ w_in, 2, axis=-1)
    xr = depthwise_conv_centred(rec, conv_w, conv_b)
    if with_ctx:
        gate_c, rec_c = jnp.split(hc @ w_in, 2, axis=-1)
    else:
        rec_c = hc @ w_in[:, D_RNN:]
    xr_c = depthwise_conv_centred(rec_c, conv_w, conv_b)
    hs_c_f = rglru(xr_c, ga_w[0], ga_b[0], gx_w[0], gx_b[0], lam[0], None, False)
    hs_f = rglru(xr, ga_w[0], ga_b[0], gx_w[0], gx_b[0], lam[0], hs_c_f[:, -1], False)
    hs_c_b = rglru(xr_c, ga_w[1], ga_b[1], gx_w[1], gx_b[1], lam[1], None, True)
    hs_b = rglru(xr, ga_w[1], ga_b[1], gx_w[1], gx_b[1], lam[1], hs_c_b[:, 0], True)
    y = ((hs_f + hs_b) * jax.nn.gelu(gate, approximate=True)) @ w_out
    yc = ((hs_c_f + hs_c_b) * jax.nn.gelu(gate_c, approximate=True)) @ w_out if with_ctx else None
    return y, yc


def conv_ffn(t, w_up, conv_w, conv_b, w_down):
    u = depthwise_conv_centred(t @ w_up, conv_w, conv_b)
    val, gate = jnp.split(u, 2, axis=-1)
    return (val * jax.nn.silu(gate)) @ w_down


def setup_inputs(seed: int = 0) -> dict:
    key = jax.random.key(seed)
    ks = jax.random.split(key, 32)
    f32 = jnp.float32

    def nrm(k, shape, scale):
        return jax.random.normal(k, shape, f32) * scale

    L, NA, NL = DEPTH, N_NA_LAYERS, N_LRU_LAYERS
    u = jax.random.uniform(ks[20], (NL, 2, D_RNN), f32, 0.9, 0.999)
    a_base = u ** (1.0 / LRU_C)
    lam = jnp.log(a_base) - jnp.log1p(-a_base)
    return {
        'x': nrm(ks[0], (BATCH, SEQ, D_MODEL), 1.0),
        'c': nrm(ks[1], (BATCH, D_MODEL), 1.0),
        'ctx': nrm(ks[2], (BATCH, CTX_LEN, D_MODEL), 1.0),
        'c_ctx': nrm(ks[3], (D_MODEL,), 1.0),
        'ada_w': nrm(ks[4], (L, D_MODEL, N_MODS * D_MODEL), 0.5 * D_MODEL ** -0.5),
        'ada_b': nrm(ks[5], (L, N_MODS * D_MODEL), 0.02),
        'norm_mix': 1.0 + nrm(ks[6], (L, D_MODEL), 0.02),
        'norm_ffn': 1.0 + nrm(ks[7], (L, D_MODEL), 0.02),
        'na_w_qkv': nrm(ks[8], (NA, D_MODEL, 3 * D_MODEL), D_MODEL ** -0.5),
        'na_q_gain': 1.0 + nrm(ks[9], (NA, HEAD_DIM), 0.02),
        'na_k_gain': 1.0 + nrm(ks[10], (NA, HEAD_DIM), 0.02),
        'na_rpb': nrm(ks[11], (NA, N_HEADS, 2 * WIN_ROWS - 1, 2 * WIN_COLS - 1), 0.5),
        'na_w_out': nrm(ks[12], (NA, D_MODEL, D_MODEL), D_MODEL ** -0.5),
        'lru_w_in': nrm(ks[13], (NL, D_MODEL, 2 * D_RNN), D_MODEL ** -0.5),
        'lru_conv_w': nrm(ks[14], (NL, LRU_CONV_W, D_RNN), LRU_CONV_W ** -0.5),
        'lru_conv_b': nrm(ks[15], (NL, D_RNN), 0.02),
        'lru_ga_w': nrm(ks[16], (NL, 2, LRU_BLOCKS, LRU_BLOCK_W, LRU_BLOCK_W), LRU_BLOCK_W ** -0.5),
        'lru_ga_b': nrm(ks[17], (NL, 2, D_RNN), 0.02),
        'lru_gx_w': nrm(ks[18], (NL, 2, LRU_BLOCKS, LRU_BLOCK_W, LRU_BLOCK_W), LRU_BLOCK_W ** -0.5),
        'lru_gx_b': nrm(ks[19], (NL, 2, D_RNN), 0.02),
        'lru_lambda': lam,
        'lru_w_out': nrm(ks[21], (NL, D_RNN, D_MODEL), D_RNN ** -0.5),
        'ffn_w_up': nrm(ks[22], (L, D_MODEL, 2 * D_FF), D_MODEL ** -0.5),
        'ffn_conv_w': nrm(ks[23], (L, FFN_CONV_W, 2 * D_FF), FFN_CONV_W ** -0.5),
        'ffn_conv_b': nrm(ks[24], (L, 2 * D_FF), 0.02),
        'ffn_w_down': nrm(ks[25], (L, D_FF, D_MODEL), D_FF ** -0.5),
    }


def reference(x, c, ctx, c_ctx, ada_w, ada_b, norm_mix, norm_ffn,
              na_w_qkv, na_q_gain, na_k_gain, na_rpb, na_w_out,
              lru_w_in, lru_conv_w, lru_conv_b, lru_ga_w, lru_ga_b,
              lru_gx_w, lru_gx_b, lru_lambda, lru_w_out,
              ffn_w_up, ffn_conv_w, ffn_conv_b, ffn_w_down):
    s_lat = jax.nn.silu(c)
    s_ctx = jax.nn.silu(c_ctx)[None]
    for i in range(DEPTH):
        last = i == DEPTH - 1
        j = i // N_MIXERS
        sh1, sc1, g1, sh2, sc2, g2 = jnp.split(s_lat @ ada_w[i] + ada_b[i], N_MODS, axis=-1)
        csh1, csc1, cg1, csh2, csc2, cg2 = jnp.split(s_ctx @ ada_w[i] + ada_b[i], N_MODS, axis=-1)
        h = modulate(x, norm_mix[i], sh1, sc1)
        hc = modulate(ctx, norm_mix[i], csh1, csc1)
        if i % N_MIXERS == 0:
            y, yc = na_mixer(h, hc, na_w_qkv[j], na_q_gain[j], na_k_gain[j], na_rpb[j],
                             na_w_out[j], not last)
        else:
            y, yc = lru_mixer(h, hc, lru_w_in[j], lru_conv_w[j], lru_conv_b[j],
                              lru_ga_w[j], lru_ga_b[j], lru_gx_w[j], lru_gx_b[j],
                              lru_lambda[j], lru_w_out[j], not last)
        x = x + g1[:, None] * y
        h = modulate(x, norm_ffn[i], sh2, sc2)
        x = x + g2[:, None] * conv_ffn(h, ffn_w_up[i], ffn_conv_w[i], ffn_conv_b[i], ffn_w_down[i])
        if not last:
            ctx = ctx + cg1[:, None] * yc
            hc = modulate(ctx, norm_ffn[i], csh2, csc2)
            ctx = ctx + cg2[:, None] * conv_ffn(hc, ffn_w_up[i], ffn_conv_w[i], ffn_conv_b[i], ffn_w_down[i])
    return x
```

```python
import functools

import jax
import jax.numpy as jnp
from jax import lax
from jax.experimental import pallas as pl
from jax.experimental.pallas import tpu as pltpu

F32 = jnp.float32
BF16 = jnp.bfloat16

D = 1024
B = 8
S = 4096
DEPTH = 4
C = 256
LT = C + S
NROWS = B * LT
GRID_W = 64
IMG_ROWS = S // GRID_W
H = 16
HD = 64
NPAIR = H // 2
LANES = 128
WIN_ROWS = 8
WIN_COLS = 16
D_FF = 3 * D
LRU_BLOCK_W = 64
LRU_C = 8.0
EPS = 1e-6
NEG_INF = float("-inf")

TM = 544
TPB = LT // TM
FC = 512
QB = 256
NQB = LT // QB
TS = 128
NCH = LT // TS
CCH = C // TS
PITCH = TS + 4
GW = 256
VMEM_LIMIT = 56 * 1024 * 1024


def _cparams(n_axes):
    return pltpu.CompilerParams(
        dimension_semantics=("arbitrary",) * n_axes, vmem_limit_bytes=VMEM_LIMIT)


def _const_spec(shape):
    nd = len(shape)
    return pl.BlockSpec(shape, lambda *_: (0,) * nd, pipeline_mode=pl.Buffered(1))


def _mods_kernel(cc_ref, w_ref, b_ref, o_ref):
    c = cc_ref[...]
    s = c * jax.nn.sigmoid(c)
    o_ref[...] = jnp.dot(s.astype(BF16), w_ref[...].astype(BF16),
                         preferred_element_type=F32) + b_ref[...]


def _mods_call(cc, ada_w, ada_b):
    return pl.pallas_call(
        _mods_kernel,
        grid=(DEPTH, 6),
        in_specs=[
            pl.BlockSpec((16, D), lambda l, k: (0, 0)),
            pl.BlockSpec((None, D, D), lambda l, k: (l, 0, k)),
            pl.BlockSpec((None, None, 1, D), lambda l, k: (l, k, 0, 0)),
        ],
        out_specs=pl.BlockSpec((None, None, 16, D), lambda l, k: (l, k, 0, 0)),
        out_shape=jax.ShapeDtypeStruct((DEPTH, 6, 16, D), F32),
        compiler_params=_cparams(2),
        name="adaln_mods",
    )(cc, ada_w, ada_b.reshape(DEPTH, 6, 1, D))


def _tile_pos(n, lo):
    i = pl.program_id(0)
    b = i // TPB
    j = i % TPB
    pos = j * TM + lo + lax.broadcasted_iota(jnp.int32, (n, 1), 0)
    return b, pos


def _mod_rows(m_ref, b, is_ctx):
    return jnp.where(is_ctx, m_ref[8:9, :], m_ref[pl.ds(b, 1), :])


def _modulate(x, gain, shift, scale):
    y = x * lax.rsqrt(jnp.mean(x * x, axis=-1, keepdims=True) + EPS)
    return (y * gain) * (1.0 + scale) + shift


def _mods_spec(layer, kind):
    return pl.BlockSpec((None, None, 16, D), lambda *_: (layer, kind, 0, 0))


def _row_spec(width=D):
    return pl.BlockSpec((TM, width), lambda i: (i, 0))


def _pair_spec():
    return pl.BlockSpec((NPAIR, TM, LANES), lambda i: (0, i, 0))


def _qkv_kernel(x_ref, sh_ref, sc_ref, gain_ref, w_ref, qg_ref, kg_ref, g1_ref, g2_ref,
                q_ref, k_ref, v_ref):
    b, pos = _tile_pos(TM, 0)
    is_ctx = pos < C
    h = _modulate(x_ref[...], gain_ref[...], _mod_rows(sh_ref, b, is_ctx),
                  _mod_rows(sc_ref, b, is_ctx)).astype(BF16)
    for idx, (o_ref, gn_ref) in enumerate(((q_ref, qg_ref), (k_ref, kg_ref))):
        acc = jnp.dot(h, w_ref[:, idx * D:(idx + 1) * D], preferred_element_type=F32)
        ss = jnp.dot((acc * acc).astype(BF16), g1_ref[...], preferred_element_type=F32)
        rstd = lax.rsqrt(ss * (1.0 / HD) + EPS)
        hi = rstd.astype(BF16)
        lo = (rstd - hi.astype(F32)).astype(BF16)
        rb = (jnp.dot(hi, g2_ref[...], preferred_element_type=F32)
              + jnp.dot(lo, g2_ref[...], preferred_element_type=F32))
        y = (acc * rb * gn_ref[...]).astype(BF16)
        for p in range(NPAIR):
            o_ref[p] = y[:, p * LANES:(p + 1) * LANES]
    v = jnp.dot(h, w_ref[:, 2 * D:], preferred_element_type=F32).astype(BF16)
    for p in range(NPAIR):
        v_ref[p] = v[:, p * LANES:(p + 1) * LANES]


def _qkv_call(xs, mods, layer, gain, w, qg, kg, g1, g2):
    out = jax.ShapeDtypeStruct((NPAIR, NROWS, LANES), BF16)
    return pl.pallas_call(
        _qkv_kernel,
        grid=(NROWS // TM,),
        in_specs=[
            _row_spec(), _mods_spec(layer, 0), _mods_spec(layer, 1), _const_spec((1, D)),
            _const_spec((D, 3 * D)), _const_spec((1, D)), _const_spec((1, D)),
            _const_spec((D, LANES)), _const_spec((LANES, D)),
        ],
        out_specs=[_pair_spec(), _pair_spec(), _pair_spec()],
        out_shape=[out, out, out],
        compiler_params=_cparams(1),
        name="na_qkv",
    )(xs, mods, mods, gain, w, qg, kg, g1, g2)


def _dot_t(a, b):
    return lax.dot_general(a, b, (((1,), (1,)), ((), ())), preferred_element_type=F32)


def _attn_kernel(q_ref, kc_ref, vc_ref, k0_ref, k1_ref, k2_ref, v0_ref, v1_ref, v2_ref,
                 t2_ref, o_ref, mask_ref):
    jj = pl.program_id(1)
    rg = jj - 1
    win0 = jnp.clip(rg, 1, NQB - 3) - 1
    qi = lax.shift_right_logical(lax.broadcasted_iota(jnp.int32, (QB, 1), 0), 6)
    ku = lax.shift_right_logical(lax.broadcasted_iota(jnp.int32, (1, 3 * QB), 1), 6)
    qrow = 4 * rg + qi
    rs = jnp.clip(qrow - WIN_ROWS // 2, 0, IMG_ROWS - WIN_ROWS)
    krow = 4 * win0 + ku
    ok = (krow >= rs) & (krow < rs + WIN_ROWS) & (jj > 0)
    neg = jnp.where(ok, 0.0, NEG_INF).astype(F32)
    for j in range(3):
        mask_ref[j] = neg[:, j * QB:(j + 1) * QB]
    off = 4 * win0 - 4 * rg + (WIN_ROWS - 1)
    lane = lax.broadcasted_iota(jnp.int32, (1, LANES), 1)
    k_refs = (k0_ref, k1_ref, k2_ref)
    v_refs = (v0_ref, v1_ref, v2_ref)

    def pair_body(p, carry):
        qp = q_ref[p]
        kcp = kc_ref[p]
        vcp = vc_ref[p]
        outs = []
        for hh in range(2):
            head = 2 * p + hh
            sel = (lane < HD) if hh == 0 else (lane >= HD)
            qm = jnp.where(sel, qp, jnp.zeros_like(qp))
            s_c = _dot_t(qm, kcp)
            s_l = []
            for j in range(3):
                s = _dot_t(qm, k_refs[j][p])
                rows = []
                for i in range(4):
                    tiles = []
                    for cpair in range(2):
                        d = jnp.clip(2 * (2 * j + cpair) - i + off, -1, 14) + 1
                        tiles.append(t2_ref[head, d])
                    rows.append(jnp.concatenate(tiles, axis=1))
                s_l.append(s + jnp.concatenate(rows, axis=0) + mask_ref[j])
            m = jnp.max(s_c, axis=-1, keepdims=True)
            for s in s_l:
                m = jnp.maximum(m, jnp.max(s, axis=-1, keepdims=True))
            e_c = jnp.exp(s_c - m)
            den = jnp.sum(e_c, axis=-1, keepdims=True)
            acc = jnp.dot(e_c.astype(BF16), vcp, preferred_element_type=F32)
            for j in range(3):
                e = jnp.exp(s_l[j] - m)
                den = den + jnp.sum(e, axis=-1, keepdims=True)
                acc = acc + jnp.dot(e.astype(BF16), v_refs[j][p], preferred_element_type=F32)
            outs.append(acc / den)
        o_ref[p] = jnp.where(lane < HD, outs[0], outs[1]).astype(BF16)
        return carry

    lax.fori_loop(0, NPAIR, pair_body, 0)


def _attn_call(q, k, v, t2):
    def blk(fn):
        return pl.BlockSpec((NPAIR, QB, LANES), lambda b, jj: (0, fn(b, jj), 0))

    def win(o):
        return lambda b, jj: b * NQB + jnp.clip(jj - 1, 1, NQB - 3) + o

    return pl.pallas_call(
        _attn_kernel,
        grid=(B, NQB),
        in_specs=[
            blk(lambda b, jj: b * NQB + jj),
            blk(lambda b, jj: b * NQB), blk(lambda b, jj: b * NQB),
            blk(win(0)), blk(win(1)), blk(win(2)),
            blk(win(0)), blk(win(1)), blk(win(2)),
            _const_spec((H, 16, GRID_W, LANES)),
        ],
        out_specs=blk(lambda b, jj: b * NQB + jj),
        out_shape=jax.ShapeDtypeStruct((NPAIR, NROWS, LANES), BF16),
        scratch_shapes=[pltpu.VMEM((3, QB, QB), F32)],
        compiler_params=_cparams(2),
        name="na_attention",
    )(q, k, v, k, k, k, v, v, v, t2)


def _bias_table(rpb):
    qc = jnp.arange(GRID_W)[:, None]
    kc = jnp.arange(GRID_W)[None, :]
    ws = jnp.clip(qc - WIN_COLS // 2, 0, GRID_W - WIN_COLS)
    col_ok = (kc >= ws) & (kc < ws + WIN_COLS)
    d_col = jnp.clip(kc - qc + (WIN_COLS - 1), 0, 2 * WIN_COLS - 2)
    t = jnp.where(col_ok[None, None], rpb[:, :, d_col], NEG_INF)
    zero = jnp.where(col_ok, 0.0, NEG_INF).astype(F32)
    zero = jnp.broadcast_to(zero[None, None], (H, 1, GRID_W, GRID_W))
    t_ext = jnp.concatenate([zero, t, zero], axis=1)
    return jnp.concatenate([t_ext[:, :-1], t_ext[:, 1:]], axis=-1)


def _proj_kernel(x_ref, y_ref, gate_ref, w_ref, o_ref, *, pair_major):
    b, pos = _tile_pos(TM, 0)
    g = _mod_rows(gate_ref, b, pos < C)
    if pair_major:
        y = jnp.concatenate([y_ref[p] for p in range(NPAIR)], axis=1)
    else:
        y = y_ref[...]
    o_ref[...] = x_ref[...] + g * jnp.dot(y, w_ref[...], preferred_element_type=F32)


def _proj_call(xs, y, mods, layer, w, pair_major):
    return pl.pallas_call(
        functools.partial(_proj_kernel, pair_major=pair_major),
        grid=(NROWS // TM,),
        in_specs=[
            _row_spec(), _pair_spec() if pair_major else _row_spec(),
            _mods_spec(layer, 2), _const_spec((D, D)),
        ],
        out_specs=_row_spec(),
        out_shape=jax.ShapeDtypeStruct((NROWS, D), F32),
        compiler_params=_cparams(1),
        name="mixer_out_proj",
    )(xs, y, mods, w)


def _lru_in_kernel(x_ref, sh_ref, sc_ref, gain_ref, w_ref, gg_ref, rec_ref):
    b, pos = _tile_pos(TM, 0)
    is_ctx = pos < C
    h = _modulate(x_ref[...], gain_ref[...], _mod_rows(sh_ref, b, is_ctx),
                  _mod_rows(sc_ref, b, is_ctx)).astype(BF16)
    gate = jnp.dot(h, w_ref[:, :D], preferred_element_type=F32)
    gg_ref[...] = jax.nn.gelu(gate, approximate=True).astype(BF16)
    rec_ref[...] = jnp.dot(h, w_ref[:, D:], preferred_element_type=F32)


def _lru_in_call(xs, mods, layer, gain, w):
    return pl.pallas_call(
        _lru_in_kernel,
        grid=(NROWS // TM,),
        in_specs=[
            _row_spec(), _mods_spec(layer, 0), _mods_spec(layer, 1), _const_spec((1, D)),
            _const_spec((D, 2 * D)),
        ],
        out_specs=[_row_spec(), _row_spec()],
        out_shape=[jax.ShapeDtypeStruct((NROWS, D), BF16),
                   jax.ShapeDtypeStruct((NROWS, D), F32)],
        compiler_params=_cparams(1),
        name="lru_in_proj",
    )(xs, mods, mods, gain, w)


def _chunk_index(g, reverse):
    if not reverse:
        return g
    return jnp.where(g < CCH, CCH - 1 - g, NCH + CCH - 1 - g)


def _scan_kernel(rec_ref, recp_ref, recn_ref, cw_ref, cb_ref, lam_ref, wg_ref, ba_ref, bx_ref,
                 *rest, reverse):
    if reverse:
        o_ref, xr_scr, a_scr, b_scr, h_scr = rest
    else:
        hsb_ref, gg_ref, o_ref, xr_scr, a_scr, b_scr, h_scr = rest
    g = pl.program_id(0)
    c = _chunk_index(g, reverse)
    has_prev = (c != 0) & (c != CCH)
    has_next = (c != CCH - 1) & (c != NCH - 1)

    @pl.when(g == 0)
    def _():
        h_scr[...] = jnp.zeros_like(h_scr)

    cw = cw_ref[...]
    cb = cb_ref[...]
    for b in range(B):
        prev = jnp.where(has_prev, recp_ref[b], 0.0)
        nxt = jnp.where(has_next, recn_ref[b], 0.0)
        ext = jnp.concatenate([prev, rec_ref[b], nxt], axis=0)
        xr = cb + ext[6:6 + TS] * cw[0:1]
        for t in range(1, 4):
            xr = xr + ext[6 + t:6 + t + TS] * cw[t:t + 1]
        xr_scr[b * TS:(b + 1) * TS, :] = xr

    lam = lam_ref[...]
    neg_c_sp = -LRU_C * (jnp.maximum(-lam, 0.0) + jnp.log1p(jnp.exp(-jnp.abs(lam))))
    for gq in range(D // GW):
        cols = slice(gq * GW, (gq + 1) * GW)
        xg = xr_scr[:, cols]
        pre = jnp.dot(xg.astype(BF16), wg_ref[gq], preferred_element_type=F32)
        r = 0.5 * jnp.tanh(0.5 * (pre[:, :GW] + ba_ref[:, cols])) + 0.5
        i_ = 0.5 * jnp.tanh(0.5 * (pre[:, GW:] + bx_ref[:, cols])) + 0.5
        log_a = neg_c_sp[:, cols] * r
        a = jnp.exp(log_a)
        bb = jnp.sqrt((1.0 + a * a) * jnp.tanh(-log_a)) * (i_ * xg)
        for k2 in range(GW // LANES):
            k = gq * (GW // LANES) + k2
            for b in range(B):
                rows = slice(k * B * PITCH + b * PITCH, k * B * PITCH + b * PITCH + TS)
                a_scr[rows, :] = a[b * TS:(b + 1) * TS, k2 * LANES:(k2 + 1) * LANES]
                b_scr[rows, :] = bb[b * TS:(b + 1) * TS, k2 * LANES:(k2 + 1) * LANES]

    nslab = D // LANES

    def step(s, hs):
        t = (TS - 1 - s) if reverse else s
        out = []
        for k in range(nslab):
            idx = pl.ds(k * B * PITCH + t, B, stride=PITCH)
            hk = a_scr[idx, :] * hs[k] + b_scr[idx, :]
            b_scr[idx, :] = hk
            out.append(hk)
        return tuple(out)

    h_fin = lax.fori_loop(0, TS, step, tuple(h_scr[k] for k in range(nslab)))
    for k in range(nslab):
        h_scr[k] = h_fin[k]

    for b in range(B):
        for k in range(nslab):
            rows = slice(k * B * PITCH + b * PITCH, k * B * PITCH + b * PITCH + TS)
            lanes = slice(k * LANES, (k + 1) * LANES)
            hs = b_scr[rows, :]
            if reverse:
                o_ref[b, :, lanes] = hs
            else:
                z = (hs + hsb_ref[b, :, lanes]) * gg_ref[b, :, lanes].astype(F32)
                o_ref[b, :, lanes] = z.astype(BF16)


def _scan_call(rec3, cw, cb, lam, wg, ba, bx, hsb, gg, reverse):
    def cidx(g):
        return _chunk_index(g, reverse)

    main = pl.BlockSpec((B, TS, D), lambda g: (0, cidx(g), 0))
    in_specs = [
        main,
        pl.BlockSpec((B, 8, D), lambda g: (0, jnp.maximum(cidx(g) * (TS // 8) - 1, 0), 0)),
        pl.BlockSpec((B, 8, D),
                     lambda g: (0, jnp.minimum((cidx(g) + 1) * (TS // 8), LT // 8 - 1), 0)),
        _const_spec((4, D)), _const_spec((1, D)), _const_spec((1, D)),
        _const_spec((D // GW, GW, 2 * GW)), _const_spec((1, D)), _const_spec((1, D)),
    ]
    args = [rec3, rec3, rec3, cw, cb, lam, wg, ba, bx]
    if not reverse:
        in_specs += [main, main]
        args += [hsb, gg]
    slab = pltpu.VMEM((D // LANES * B * PITCH, LANES), F32)
    return pl.pallas_call(
        functools.partial(_scan_kernel, reverse=reverse),
        grid=(NCH,),
        in_specs=in_specs,
        out_specs=main,
        out_shape=jax.ShapeDtypeStruct((B, LT, D), F32 if reverse else BF16),
        scratch_shapes=[pltpu.VMEM((B * TS, D), F32), slab, slab,
                        pltpu.VMEM((D // LANES, B, LANES), F32)],
        compiler_params=_cparams(1),
        name="lru_scan_bwd" if reverse else "lru_scan_fwd",
    )(*args)


def _gate_weights(wa, wx):
    def bd(w):
        w4 = w.reshape(D // GW, GW // LRU_BLOCK_W, LRU_BLOCK_W, LRU_BLOCK_W)
        eye = jnp.eye(GW // LRU_BLOCK_W, dtype=w.dtype)
        return jnp.einsum("gnde,nm->gndme", w4, eye).reshape(D // GW, GW, GW)
    return jnp.concatenate([bd(wa), bd(wx)], axis=-1).astype(BF16)


def _ffn_kernel(xp_ref, x_ref, xn_ref, sh_ref, sc_ref, gt_ref, gain_ref, wv_ref, wg_ref,
                cwv_ref, cwg_ref, cbv_ref, cbg_ref, wd_ref, o_ref):
    b, pos = _tile_pos(TM, 0)
    _, pos_e = _tile_pos(TM + 16, -8)
    first = (pos == 0) | (pos == C)
    last = (pos == C - 1) | (pos == LT - 1)
    is_ctx_e = pos_e < C
    x = x_ref[...]
    x_e = jnp.concatenate([xp_ref[...], x, xn_ref[...]], axis=0)
    h = _modulate(x_e, gain_ref[...], _mod_rows(sh_ref, b, is_ctx_e),
                  _mod_rows(sc_ref, b, is_ctx_e)).astype(BF16)

    def conv(u, cw, cb):
        y = cb + u[8:8 + TM] * cw[1:2]
        y = y + jnp.where(first, 0.0, u[7:7 + TM]) * cw[0:1]
        return y + jnp.where(last, 0.0, u[9:9 + TM]) * cw[2:3]

    acc = jnp.zeros((TM, D), F32)
    for ci in range(D_FF // FC):
        val = conv(jnp.dot(h, wv_ref[ci], preferred_element_type=F32), cwv_ref[ci], cbv_ref[ci])
        gate = conv(jnp.dot(h, wg_ref[ci], preferred_element_type=F32), cwg_ref[ci], cbg_ref[ci])
        act = (val * (gate * jax.nn.sigmoid(gate))).astype(BF16)
        acc = acc + jnp.dot(act, wd_ref[ci], preferred_element_type=F32)
    o_ref[...] = x + _mod_rows(gt_ref, b, pos < C) * acc


def _ffn_call(xs, mods, layer, gain, wv, wg, cwv, cwg, cbv, cbg, wd):
    nc = D_FF // FC
    t8 = TM // 8
    return pl.pallas_call(
        _ffn_kernel,
        grid=(NROWS // TM,),
        in_specs=[
            pl.BlockSpec((8, D), lambda i: (jnp.maximum(i * t8 - 1, 0), 0)),
            _row_spec(),
            pl.BlockSpec((8, D), lambda i: (jnp.minimum((i + 1) * t8, NROWS // 8 - 1), 0)),
            _mods_spec(layer, 3), _mods_spec(layer, 4), _mods_spec(layer, 5),
            _const_spec((1, D)),
            _const_spec((nc, D, FC)), _const_spec((nc, D, FC)),
            _const_spec((nc, 3, FC)), _const_spec((nc, 3, FC)),
            _const_spec((nc, 1, FC)), _const_spec((nc, 1, FC)),
            _const_spec((nc, FC, D)),
        ],
        out_specs=_row_spec(),
        out_shape=jax.ShapeDtypeStruct((NROWS, D), F32),
        compiler_params=_cparams(1),
        name="conv_ffn",
    )(xs, xs, xs, mods, mods, mods, gain, wv, wg, cwv, cwg, cbv, cbg, wd)


def _chunk_cols(w):
    return w.reshape(w.shape[0], D_FF // FC, FC).transpose(1, 0, 2)


def kernel(x, c, ctx, c_ctx, ada_w, ada_b, norm_mix, norm_ffn, na_w_qkv, na_q_gain, na_k_gain,
           na_rpb, na_w_out, lru_w_in, lru_conv_w, lru_conv_b, lru_ga_w, lru_ga_b, lru_gx_w,
           lru_gx_b, lru_lambda, lru_w_out, ffn_w_up, ffn_conv_w, ffn_conv_b, ffn_w_down):
    xs = jnp.concatenate([ctx, x], axis=1).reshape(NROWS, D)
    cc = jnp.concatenate([c, c_ctx[None], jnp.zeros((7, D), F32)], axis=0)
    mods = _mods_call(cc, ada_w, ada_b)

    head_of_lane = jnp.arange(D) // HD
    g1 = (head_of_lane[:, None] == jnp.arange(LANES)[None, :]).astype(BF16)
    g2 = g1.T

    for i in range(DEPTH):
        j = i // 2
        gain_mix = norm_mix[i][None]
        if i % 2 == 0:
            qg = (jnp.tile(na_q_gain[j], H) * (HD ** -0.5))[None]
            kg = jnp.tile(na_k_gain[j], H)[None]
            q, k, v = _qkv_call(xs, mods, i, gain_mix, na_w_qkv[j].astype(BF16), qg, kg, g1, g2)
            o = _attn_call(q, k, v, _bias_table(na_rpb[j]))
            xs = _proj_call(xs, o, mods, i, na_w_out[j].astype(BF16), True)
        else:
            gg, rec = _lru_in_call(xs, mods, i, gain_mix, lru_w_in[j].astype(BF16))
            rec3 = rec.reshape(B, LT, D)
            gg3 = gg.reshape(B, LT, D)
            cw, cb = lru_conv_w[j], lru_conv_b[j][None]
            hsb = _scan_call(rec3, cw, cb, lru_lambda[j, 1][None],
                             _gate_weights(lru_ga_w[j, 1], lru_gx_w[j, 1]),
                             lru_ga_b[j, 1][None], lru_gx_b[j, 1][None], None, None, True)
            z = _scan_call(rec3, cw, cb, lru_lambda[j, 0][None],
                           _gate_weights(lru_ga_w[j, 0], lru_gx_w[j, 0]),
                           lru_ga_b[j, 0][None], lru_gx_b[j, 0][None], hsb, gg3, False)
            xs = _proj_call(xs, z.reshape(NROWS, D), mods, i, lru_w_out[j].astype(BF16), False)
        wup = ffn_w_up[i].astype(BF16)
        cwf, cbf = ffn_conv_w[i], ffn_conv_b[i][None]
        xs = _ffn_call(
            xs, mods, i, norm_ffn[i][None],
            _chunk_cols(wup[:, :D_FF]), _chunk_cols(wup[:, D_FF:]),
            _chunk_cols(cwf[:, :D_FF]), _chunk_cols(cwf[:, D_FF:]),
            _chunk_cols(cbf[:, :D_FF]), _chunk_cols(cbf[:, D_FF:]),
            ffn_w_down[i].astype(BF16).reshape(D_FF // FC, FC, D))
    return xs.reshape(B, LT, D)[:, C:]
```

```python
import functools

import jax
import jax.numpy as jnp
from jax import lax
from jax.experimental import pallas as pl
from jax.experimental.pallas import tpu as pltpu

F32 = jnp.float32
BF16 = jnp.bfloat16

D = 1024
B = 8
S = 4096
DEPTH = 4
C = 256
LT = C + S
NROWS = B * LT
GRID_W = 64
IMG_ROWS = S // GRID_W
H = 16
HD = 64
NPAIR = H // 2
LANES = 128
WIN_ROWS = 8
WIN_COLS = 16
D_FF = 3 * D
LRU_BLOCK_W = 64
LRU_C = 8.0
EPS = 1e-6
NEG_INF = float("-inf")

TM = 544
TPB = LT // TM
FC = 512
QB = 256
NQB = LT // QB
TS = 128
NCH = LT // TS
CCH = C // TS
PITCH = TS + 4
GW = 256
VMEM_LIMIT = 56 * 1024 * 1024

T2_LEFT_DEAD = 16
T2_RIGHT_DEAD = 17
T2_DEAD = 18
T2_ENTRIES = 19


def _cparams(n_axes):
    return pltpu.CompilerParams(
        dimension_semantics=("arbitrary",) * n_axes, vmem_limit_bytes=VMEM_LIMIT)


def _const_spec(shape):
    nd = len(shape)
    return pl.BlockSpec(shape, lambda *_: (0,) * nd, pipeline_mode=pl.Buffered(1))


def _mods_kernel(cc_ref, w_ref, b_ref, o_ref):
    c = cc_ref[...]
    s = c * jax.nn.sigmoid(c)
    o_ref[...] = jnp.dot(s.astype(BF16), w_ref[...].astype(BF16),
                         preferred_element_type=F32) + b_ref[...]


def _mods_call(cc, ada_w, ada_b):
    return pl.pallas_call(
        _mods_kernel,
        grid=(DEPTH, 6),
        in_specs=[
            pl.BlockSpec((16, D), lambda l, k: (0, 0)),
            pl.BlockSpec((None, D, D), lambda l, k: (l, 0, k)),
            pl.BlockSpec((None, None, 1, D), lambda l, k: (l, k, 0, 0)),
        ],
        out_specs=pl.BlockSpec((None, None, 16, D), lambda l, k: (l, k, 0, 0)),
        out_shape=jax.ShapeDtypeStruct((DEPTH, 6, 16, D), F32),
        compiler_params=_cparams(2),
        name="adaln_mods",
    )(cc, ada_w, ada_b.reshape(DEPTH, 6, 1, D))


def _tile_pos(n, lo):
    i = pl.program_id(0)
    b = i // TPB
    j = i % TPB
    pos = j * TM + lo + lax.broadcasted_iota(jnp.int32, (n, 1), 0)
    return b, pos


def _mod_rows(m_ref, b, is_ctx):
    return jnp.where(is_ctx, m_ref[8:9, :], m_ref[pl.ds(b, 1), :])


def _modulate(x, gain, shift, scale):
    y = x * lax.rsqrt(jnp.mean(x * x, axis=-1, keepdims=True) + EPS)
    return (y * gain) * (1.0 + scale) + shift


def _mods_spec(layer, kind):
    return pl.BlockSpec((None, None, 16, D), lambda *_: (layer, kind, 0, 0))


def _row_spec(width=D):
    return pl.BlockSpec((TM, width), lambda i: (i, 0))


def _pair_spec():
    return pl.BlockSpec((NPAIR, TM, LANES), lambda i: (0, i, 0))


def _qkv_kernel(x_ref, sh_ref, sc_ref, gain_ref, w_ref, qg_ref, kg_ref, g1_ref, g2_ref,
                q_ref, k_ref, v_ref):
    b, pos = _tile_pos(TM, 0)
    is_ctx = pos < C
    h = _modulate(x_ref[...], gain_ref[...], _mod_rows(sh_ref, b, is_ctx),
                  _mod_rows(sc_ref, b, is_ctx)).astype(BF16)
    for idx, (o_ref, gn_ref) in enumerate(((q_ref, qg_ref), (k_ref, kg_ref))):
        acc = jnp.dot(h, w_ref[:, idx * D:(idx + 1) * D], preferred_element_type=F32)
        ss = jnp.dot((acc * acc).astype(BF16), g1_ref[...], preferred_element_type=F32)
        rstd = lax.rsqrt(ss * (1.0 / HD) + EPS)
        hi = rstd.astype(BF16)
        lo = (rstd - hi.astype(F32)).astype(BF16)
        rb = (jnp.dot(hi, g2_ref[...], preferred_element_type=F32)
              + jnp.dot(lo, g2_ref[...], preferred_element_type=F32))
        y = (acc * rb * gn_ref[...]).astype(BF16)
        for p in range(NPAIR):
            o_ref[p] = y[:, p * LANES:(p + 1) * LANES]
    v = jnp.dot(h, w_ref[:, 2 * D:], preferred_element_type=F32).astype(BF16)
    first_head = lax.broadcasted_iota(jnp.int32, (1, LANES), 1) < HD
    one = jnp.ones((), BF16)
    for p in range(NPAIR):
        vp = v[:, p * LANES:(p + 1) * LANES]
        v_ref[0, p] = jnp.where(first_head, vp, one)
        v_ref[1, p] = jnp.where(first_head, one, vp)


def _qkv_call(xs, mods, layer, gain, w, qg, kg, g1, g2):
    out = jax.ShapeDtypeStruct((NPAIR, NROWS, LANES), BF16)
    return pl.pallas_call(
        _qkv_kernel,
        grid=(NROWS // TM,),
        in_specs=[
            _row_spec(), _mods_spec(layer, 0), _mods_spec(layer, 1), _const_spec((1, D)),
            _const_spec((D, 3 * D)), _const_spec((1, D)), _const_spec((1, D)),
            _const_spec((D, LANES)), _const_spec((LANES, D)),
        ],
        out_specs=[_pair_spec(), _pair_spec(),
                   pl.BlockSpec((2, NPAIR, TM, LANES), lambda i: (0, 0, i, 0))],
        out_shape=[out, out, jax.ShapeDtypeStruct((2, NPAIR, NROWS, LANES), BF16)],
        compiler_params=_cparams(1),
        name="na_qkv",
    )(xs, mods, mods, gain, w, qg, kg, g1, g2)


def _dot_t(a, b):
    return lax.dot_general(a, b, (((1,), (1,)), ((), ())), preferred_element_type=F32)


def _attn_kernel(q_ref, kc_ref, vc_ref, k0_ref, k1_ref, k2_ref, v0_ref, v1_ref, v2_ref,
                 t2_ref, o_ref):
    jj = pl.program_id(1)
    rg = jj - 1
    win0 = jnp.clip(rg, 1, NQB - 3) - 1
    off = 4 * win0 - 4 * rg + (WIN_ROWS - 1)
    tile_idx = []
    for i in range(4):
        rs = jnp.clip(4 * rg + i - WIN_ROWS // 2, 0, IMG_ROWS - WIN_ROWS)
        row = []
        for pu in range(6):
            kl = 4 * win0 + 2 * pu
            lv = (kl >= rs) & (kl < rs + WIN_ROWS) & (jj > 0)
            rv = (kl + 1 >= rs) & (kl + 1 < rs + WIN_ROWS) & (jj > 0)
            d = jnp.clip(2 * pu - i + off, -1, 14) + 1
            row.append(jnp.where(lv & rv, d, jnp.where(
                rv, T2_LEFT_DEAD, jnp.where(lv, T2_RIGHT_DEAD, T2_DEAD))))
        tile_idx.append(row)
    lane = lax.broadcasted_iota(jnp.int32, (1, LANES), 1)
    k_refs = (k0_ref, k1_ref, k2_ref)
    v_refs = (v0_ref, v1_ref, v2_ref)

    for p in range(NPAIR):
        qp = q_ref[p]
        kcp = kc_ref[p]
        outs = []
        for hh in range(2):
            head = 2 * p + hh
            sel = (lane < HD) if hh == 0 else (lane >= HD)
            qm = jnp.where(sel, qp, jnp.zeros_like(qp))
            s_all = [_dot_t(qm, kcp)]
            for j in range(3):
                bias = jnp.concatenate(
                    [jnp.concatenate([t2_ref[head, tile_idx[i][2 * j + cp]] for cp in range(2)],
                                     axis=1) for i in range(4)], axis=0)
                s_all.append(_dot_t(qm, k_refs[j][p]) + bias)
            mm = jnp.maximum(jnp.maximum(s_all[0], s_all[1]), jnp.maximum(s_all[2], s_all[3]))
            m = jnp.max(jnp.maximum(mm[:, :LANES], mm[:, LANES:]), axis=-1, keepdims=True)
            acc = jnp.dot(jnp.exp(s_all[0] - m).astype(BF16), vc_ref[hh, p],
                          preferred_element_type=F32)
            for j in range(3):
                acc = acc + jnp.dot(jnp.exp(s_all[j + 1] - m).astype(BF16), v_refs[j][hh, p],
                                    preferred_element_type=F32)
            outs.append(acc / pltpu.roll(acc, HD, axis=1))
        o_ref[p] = jnp.where(lane < HD, outs[0], outs[1]).astype(BF16)


def _attn_call(q, k, v, t2):
    def blk(fn):
        return pl.BlockSpec((NPAIR, QB, LANES), lambda b, jj: (0, fn(b, jj), 0))

    def vblk(fn):
        return pl.BlockSpec((2, NPAIR, QB, LANES), lambda b, jj: (0, 0, fn(b, jj), 0))

    def win(o):
        return lambda b, jj: b * NQB + jnp.clip(jj - 1, 1, NQB - 3) + o

    return pl.pallas_call(
        _attn_kernel,
        grid=(B, NQB),
        in_specs=[
            blk(lambda b, jj: b * NQB + jj),
            blk(lambda b, jj: b * NQB), vblk(lambda b, jj: b * NQB),
            blk(win(0)), blk(win(1)), blk(win(2)),
            vblk(win(0)), vblk(win(1)), vblk(win(2)),
            _const_spec((H, T2_ENTRIES, GRID_W, LANES)),
        ],
        out_specs=blk(lambda b, jj: b * NQB + jj),
        out_shape=jax.ShapeDtypeStruct((NPAIR, NROWS, LANES), BF16),
        compiler_params=_cparams(2),
        name="na_attention",
    )(q, k, v, k, k, k, v, v, v, t2)


def _bias_table(rpb):
    qc = jnp.arange(GRID_W)[:, None]
    kc = jnp.arange(GRID_W)[None, :]
    ws = jnp.clip(qc - WIN_COLS // 2, 0, GRID_W - WIN_COLS)
    col_ok = (kc >= ws) & (kc < ws + WIN_COLS)
    d_col = jnp.clip(kc - qc + (WIN_COLS - 1), 0, 2 * WIN_COLS - 2)
    t = jnp.where(col_ok[None, None], rpb[:, :, d_col], NEG_INF)
    dead = jnp.full((H, 1, GRID_W, GRID_W), NEG_INF, F32)
    t_ext = jnp.concatenate([dead, t, dead], axis=1)
    pairs = jnp.concatenate([t_ext[:, :-1], t_ext[:, 1:]], axis=-1)
    lo, hi = WIN_ROWS // 2 - 1, WIN_ROWS // 2 - 1 + WIN_ROWS - 1
    left_dead = jnp.concatenate([dead, t[:, lo:lo + 1]], axis=-1)
    right_dead = jnp.concatenate([t[:, hi:hi + 1], dead], axis=-1)
    all_dead = jnp.concatenate([dead, dead], axis=-1)
    return jnp.concatenate([pairs, left_dead, right_dead, all_dead], axis=1)


def _proj_kernel(x_ref, y_ref, gate_ref, w_ref, o_ref, *, pair_major):
    b, pos = _tile_pos(TM, 0)
    g = _mod_rows(gate_ref, b, pos < C)
    if pair_major:
        y = jnp.concatenate([y_ref[p] for p in range(NPAIR)], axis=1)
    else:
        y = y_ref[...]
    o_ref[...] = x_ref[...] + g * jnp.dot(y, w_ref[...], preferred_element_type=F32)


def _proj_call(xs, y, mods, layer, w, pair_major):
    return pl.pallas_call(
        functools.partial(_proj_kernel, pair_major=pair_major),
        grid=(NROWS // TM,),
        in_specs=[
            _row_spec(), _pair_spec() if pair_major else _row_spec(),
            _mods_spec(layer, 2), _const_spec((D, D)),
        ],
        out_specs=_row_spec(),
        out_shape=jax.ShapeDtypeStruct((NROWS, D), F32),
        compiler_params=_cparams(1),
        name="mixer_out_proj",
    )(xs, y, mods, w)


def _lru_in_kernel(x_ref, sh_ref, sc_ref, gain_ref, w_ref, gg_ref, rec_ref):
    b, pos = _tile_pos(TM, 0)
    is_ctx = pos < C
    h = _modulate(x_ref[...], gain_ref[...], _mod_rows(sh_ref, b, is_ctx),
                  _mod_rows(sc_ref, b, is_ctx)).astype(BF16)
    gate = jnp.dot(h, w_ref[:, :D], preferred_element_type=F32)
    gg_ref[...] = jax.nn.gelu(gate, approximate=True).astype(BF16)
    rec_ref[...] = jnp.dot(h, w_ref[:, D:], preferred_element_type=F32)


def _lru_in_call(xs, mods, layer, gain, w):
    return pl.pallas_call(
        _lru_in_kernel,
        grid=(NROWS // TM,),
        in_specs=[
            _row_spec(), _mods_spec(layer, 0), _mods_spec(layer, 1), _const_spec((1, D)),
            _const_spec((D, 2 * D)),
        ],
        out_specs=[_row_spec(), _row_spec()],
        out_shape=[jax.ShapeDtypeStruct((NROWS, D), BF16),
                   jax.ShapeDtypeStruct((NROWS, D), F32)],
        compiler_params=_cparams(1),
        name="lru_in_proj",
    )(xs, mods, mods, gain, w)


def _chunk_index(g, reverse):
    if not reverse:
        return g
    return jnp.where(g < CCH, CCH - 1 - g, NCH + CCH - 1 - g)


def _scan_kernel(rec_ref, recp_ref, recn_ref, cw_ref, cb_ref, lam_ref, wg_ref, ba_ref, bx_ref,
                 *rest, reverse):
    if reverse:
        o_ref, xr_scr, a_scr, b_scr, h_scr = rest
    else:
        hsb_ref, gg_ref, o_ref, xr_scr, a_scr, b_scr, h_scr = rest
    g = pl.program_id(0)
    c = _chunk_index(g, reverse)
    has_prev = (c != 0) & (c != CCH)
    has_next = (c != CCH - 1) & (c != NCH - 1)

    @pl.when(g == 0)
    def _():
        h_scr[...] = jnp.zeros_like(h_scr)

    cw = cw_ref[...]
    cb = cb_ref[...]
    for b in range(B):
        prev = jnp.where(has_prev, recp_ref[b], 0.0)
        nxt = jnp.where(has_next, recn_ref[b], 0.0)
        ext = jnp.concatenate([prev, rec_ref[b], nxt], axis=0)
        xr = cb + ext[6:6 + TS] * cw[0:1]
        for t in range(1, 4):
            xr = xr + ext[6 + t:6 + t + TS] * cw[t:t + 1]
        xr_scr[b * TS:(b + 1) * TS, :] = xr

    lam = lam_ref[...]
    neg_c_sp = -LRU_C * (jnp.maximum(-lam, 0.0) + jnp.log1p(jnp.exp(-jnp.abs(lam))))
    for gq in range(D // GW):
        cols = slice(gq * GW, (gq + 1) * GW)
        xg = xr_scr[:, cols]
        pre = jnp.dot(xg.astype(BF16), wg_ref[gq], preferred_element_type=F32)
        r = 0.5 * jnp.tanh(0.5 * (pre[:, :GW] + ba_ref[:, cols])) + 0.5
        i_ = 0.5 * jnp.tanh(0.5 * (pre[:, GW:] + bx_ref[:, cols])) + 0.5
        log_a = neg_c_sp[:, cols] * r
        a = jnp.exp(log_a)
        bb = jnp.sqrt((1.0 + a * a) * jnp.tanh(-log_a)) * (i_ * xg)
        for k2 in range(GW // LANES):
            k = gq * (GW // LANES) + k2
            for b in range(B):
                rows = slice(k * B * PITCH + b * PITCH, k * B * PITCH + b * PITCH + TS)
                a_scr[rows, :] = a[b * TS:(b + 1) * TS, k2 * LANES:(k2 + 1) * LANES]
                b_scr[rows, :] = bb[b * TS:(b + 1) * TS, k2 * LANES:(k2 + 1) * LANES]

    nslab = D // LANES

    def step(s, hs):
        t = (TS - 1 - s) if reverse else s
        out = []
        for k in range(nslab):
            idx = pl.ds(k * B * PITCH + t, B, stride=PITCH)
            hk = a_scr[idx, :] * hs[k] + b_scr[idx, :]
            b_scr[idx, :] = hk
            out.append(hk)
        return tuple(out)

    h_fin = lax.fori_loop(0, TS, step, tuple(h_scr[k] for k in range(nslab)))
    for k in range(nslab):
        h_scr[k] = h_fin[k]

    for b in range(B):
        for k in range(nslab):
            rows = slice(k * B * PITCH + b * PITCH, k * B * PITCH + b * PITCH + TS)
            lanes = slice(k * LANES, (k + 1) * LANES)
            hs = b_scr[rows, :]
            if reverse:
                o_ref[b, :, lanes] = hs
            else:
                z = (hs + hsb_ref[b, :, lanes]) * gg_ref[b, :, lanes].astype(F32)
                o_ref[b, :, lanes] = z.astype(BF16)


def _scan_call(rec3, cw, cb, lam, wg, ba, bx, hsb, gg, reverse):
    def cidx(g):
        return _chunk_index(g, reverse)

    main = pl.BlockSpec((B, TS, D), lambda g: (0, cidx(g), 0))
    in_specs = [
        main,
        pl.BlockSpec((B, 8, D), lambda g: (0, jnp.maximum(cidx(g) * (TS // 8) - 1, 0), 0)),
        pl.BlockSpec((B, 8, D),
                     lambda g: (0, jnp.minimum((cidx(g) + 1) * (TS // 8), LT // 8 - 1), 0)),
        _const_spec((4, D)), _const_spec((1, D)), _const_spec((1, D)),
        _const_spec((D // GW, GW, 2 * GW)), _const_spec((1, D)), _const_spec((1, D)),
    ]
    args = [rec3, rec3, rec3, cw, cb, lam, wg, ba, bx]
    if not reverse:
        in_specs += [main, main]
        args += [hsb, gg]
    slab = pltpu.VMEM((D // LANES * B * PITCH, LANES), F32)
    return pl.pallas_call(
        functools.partial(_scan_kernel, reverse=reverse),
        grid=(NCH,),
        in_specs=in_specs,
        out_specs=main,
        out_shape=jax.ShapeDtypeStruct((B, LT, D), F32 if reverse else BF16),
        scratch_shapes=[pltpu.VMEM((B * TS, D), F32), slab, slab,
                        pltpu.VMEM((D // LANES, B, LANES), F32)],
        compiler_params=_cparams(1),
        name="lru_scan_bwd" if reverse else "lru_scan_fwd",
    )(*args)


def _gate_weights(wa, wx):
    def bd(w):
        w4 = w.reshape(D // GW, GW // LRU_BLOCK_W, LRU_BLOCK_W, LRU_BLOCK_W)
        eye = jnp.eye(GW // LRU_BLOCK_W, dtype=w.dtype)
        return jnp.einsum("gnde,nm->gndme", w4, eye).reshape(D // GW, GW, GW)
    return jnp.concatenate([bd(wa), bd(wx)], axis=-1).astype(BF16)


FSLAB = FC // LANES
BND = C % TM
BND_LO = BND - 8


def _ffn_kernel(xp_ref, x_ref, xn_ref, sh_ref, sc_ref, gt_ref, gain_ref, wv_ref, wg_ref,
                cwv_ref, cwg_ref, cbv_ref, cbg_ref, wd_ref, o_ref, uv_scr, ug_scr):
    b, pos = _tile_pos(TM, 0)
    _, pos_e = _tile_pos(TM + 16, -8)
    x = x_ref[...]
    x_e = jnp.concatenate([xp_ref[...], x, xn_ref[...]], axis=0)
    is_ctx_e = pos_e < C
    h = _modulate(x_e, gain_ref[...], _mod_rows(sh_ref, b, is_ctx_e),
                  _mod_rows(sc_ref, b, is_ctx_e))
    e = lax.broadcasted_iota(jnp.int32, (TM + 16, 1), 0)
    dead = ((e == 7) & ((pos_e == -1) | (pos_e == C - 1))) | (
        (e == TM + 8) & ((pos_e == C) | (pos_e == LT)))
    h = jnp.where(dead, 0.0, h).astype(BF16)
    first_m = pos[BND_LO:BND_LO + 16] == C
    last_m = pos[BND_LO:BND_LO + 16] == C - 1

    def conv(u_scr, s, cw, cb):
        def seg(lo, n, masked):
            prev = u_scr[s, pl.ds(lo + 7, n), :]
            nxt = u_scr[s, pl.ds(lo + 9, n), :]
            if masked:
                prev = jnp.where(first_m, 0.0, prev)
                nxt = jnp.where(last_m, 0.0, nxt)
            return cb + u_scr[s, pl.ds(lo + 8, n), :] * cw[1:2] + prev * cw[0:1] + nxt * cw[2:3]
        return jnp.concatenate([seg(0, BND_LO, False), seg(BND_LO, 16, True),
                                seg(BND_LO + 16, TM - BND_LO - 16, False)], axis=0)

    acc = jnp.zeros((TM, D), F32)
    for ci in range(D_FF // FC):
        uv = jnp.dot(h, wv_ref[ci], preferred_element_type=F32)
        ug = jnp.dot(h, wg_ref[ci], preferred_element_type=F32)
        base = (ci % 2) * FSLAB
        for s in range(FSLAB):
            uv_scr[base + s] = uv[:, s * LANES:(s + 1) * LANES]
            ug_scr[base + s] = ug[:, s * LANES:(s + 1) * LANES]
        cwv, cbv = cwv_ref[ci], cbv_ref[ci]
        cwg, cbg = 0.5 * cwg_ref[ci], 0.5 * cbg_ref[ci]
        acts = []
        for s in range(FSLAB):
            ls = slice(s * LANES, (s + 1) * LANES)
            val = conv(uv_scr, base + s, cwv[:, ls], cbv[:, ls])
            hg = conv(ug_scr, base + s, cwg[:, ls], cbg[:, ls])
            acts.append((val * (hg + hg * jnp.tanh(hg))).astype(BF16))
        acc = acc + jnp.dot(jnp.concatenate(acts, axis=1), wd_ref[ci],
                            preferred_element_type=F32)
    o_ref[...] = x + _mod_rows(gt_ref, b, pos < C) * acc


def _ffn_call(xs, mods, layer, gain, wv, wg, cwv, cwg, cbv, cbg, wd):
    nc = D_FF // FC
    t8 = TM // 8
    slab = pltpu.VMEM((2 * FSLAB, TM + 16, LANES), F32)
    return pl.pallas_call(
        _ffn_kernel,
        grid=(NROWS // TM,),
        in_specs=[
            pl.BlockSpec((8, D), lambda i: (jnp.maximum(i * t8 - 1, 0), 0)),
            _row_spec(),
            pl.BlockSpec((8, D), lambda i: (jnp.minimum((i + 1) * t8, NROWS // 8 - 1), 0)),
            _mods_spec(layer, 3), _mods_spec(layer, 4), _mods_spec(layer, 5),
            _const_spec((1, D)),
            _const_spec((nc, D, FC)), _const_spec((nc, D, FC)),
            _const_spec((nc, 3, FC)), _const_spec((nc, 3, FC)),
            _const_spec((nc, 1, FC)), _const_spec((nc, 1, FC)),
            _const_spec((nc, FC, D)),
        ],
        out_specs=_row_spec(),
        out_shape=jax.ShapeDtypeStruct((NROWS, D), F32),
        scratch_shapes=[slab, slab],
        compiler_params=_cparams(1),
        name="conv_ffn",
    )(xs, xs, xs, mods, mods, mods, gain, wv, wg, cwv, cwg, cbv, cbg, wd)


def _chunk_cols(w):
    return w.reshape(w.shape[0], D_FF // FC, FC).transpose(1, 0, 2)


def kernel(x, c, ctx, c_ctx, ada_w, ada_b, norm_mix, norm_ffn, na_w_qkv, na_q_gain, na_k_gain,
           na_rpb, na_w_out, lru_w_in, lru_conv_w, lru_conv_b, lru_ga_w, lru_ga_b, lru_gx_w,
           lru_gx_b, lru_lambda, lru_w_out, ffn_w_up, ffn_conv_w, ffn_conv_b, ffn_w_down):
    xs = jnp.concatenate([ctx, x], axis=1).reshape(NROWS, D)
    cc = jnp.concatenate([c, c_ctx[None], jnp.zeros((7, D), F32)], axis=0)
    mods = _mods_call(cc, ada_w, ada_b)

    head_of_lane = jnp.arange(D) // HD
    g1 = (head_of_lane[:, None] == jnp.arange(LANES)[None, :]).astype(BF16)
    g2 = g1.T

    for i in range(DEPTH):
        j = i // 2
        gain_mix = norm_mix[i][None]
        if i % 2 == 0:
            qg = (jnp.tile(na_q_gain[j], H) * (HD ** -0.5))[None]
            kg = jnp.tile(na_k_gain[j], H)[None]
            q, k, v = _qkv_call(xs, mods, i, gain_mix, na_w_qkv[j].astype(BF16), qg, kg, g1, g2)
            o = _attn_call(q, k, v, _bias_table(na_rpb[j]))
            xs = _proj_call(xs, o, mods, i, na_w_out[j].astype(BF16), True)
        else:
            gg, rec = _lru_in_call(xs, mods, i, gain_mix, lru_w_in[j].astype(BF16))
            rec3 = rec.reshape(B, LT, D)
            gg3 = gg.reshape(B, LT, D)
            cw, cb = lru_conv_w[j], lru_conv_b[j][None]
            hsb = _scan_call(rec3, cw, cb, lru_lambda[j, 1][None],
                             _gate_weights(lru_ga_w[j, 1], lru_gx_w[j, 1]),
                             lru_ga_b[j, 1][None], lru_gx_b[j, 1][None], None, None, True)
            z = _scan_call(rec3, cw, cb, lru_lambda[j, 0][None],
                           _gate_weights(lru_ga_w[j, 0], lru_gx_w[j, 0]),
                           lru_ga_b[j, 0][None], lru_gx_b[j, 0][None], hsb, gg3, False)
            xs = _proj_call(xs, z.reshape(NROWS, D), mods, i, lru_w_out[j].astype(BF16), False)
        wup = ffn_w_up[i].astype(BF16)
        cwf, cbf = ffn_conv_w[i], ffn_conv_b[i][None]
        xs = _ffn_call(
            xs, mods, i, norm_ffn[i][None],
            _chunk_cols(wup[:, :D_FF]), _chunk_cols(wup[:, D_FF:]),
            _chunk_cols(cwf[:, :D_FF]), _chunk_cols(cwf[:, D_FF:]),
            _chunk_cols(cbf[:, :D_FF]), _chunk_cols(cbf[:, D_FF:]),
            ffn_w_down[i].astype(BF16).reshape(D_FF // FC, FC, D))
    return xs.reshape(B, LT, D)[:, C:]
```

```python
import functools

import jax
import jax.numpy as jnp
from jax import lax
from jax.experimental import pallas as pl
from jax.experimental.pallas import tpu as pltpu

F32 = jnp.float32
BF16 = jnp.bfloat16

D = 1024
B = 8
S = 4096
DEPTH = 4
C = 256
LT = C + S
NROWS = B * LT
GRID_W = 64
IMG_ROWS = S // GRID_W
H = 16
HD = 64
NPAIR = H // 2
LANES = 128
WIN_ROWS = 8
WIN_COLS = 16
D_FF = 3 * D
LRU_BLOCK_W = 64
LRU_C = 8.0
EPS = 1e-6
NEG_INF = float("-inf")

TM = 544
TPB = LT // TM
FC = 512
QB = 256
NQB = LT // QB
TS = 64
NCH = LT // TS
CCH = C // TS
PITCH = TS + 4
GW = 256
VMEM_LIMIT = 56 * 1024 * 1024

T2_LEFT_DEAD = 16
T2_RIGHT_DEAD = 17
T2_DEAD = 18
T2_ENTRIES = 19


def _cparams(n_axes):
    return pltpu.CompilerParams(
        dimension_semantics=("arbitrary",) * n_axes, vmem_limit_bytes=VMEM_LIMIT)


def _const_spec(shape):
    nd = len(shape)
    return pl.BlockSpec(shape, lambda *_: (0,) * nd, pipeline_mode=pl.Buffered(1))


def _mods_kernel(cc_ref, w_ref, b_ref, o_ref):
    c = cc_ref[...]
    s = c * jax.nn.sigmoid(c)
    o_ref[...] = jnp.dot(s.astype(BF16), w_ref[...].astype(BF16),
                         preferred_element_type=F32) + b_ref[...]


def _mods_call(cc, ada_w, ada_b):
    return pl.pallas_call(
        _mods_kernel,
        grid=(DEPTH, 6),
        in_specs=[
            pl.BlockSpec((16, D), lambda l, k: (0, 0)),
            pl.BlockSpec((None, D, D), lambda l, k: (l, 0, k)),
            pl.BlockSpec((None, None, 1, D), lambda l, k: (l, k, 0, 0)),
        ],
        out_specs=pl.BlockSpec((None, None, 16, D), lambda l, k: (l, k, 0, 0)),
        out_shape=jax.ShapeDtypeStruct((DEPTH, 6, 16, D), F32),
        compiler_params=_cparams(2),
        name="adaln_mods",
    )(cc, ada_w, ada_b.reshape(DEPTH, 6, 1, D))


def _tile_pos(n, lo):
    i = pl.program_id(0)
    b = i // TPB
    j = i % TPB
    pos = j * TM + lo + lax.broadcasted_iota(jnp.int32, (n, 1), 0)
    return b, pos


def _mod_rows(m_ref, b, is_ctx):
    return jnp.where(is_ctx, m_ref[8:9, :], m_ref[pl.ds(b, 1), :])


def _modulate(x, gain, shift, scale):
    y = x * lax.rsqrt(jnp.mean(x * x, axis=-1, keepdims=True) + EPS)
    return (y * gain) * (1.0 + scale) + shift


def _mods_spec(layer, kind):
    return pl.BlockSpec((None, None, 16, D), lambda *_: (layer, kind, 0, 0))


def _row_spec(width=D):
    return pl.BlockSpec((TM, width), lambda i: (i, 0))


def _pair_spec():
    return pl.BlockSpec((NPAIR, TM, LANES), lambda i: (0, i, 0))


def _qkv_kernel(x_ref, sh_ref, sc_ref, gain_ref, w_ref, qg_ref, kg_ref, g1_ref,
                q_ref, k_ref, v_ref):
    b, pos = _tile_pos(TM, 0)
    is_ctx = pos < C
    h = _modulate(x_ref[...], gain_ref[...], _mod_rows(sh_ref, b, is_ctx),
                  _mod_rows(sc_ref, b, is_ctx)).astype(BF16)
    head_col = lax.shift_right_logical(lax.broadcasted_iota(jnp.int32, (TM, LANES), 1), 6)
    for idx, (o_ref, gn_ref) in enumerate(((q_ref, qg_ref), (k_ref, kg_ref))):
        acc = jnp.dot(h, w_ref[idx], preferred_element_type=F32)
        ss = jnp.dot((acc * acc).astype(BF16), g1_ref[...], preferred_element_type=F32)
        rstd = lax.rsqrt(ss * (1.0 / HD) + EPS)
        for p in range(NPAIR):
            ls = slice(p * LANES, (p + 1) * LANES)
            rb = jnp.take_along_axis(rstd, head_col + 2 * p, axis=1, mode="promise_in_bounds")
            o_ref[p] = (acc[:, ls] * rb * gn_ref[:, ls]).astype(BF16)
    v = jnp.dot(h, w_ref[2], preferred_element_type=F32).astype(BF16)
    first_head = lax.broadcasted_iota(jnp.int32, (1, LANES), 1) < HD
    one = jnp.ones((), BF16)
    for p in range(NPAIR):
        vp = v[:, p * LANES:(p + 1) * LANES]
        v_ref[0, p] = jnp.where(first_head, vp, one)
        v_ref[1, p] = jnp.where(first_head, one, vp)


def _qkv_call(xs, mods, layer, gain, w, qg, kg, g1):
    out = jax.ShapeDtypeStruct((NPAIR, NROWS, LANES), BF16)
    return pl.pallas_call(
        _qkv_kernel,
        grid=(NROWS // TM,),
        in_specs=[
            _row_spec(), _mods_spec(layer, 0), _mods_spec(layer, 1), _const_spec((1, D)),
            _const_spec((3, D, D)), _const_spec((1, D)), _const_spec((1, D)),
            _const_spec((D, LANES)),
        ],
        out_specs=[_pair_spec(), _pair_spec(),
                   pl.BlockSpec((2, NPAIR, TM, LANES), lambda i: (0, 0, i, 0))],
        out_shape=[out, out, jax.ShapeDtypeStruct((2, NPAIR, NROWS, LANES), BF16)],
        compiler_params=_cparams(1),
        name="na_qkv",
    )(xs, mods, mods, gain, w, qg, kg, g1)


def _dot_t(a, b):
    return lax.dot_general(a, b, (((1,), (1,)), ((), ())), preferred_element_type=F32)


def _attn_kernel(q_ref, kc_ref, vc_ref, k0_ref, k1_ref, k2_ref, v0_ref, v1_ref, v2_ref,
                 t2_ref, o_ref):
    jj = pl.program_id(1)
    rg = jj - 1
    win0 = jnp.clip(rg, 1, NQB - 3) - 1
    off = 4 * win0 - 4 * rg + (WIN_ROWS - 1)
    tile_idx = []
    for i in range(4):
        rs = jnp.clip(4 * rg + i - WIN_ROWS // 2, 0, IMG_ROWS - WIN_ROWS)
        row = []
        for pu in range(6):
            kl = 4 * win0 + 2 * pu
            lv = (kl >= rs) & (kl < rs + WIN_ROWS) & (jj > 0)
            rv = (kl + 1 >= rs) & (kl + 1 < rs + WIN_ROWS) & (jj > 0)
            d = jnp.clip(2 * pu - i + off, -1, 14) + 1
            row.append(jnp.where(lv & rv, d, jnp.where(
                rv, T2_LEFT_DEAD, jnp.where(lv, T2_RIGHT_DEAD, T2_DEAD))))
        tile_idx.append(row)
    lane = lax.broadcasted_iota(jnp.int32, (1, LANES), 1)
    k_refs = (k0_ref, k1_ref, k2_ref)
    v_refs = (v0_ref, v1_ref, v2_ref)

    for p in range(NPAIR):
        qp = q_ref[p]
        kcp = kc_ref[p]
        outs = []
        for hh in range(2):
            head = 2 * p + hh
            sel = (lane < HD) if hh == 0 else (lane >= HD)
            qm = jnp.where(sel, qp, jnp.zeros_like(qp))
            s_all = [_dot_t(qm, kcp)]
            for j in range(3):
                bias = jnp.concatenate(
                    [jnp.concatenate([t2_ref[head, tile_idx[i][2 * j + cp]] for cp in range(2)],
                                     axis=1) for i in range(4)], axis=0)
                s_all.append(_dot_t(qm, k_refs[j][p]) + bias)
            mm = jnp.maximum(jnp.maximum(s_all[0], s_all[1]), jnp.maximum(s_all[2], s_all[3]))
            m = jnp.max(jnp.maximum(mm[:, :LANES], mm[:, LANES:]), axis=-1, keepdims=True)
            acc = jnp.dot(jnp.exp(s_all[0] - m).astype(BF16), vc_ref[hh, p],
                          preferred_element_type=F32)
            for j in range(3):
                acc = acc + jnp.dot(jnp.exp(s_all[j + 1] - m).astype(BF16), v_refs[j][hh, p],
                                    preferred_element_type=F32)
            outs.append(acc / pltpu.roll(acc, HD, axis=1))
        o_ref[p] = jnp.where(lane < HD, outs[0], outs[1]).astype(BF16)


def _attn_call(q, k, v, t2):
    def blk(fn):
        return pl.BlockSpec((NPAIR, QB, LANES), lambda b, jj: (0, fn(b, jj), 0))

    def vblk(fn):
        return pl.BlockSpec((2, NPAIR, QB, LANES), lambda b, jj: (0, 0, fn(b, jj), 0))

    def win(o):
        return lambda b, jj: b * NQB + jnp.clip(jj - 1, 1, NQB - 3) + o

    return pl.pallas_call(
        _attn_kernel,
        grid=(B, NQB),
        in_specs=[
            blk(lambda b, jj: b * NQB + jj),
            blk(lambda b, jj: b * NQB), vblk(lambda b, jj: b * NQB),
            blk(win(0)), blk(win(1)), blk(win(2)),
            vblk(win(0)), vblk(win(1)), vblk(win(2)),
            _const_spec((H, T2_ENTRIES, GRID_W, LANES)),
        ],
        out_specs=blk(lambda b, jj: b * NQB + jj),
        out_shape=jax.ShapeDtypeStruct((NPAIR, NROWS, LANES), BF16),
        compiler_params=_cparams(2),
        name="na_attention",
    )(q, k, v, k, k, k, v, v, v, t2)


def _proj_kernel(x_ref, y_ref, gate_ref, w_ref, o_ref):
    b, pos = _tile_pos(TM, 0)
    y = jnp.concatenate([y_ref[p] for p in range(NPAIR)], axis=1)
    o_ref[...] = x_ref[...] + _mod_rows(gate_ref, b, pos < C) * jnp.dot(
        y, w_ref[...], preferred_element_type=F32)


def _proj_call(xs, y, mods, layer, w):
    return pl.pallas_call(
        _proj_kernel,
        grid=(NROWS // TM,),
        in_specs=[_row_spec(), _pair_spec(), _mods_spec(layer, 2), _const_spec((D, D))],
        out_specs=_row_spec(),
        out_shape=jax.ShapeDtypeStruct((NROWS, D), F32),
        compiler_params=_cparams(1),
        name="na_out_proj",
    )(xs, y, mods, w)


def _bias_table(rpb):
    qc = jnp.arange(GRID_W)[:, None]
    kc = jnp.arange(GRID_W)[None, :]
    ws = jnp.clip(qc - WIN_COLS // 2, 0, GRID_W - WIN_COLS)
    col_ok = (kc >= ws) & (kc < ws + WIN_COLS)
    onehot = ((kc - qc + (WIN_COLS - 1))[None] == jnp.arange(2 * WIN_COLS - 1)[:, None, None])
    t = jnp.einsum("hdc,cqk->hdqk", rpb, onehot.astype(F32), precision=lax.Precision.HIGHEST)
    t = jnp.where(col_ok[None, None], t, NEG_INF)
    dead = jnp.full((H, 1, GRID_W, GRID_W), NEG_INF, F32)
    t_ext = jnp.concatenate([dead, t, dead], axis=1)
    pairs = jnp.concatenate([t_ext[:, :-1], t_ext[:, 1:]], axis=-1)
    lo, hi = WIN_ROWS // 2 - 1, WIN_ROWS // 2 - 1 + WIN_ROWS - 1
    left_dead = jnp.concatenate([dead, t[:, lo:lo + 1]], axis=-1)
    right_dead = jnp.concatenate([t[:, hi:hi + 1], dead], axis=-1)
    all_dead = jnp.concatenate([dead, dead], axis=-1)
    return jnp.concatenate([pairs, left_dead, right_dead, all_dead], axis=1)


assert 8 < C % TM < TM - 8
BND = C % TM
BND_LO = BND - 8


def _halo_specs():
    t8 = TM // 8
    return [
        pl.BlockSpec((8, D), lambda i: (jnp.maximum(i * t8 - 1, 0), 0)),
        _row_spec(),
        pl.BlockSpec((8, D), lambda i: (jnp.minimum((i + 1) * t8, NROWS // 8 - 1), 0)),
    ]


def _dead_halo_rows(reach_back):
    start = (pl.program_id(0) % TPB) * TM
    e = lax.broadcasted_iota(jnp.int32, (TM + 16, 1), 0)
    at_start = (start == 0) | (start == C)
    at_end = (start + TM == C) | (start + TM == LT)
    return ((e >= 8 - reach_back) & (e < 8) & at_start) | ((e == TM + 8) & at_end)


def _slab_conv(u_scr, s, taps, cb, pos):
    def seg(lo, n, masked):
        y = cb
        for off, w in taps:
            u = u_scr[s, pl.ds(lo + 8 + off, n), :]
            if masked and off != 0:
                p = pos[lo:lo + n]
                crosses = (p >= C) & (p < C - off) if off < 0 else (p < C) & (p >= C - off)
                u = jnp.where(crosses, 0.0, u)
            y = y + u * w
        return y
    return jnp.concatenate([seg(0, BND_LO, False), seg(BND_LO, 16, True),
                            seg(BND_LO + 16, TM - BND_LO - 16, False)], axis=0)


def _lru_in_kernel(xp_ref, x_ref, xn_ref, sh_ref, sc_ref, gain_ref, w_ref, cw_ref, cb_ref,
                   gg_ref, xr_ref, rec_scr):
    b, pos = _tile_pos(TM, 0)
    _, pos_e = _tile_pos(TM + 16, -8)
    is_ctx_e = pos_e < C
    x_e = jnp.concatenate([xp_ref[...], x_ref[...], xn_ref[...]], axis=0)
    h = _modulate(x_e, gain_ref[...], _mod_rows(sh_ref, b, is_ctx_e),
                  _mod_rows(sc_ref, b, is_ctx_e))
    h = jnp.where(_dead_halo_rows(2), 0.0, h).astype(BF16)
    gate = jnp.dot(h, w_ref[0], preferred_element_type=F32)[8:8 + TM]
    gg_ref[...] = jax.nn.gelu(gate, approximate=True).astype(BF16)
    rec = jnp.dot(h, w_ref[1], preferred_element_type=F32)
    cw = cw_ref[...]
    cb = cb_ref[...]
    for s in range(D // LANES):
        ls = slice(s * LANES, (s + 1) * LANES)
        rec_scr[s] = rec[:, ls]
        taps = [(k - 2, cw[k:k + 1, ls]) for k in range(4)]
        xr_ref[:, ls] = _slab_conv(rec_scr, s, taps, cb[:, ls], pos)


def _lru_in_call(xs, mods, layer, gain, w, cw, cb):
    return pl.pallas_call(
        _lru_in_kernel,
        grid=(NROWS // TM,),
        in_specs=_halo_specs() + [
            _mods_spec(layer, 0), _mods_spec(layer, 1), _const_spec((1, D)),
            _const_spec((2, D, D)), _const_spec((4, D)), _const_spec((1, D)),
        ],
        out_specs=[_row_spec(), _row_spec()],
        out_shape=[jax.ShapeDtypeStruct((NROWS, D), BF16),
                   jax.ShapeDtypeStruct((NROWS, D), F32)],
        scratch_shapes=[pltpu.VMEM((D // LANES, TM + 16, LANES), F32)],
        compiler_params=_cparams(1),
        name="lru_in_proj",
    )(xs, xs, xs, mods, mods, gain, w, cw, cb)


def _chunk_index(g, reverse):
    if not reverse:
        return g
    return jnp.where(g < CCH, CCH - 1 - g, NCH + CCH - 1 - g)


def _scan_kernel(xr_ref, lam_ref, wg_ref, ba_ref, bx_ref, *rest, reverse):
    if reverse:
        o_ref, a_scr, b_scr, h_scr = rest
    else:
        hsb_ref, gg_ref, x_ref, wo_ref, g1_ref, o_ref, a_scr, b_scr, h_scr = rest
    g = pl.program_id(0)

    @pl.when(g == 0)
    def _():
        h_scr[...] = jnp.zeros_like(h_scr)

    lam = lam_ref[...]
    half_log_a = (-0.5 * LRU_C) * (jnp.maximum(-lam, 0.0) + jnp.log1p(jnp.exp(-jnp.abs(lam))))
    for gq in range(D // GW):
        cols = slice(gq * GW, (gq + 1) * GW)
        xg = jnp.concatenate([xr_ref[b, :, cols] for b in range(B)], axis=0)
        pre = jnp.dot(xg.astype(BF16), wg_ref[gq], preferred_element_type=F32)
        t_r = jnp.tanh(pre[:, :GW] + ba_ref[:, cols])
        t_i = jnp.tanh(pre[:, GW:] + bx_ref[:, cols])
        log_a = half_log_a[:, cols] * t_r + half_log_a[:, cols]
        a = jnp.exp(log_a)
        y = (-1.0 - a * a) * jnp.tanh(log_a)
        root = jnp.where(y > 0.0, y * lax.rsqrt(y), 0.0)
        hx = 0.5 * xg
        bb = root * (hx * t_i + hx)
        for k2 in range(GW // LANES):
            k = gq * (GW // LANES) + k2
            for b in range(B):
                rows = slice(k * B * PITCH + b * PITCH, k * B * PITCH + b * PITCH + TS)
                a_scr[rows, :] = a[b * TS:(b + 1) * TS, k2 * LANES:(k2 + 1) * LANES]
                b_scr[rows, :] = bb[b * TS:(b + 1) * TS, k2 * LANES:(k2 + 1) * LANES]

    nslab = D // LANES

    def step(_, carry):
        t, hs = carry
        out = []
        for k in range(nslab):
            idx = pl.ds(k * B * PITCH + t, B, stride=PITCH)
            hk = a_scr[idx, :] * hs[k] + b_scr[idx, :]
            b_scr[idx, :] = hk
            out.append(hk)
        return (t - 1 if reverse else t + 1), tuple(out)

    t0 = jnp.int32(TS - 1 if reverse else 0)
    _, h_fin = lax.fori_loop(0, TS, step, (t0, tuple(h_scr[k] for k in range(nslab))))
    for k in range(nslab):
        h_scr[k] = h_fin[k]

    def slab_rows(b, k):
        return slice(k * B * PITCH + b * PITCH, k * B * PITCH + b * PITCH + TS)

    if reverse:
        for b in range(B):
            for k in range(nslab):
                o_ref[b, :, k * LANES:(k + 1) * LANES] = b_scr[slab_rows(b, k), :]
    else:
        z = jnp.concatenate([
            jnp.concatenate([
                ((b_scr[slab_rows(b, k), :] + hsb_ref[b, :, k * LANES:(k + 1) * LANES])
                 * gg_ref[b, :, k * LANES:(k + 1) * LANES].astype(F32)).astype(BF16)
                for k in range(nslab)], axis=1)
            for b in range(B)], axis=0)
        proj = jnp.dot(z, wo_ref[...], preferred_element_type=F32)
        is_ctx = g < CCH
        for b in range(B):
            g1 = jnp.where(is_ctx, g1_ref[8:9, :], g1_ref[b:b + 1, :])
            o_ref[b] = x_ref[b] + g1 * proj[b * TS:(b + 1) * TS]


def _scan_call(xr3, lam, wg, ba, bx, reverse, fwd_args=None):
    main = pl.BlockSpec((B, TS, D), lambda g: (0, _chunk_index(g, reverse), 0))
    in_specs = [
        main, _const_spec((1, D)),
        _const_spec((D // GW, GW, 2 * GW)), _const_spec((1, D)), _const_spec((1, D)),
    ]
    args = [xr3, lam, wg, 0.5 * ba, 0.5 * bx]
    if not reverse:
        hsb, gg, xs3, wo, mods, layer = fwd_args
        in_specs += [main, main, main, _const_spec((D, D)), _mods_spec(layer, 2)]
        args += [hsb, gg, xs3, wo, mods]
    slab = pltpu.VMEM((D // LANES * B * PITCH, LANES), F32)
    return pl.pallas_call(
        functools.partial(_scan_kernel, reverse=reverse),
        grid=(NCH,),
        in_specs=in_specs,
        out_specs=main,
        out_shape=jax.ShapeDtypeStruct((B, LT, D), F32),
        scratch_shapes=[slab, slab, pltpu.VMEM((D // LANES, B, LANES), F32)],
        compiler_params=_cparams(1),
        name="lru_scan_bwd" if reverse else "lru_scan_fwd",
    )(*args)


def _gate_weights(wa, wx):
    def bd(w):
        w4 = w.reshape(D // GW, GW // LRU_BLOCK_W, LRU_BLOCK_W, LRU_BLOCK_W)
        eye = jnp.eye(GW // LRU_BLOCK_W, dtype=w.dtype)
        return (w4[:, :, :, None, :] * eye[None, :, None, :, None]).reshape(D // GW, GW, GW)
    return (0.5 * jnp.concatenate([bd(wa), bd(wx)], axis=-1)).astype(BF16)


FSLAB = FC // LANES


def _ffn_kernel(xp_ref, x_ref, xn_ref, sh_ref, sc_ref, g2_ref, gain_ref, wu_ref, cw_ref, cb_ref,
                wd_ref, o_ref, uv_scr, ug_scr):
    b, pos = _tile_pos(TM, 0)
    _, pos_e = _tile_pos(TM + 16, -8)
    is_ctx_e = pos_e < C
    x = x_ref[...]
    x_e = jnp.concatenate([xp_ref[...], x, xn_ref[...]], axis=0)
    h = _modulate(x_e, gain_ref[...], _mod_rows(sh_ref, b, is_ctx_e),
                  _mod_rows(sc_ref, b, is_ctx_e))
    h = jnp.where(_dead_halo_rows(1), 0.0, h).astype(BF16)

    cw = cw_ref[...]
    cb = cb_ref[...]
    acc = jnp.zeros((TM, D), F32)
    for ci in range(D_FF // FC):
        uv = jnp.dot(h, wu_ref[ci], preferred_element_type=F32)
        ug = jnp.dot(h, wu_ref[D_FF // FC + ci], preferred_element_type=F32)
        base = (ci % 2) * FSLAB
        for s in range(FSLAB):
            uv_scr[base + s] = uv[:, s * LANES:(s + 1) * LANES]
            ug_scr[base + s] = ug[:, s * LANES:(s + 1) * LANES]
        acts = []
        for s in range(FSLAB):
            lv = slice(ci * FC + s * LANES, ci * FC + (s + 1) * LANES)
            lg = slice(D_FF + lv.start, D_FF + lv.stop)
            val = _slab_conv(uv_scr, base + s, [(k - 1, cw[k:k + 1, lv]) for k in range(3)],
                             cb[:, lv], pos)
            hg = _slab_conv(ug_scr, base + s,
                            [(k - 1, 0.5 * cw[k:k + 1, lg]) for k in range(3)],
                            0.5 * cb[:, lg], pos)
            acts.append((val * (hg + hg * jnp.tanh(hg))).astype(BF16))
        acc = acc + jnp.dot(jnp.concatenate(acts, axis=1), wd_ref[ci],
                            preferred_element_type=F32)
    o_ref[...] = x + _mod_rows(g2_ref, b, pos < C) * acc


def _ffn_call(xs, mods, layer, gain, wu, cw, cb, wd):
    slab = pltpu.VMEM((2 * FSLAB, TM + 16, LANES), F32)
    return pl.pallas_call(
        _ffn_kernel,
        grid=(NROWS // TM,),
        in_specs=_halo_specs() + [
            _mods_spec(layer, 3), _mods_spec(layer, 4), _mods_spec(layer, 5),
            _const_spec((1, D)),
            _const_spec((2 * D_FF // FC, D, FC)), _const_spec((3, 2 * D_FF)),
            _const_spec((1, 2 * D_FF)), _const_spec((D_FF // FC, FC, D)),
        ],
        out_specs=_row_spec(),
        out_shape=jax.ShapeDtypeStruct((NROWS, D), F32),
        scratch_shapes=[slab, slab],
        compiler_params=_cparams(1),
        name="conv_ffn",
    )(xs, xs, xs, mods, mods, mods, gain, wu, cw, cb, wd)


def _split_cols(w, n):
    return w.astype(BF16).reshape(D, n, D).transpose(1, 0, 2)


def kernel(x, c, ctx, c_ctx, ada_w, ada_b, norm_mix, norm_ffn, na_w_qkv, na_q_gain, na_k_gain,
           na_rpb, na_w_out, lru_w_in, lru_conv_w, lru_conv_b, lru_ga_w, lru_ga_b, lru_gx_w,
           lru_gx_b, lru_lambda, lru_w_out, ffn_w_up, ffn_conv_w, ffn_conv_b, ffn_w_down):
    xs = jnp.concatenate([ctx, x], axis=1).reshape(NROWS, D)
    cc = jnp.concatenate([c, c_ctx[None], jnp.zeros((7, D), F32)], axis=0)
    mods = _mods_call(cc, ada_w, ada_b)

    head_of_lane = jnp.arange(D) // HD
    g1 = (head_of_lane[:, None] == jnp.arange(LANES)[None, :]).astype(BF16)

    for i in range(DEPTH):
        j = i // 2
        gain_mix = norm_mix[i][None]
        if i % 2 == 0:
            qg = (jnp.tile(na_q_gain[j], H) * (HD ** -0.5))[None]
            kg = jnp.tile(na_k_gain[j], H)[None]
            q, k, v = _qkv_call(xs, mods, i, gain_mix, _split_cols(na_w_qkv[j], 3), qg, kg, g1)
            o = _attn_call(q, k, v, _bias_table(na_rpb[j]))
            xs = _proj_call(xs, o, mods, i, na_w_out[j].astype(BF16))
        else:
            gg, xr = _lru_in_call(xs, mods, i, gain_mix, _split_cols(lru_w_in[j], 2),
                                  lru_conv_w[j], lru_conv_b[j][None])
            xr3 = xr.reshape(B, LT, D)
            hsb = _scan_call(xr3, lru_lambda[j, 1][None],
                             _gate_weights(lru_ga_w[j, 1], lru_gx_w[j, 1]),
                             lru_ga_b[j, 1][None], lru_gx_b[j, 1][None], True)
            xs = _scan_call(xr3, lru_lambda[j, 0][None],
                            _gate_weights(lru_ga_w[j, 0], lru_gx_w[j, 0]),
                            lru_ga_b[j, 0][None], lru_gx_b[j, 0][None], False,
                            (hsb, gg.reshape(B, LT, D), xs.reshape(B, LT, D),
                             lru_w_out[j].astype(BF16), mods, i)).reshape(NROWS, D)
        wu = ffn_w_up[i].astype(BF16).reshape(D, 2 * D_FF // FC, FC).transpose(1, 0, 2)
        xs = _ffn_call(xs, mods, i, norm_ffn[i][None], wu, ffn_conv_w[i], ffn_conv_b[i][None],
                       ffn_w_down[i].astype(BF16).reshape(D_FF // FC, FC, D))
    return xs.reshape(B, LT, D)[:, C:]
```

```python
import functools

import jax
import jax.numpy as jnp
from jax import lax
from jax.experimental import pallas as pl
from jax.experimental.pallas import tpu as pltpu

F32 = jnp.float32
BF16 = jnp.bfloat16

D = 1024
B = 8
S = 4096
DEPTH = 4
C = 256
LT = C + S
NROWS = B * LT
GRID_W = 64
IMG_ROWS = S // GRID_W
H = 16
HD = 64
NPAIR = H // 2
LANES = 128
WIN_ROWS = 8
WIN_COLS = 16
D_FF = 3 * D
LRU_BLOCK_W = 64
LRU_C = 8.0
EPS = 1e-6
NEG_INF = float("-inf")

TM = 544
TPB = LT // TM
FC = 512
QB = 256
NQB = LT // QB
TS = 64
NCH = LT // TS
CCH = C // TS
PITCH = TS + 4
GW = 256
VMEM_LIMIT = 56 * 1024 * 1024

T2_LEFT_DEAD = 16
T2_RIGHT_DEAD = 17
T2_DEAD = 18
T2_ENTRIES = 19


def _cparams(n_axes):
    return pltpu.CompilerParams(
        dimension_semantics=("arbitrary",) * n_axes, vmem_limit_bytes=VMEM_LIMIT)


def _const_spec(shape):
    nd = len(shape)
    return pl.BlockSpec(shape, lambda *_: (0,) * nd, pipeline_mode=pl.Buffered(1))


def _mods_kernel(cc_ref, w_ref, b_ref, o_ref):
    c = cc_ref[...]
    s = c * jax.nn.sigmoid(c)
    o_ref[...] = jnp.dot(s.astype(BF16), w_ref[...].astype(BF16),
                         preferred_element_type=F32) + b_ref[...]


def _mods_call(cc, ada_w, ada_b):
    return pl.pallas_call(
        _mods_kernel,
        grid=(DEPTH, 6),
        in_specs=[
            pl.BlockSpec((16, D), lambda l, k: (0, 0)),
            pl.BlockSpec((None, D, D), lambda l, k: (l, 0, k)),
            pl.BlockSpec((None, None, 1, D), lambda l, k: (l, k, 0, 0)),
        ],
        out_specs=pl.BlockSpec((None, None, 16, D), lambda l, k: (l, k, 0, 0)),
        out_shape=jax.ShapeDtypeStruct((DEPTH, 6, 16, D), F32),
        compiler_params=_cparams(2),
        name="adaln_mods",
    )(cc, ada_w, ada_b.reshape(DEPTH, 6, 1, D))


def _tile_pos(n, lo):
    i = pl.program_id(0)
    b = i // TPB
    j = i % TPB
    pos = j * TM + lo + lax.broadcasted_iota(jnp.int32, (n, 1), 0)
    return b, pos


def _mod_rows(m_ref, b, is_ctx):
    return jnp.where(is_ctx, m_ref[8:9, :], m_ref[pl.ds(b, 1), :])


def _modulate(x, gain, shift, scale):
    y = x * lax.rsqrt(jnp.mean(x * x, axis=-1, keepdims=True) + EPS)
    return (y * gain) * (1.0 + scale) + shift


def _mods_spec(layer, kind):
    return pl.BlockSpec((None, None, 16, D), lambda *_: (layer, kind, 0, 0))


def _row_spec(width=D):
    return pl.BlockSpec((TM, width), lambda i: (i, 0))


def _pair_spec():
    return pl.BlockSpec((NPAIR, TM, LANES), lambda i: (0, i, 0))


def _qkv_kernel(x_ref, sh_ref, sc_ref, gain_ref, w_ref, qg_ref, kg_ref, g1_ref,
                q_ref, k_ref, v_ref):
    b, pos = _tile_pos(TM, 0)
    is_ctx = pos < C
    h = _modulate(x_ref[...], gain_ref[...], _mod_rows(sh_ref, b, is_ctx),
                  _mod_rows(sc_ref, b, is_ctx)).astype(BF16)
    head_col = lax.shift_right_logical(lax.broadcasted_iota(jnp.int32, (TM, LANES), 1), 6)
    for idx, (o_ref, gn_ref) in enumerate(((q_ref, qg_ref), (k_ref, kg_ref))):
        acc = jnp.dot(h, w_ref[idx], preferred_element_type=F32)
        ss = jnp.dot((acc * acc).astype(BF16), g1_ref[...], preferred_element_type=F32)
        rstd = lax.rsqrt(ss * (1.0 / HD) + EPS)
        for p in range(NPAIR):
            ls = slice(p * LANES, (p + 1) * LANES)
            rb = jnp.take_along_axis(rstd, head_col + 2 * p, axis=1, mode="promise_in_bounds")
            o_ref[p] = (acc[:, ls] * rb * gn_ref[:, ls]).astype(BF16)
    v = jnp.dot(h, w_ref[2], preferred_element_type=F32).astype(BF16)
    first_head = lax.broadcasted_iota(jnp.int32, (1, LANES), 1) < HD
    one = jnp.ones((), BF16)
    for p in range(NPAIR):
        vp = v[:, p * LANES:(p + 1) * LANES]
        v_ref[0, p] = jnp.where(first_head, vp, one)
        v_ref[1, p] = jnp.where(first_head, one, vp)


def _qkv_call(xs, mods, layer, gain, w, qg, kg, g1):
    out = jax.ShapeDtypeStruct((NPAIR, NROWS, LANES), BF16)
    return pl.pallas_call(
        _qkv_kernel,
        grid=(NROWS // TM,),
        in_specs=[
            _row_spec(), _mods_spec(layer, 0), _mods_spec(layer, 1), _const_spec((1, D)),
            _const_spec((3, D, D)), _const_spec((1, D)), _const_spec((1, D)),
            _const_spec((D, LANES)),
        ],
        out_specs=[_pair_spec(), _pair_spec(),
                   pl.BlockSpec((2, NPAIR, TM, LANES), lambda i: (0, 0, i, 0))],
        out_shape=[out, out, jax.ShapeDtypeStruct((2, NPAIR, NROWS, LANES), BF16)],
        compiler_params=_cparams(1),
        name="na_qkv",
    )(xs, mods, mods, gain, w, qg, kg, g1)


def _dot_t(a, b):
    return lax.dot_general(a, b, (((1,), (1,)), ((), ())), preferred_element_type=F32)


def _attn_kernel(q_ref, kc_ref, vc_ref, k0_ref, k1_ref, k2_ref, v0_ref, v1_ref, v2_ref,
                 t2_ref, o_ref):
    jj = pl.program_id(1)
    rg = jj - 1
    win0 = jnp.clip(rg, 1, NQB - 3) - 1
    off = 4 * win0 - 4 * rg + (WIN_ROWS - 1)
    tile_idx = []
    for i in range(4):
        rs = jnp.clip(4 * rg + i - WIN_ROWS // 2, 0, IMG_ROWS - WIN_ROWS)
        row = []
        for pu in range(6):
            kl = 4 * win0 + 2 * pu
            lv = (kl >= rs) & (kl < rs + WIN_ROWS) & (jj > 0)
            rv = (kl + 1 >= rs) & (kl + 1 < rs + WIN_ROWS) & (jj > 0)
            d = jnp.clip(2 * pu - i + off, -1, 14) + 1
            row.append(jnp.where(lv & rv, d, jnp.where(
                rv, T2_LEFT_DEAD, jnp.where(lv, T2_RIGHT_DEAD, T2_DEAD))))
        tile_idx.append(row)
    lane = lax.broadcasted_iota(jnp.int32, (1, LANES), 1)
    k_refs = (k0_ref, k1_ref, k2_ref)
    v_refs = (v0_ref, v1_ref, v2_ref)

    def scores(head):
        p, hh = divmod(head, 2)
        qp = q_ref[p]
        sel = (lane < HD) if hh == 0 else (lane >= HD)
        qm = jnp.where(sel, qp, jnp.zeros_like(qp))
        s_all = [_dot_t(qm, kc_ref[p])]
        for j in range(3):
            bias = jnp.concatenate(
                [jnp.concatenate([t2_ref[head, tile_idx[i][2 * j + cp]] for cp in range(2)],
                                 axis=1) for i in range(4)], axis=0)
            s_all.append(_dot_t(qm, k_refs[j][p]) + bias)
        mm = jnp.maximum(jnp.maximum(s_all[0], s_all[1]), jnp.maximum(s_all[2], s_all[3]))
        m = jnp.max(jnp.maximum(mm[:, :LANES], mm[:, LANES:]), axis=-1, keepdims=True)
        return s_all, m

    def weighted_values(head, s_all, m):
        p, hh = divmod(head, 2)
        acc = jnp.dot(jnp.exp(s_all[0] - m).astype(BF16), vc_ref[hh, p],
                      preferred_element_type=F32)
        for j in range(3):
            acc = acc + jnp.dot(jnp.exp(s_all[j + 1] - m).astype(BF16), v_refs[j][hh, p],
                                preferred_element_type=F32)
        return acc / pltpu.roll(acc, HD, axis=1)

    skew = 1
    pending = [scores(h) for h in range(skew)]
    outs = []
    for head in range(H):
        if head + skew < H:
            pending.append(scores(head + skew))
        outs.append(weighted_values(head, *pending.pop(0)))
        if head % 2 == 1:
            o_ref[head // 2] = jnp.where(lane < HD, outs[0], outs[1]).astype(BF16)
            outs = []


def _attn_call(q, k, v, t2):
    def blk(fn):
        return pl.BlockSpec((NPAIR, QB, LANES), lambda b, jj: (0, fn(b, jj), 0))

    def vblk(fn):
        return pl.BlockSpec((2, NPAIR, QB, LANES), lambda b, jj: (0, 0, fn(b, jj), 0))

    def win(o):
        return lambda b, jj: b * NQB + jnp.clip(jj - 1, 1, NQB - 3) + o

    return pl.pallas_call(
        _attn_kernel,
        grid=(B, NQB),
        in_specs=[
            blk(lambda b, jj: b * NQB + jj),
            blk(lambda b, jj: b * NQB), vblk(lambda b, jj: b * NQB),
            blk(win(0)), blk(win(1)), blk(win(2)),
            vblk(win(0)), vblk(win(1)), vblk(win(2)),
            _const_spec((H, T2_ENTRIES, GRID_W, LANES)),
        ],
        out_specs=blk(lambda b, jj: b * NQB + jj),
        out_shape=jax.ShapeDtypeStruct((NPAIR, NROWS, LANES), BF16),
        compiler_params=_cparams(2),
        name="na_attention",
    )(q, k, v, k, k, k, v, v, v, t2)


def _proj_kernel(x_ref, y_ref, gate_ref, w_ref, o_ref):
    b, pos = _tile_pos(TM, 0)
    y = jnp.concatenate([y_ref[p] for p in range(NPAIR)], axis=1)
    o_ref[...] = x_ref[...] + _mod_rows(gate_ref, b, pos < C) * jnp.dot(
        y, w_ref[...], preferred_element_type=F32)


def _proj_call(xs, y, mods, layer, w):
    return pl.pallas_call(
        _proj_kernel,
        grid=(NROWS // TM,),
        in_specs=[_row_spec(), _pair_spec(), _mods_spec(layer, 2), _const_spec((D, D))],
        out_specs=_row_spec(),
        out_shape=jax.ShapeDtypeStruct((NROWS, D), F32),
        compiler_params=_cparams(1),
        name="na_out_proj",
    )(xs, y, mods, w)


def _bias_table(rpb):
    qc = jnp.arange(GRID_W)[:, None]
    kc = jnp.arange(GRID_W)[None, :]
    ws = jnp.clip(qc - WIN_COLS // 2, 0, GRID_W - WIN_COLS)
    col_ok = (kc >= ws) & (kc < ws + WIN_COLS)
    onehot = ((kc - qc + (WIN_COLS - 1))[None] == jnp.arange(2 * WIN_COLS - 1)[:, None, None])
    t = jnp.einsum("hdc,cqk->hdqk", rpb, onehot.astype(F32), precision=lax.Precision.HIGHEST)
    t = jnp.where(col_ok[None, None], t, NEG_INF)
    dead = jnp.full((H, 1, GRID_W, GRID_W), NEG_INF, F32)
    t_ext = jnp.concatenate([dead, t, dead], axis=1)
    pairs = jnp.concatenate([t_ext[:, :-1], t_ext[:, 1:]], axis=-1)
    lo, hi = WIN_ROWS // 2 - 1, WIN_ROWS // 2 - 1 + WIN_ROWS - 1
    left_dead = jnp.concatenate([dead, t[:, lo:lo + 1]], axis=-1)
    right_dead = jnp.concatenate([t[:, hi:hi + 1], dead], axis=-1)
    all_dead = jnp.concatenate([dead, dead], axis=-1)
    return jnp.concatenate([pairs, left_dead, right_dead, all_dead], axis=1)


assert 8 < C % TM < TM - 8
BND = C % TM
BND_LO = BND - 8


def _halo_specs():
    t8 = TM // 8
    return [
        pl.BlockSpec((8, D), lambda i: (jnp.maximum(i * t8 - 1, 0), 0)),
        _row_spec(),
        pl.BlockSpec((8, D), lambda i: (jnp.minimum((i + 1) * t8, NROWS // 8 - 1), 0)),
    ]


def _dead_halo_rows(reach_back):
    start = (pl.program_id(0) % TPB) * TM
    e = lax.broadcasted_iota(jnp.int32, (TM + 16, 1), 0)
    at_start = (start == 0) | (start == C)
    at_end = (start + TM == C) | (start + TM == LT)
    return ((e >= 8 - reach_back) & (e < 8) & at_start) | ((e == TM + 8) & at_end)


def _slab_conv(u_scr, s, taps, cb, pos):
    def seg(lo, n, masked):
        y = cb
        for off, w in taps:
            u = u_scr[s, pl.ds(lo + 8 + off, n), :]
            if masked and off != 0:
                p = pos[lo:lo + n]
                crosses = (p >= C) & (p < C - off) if off < 0 else (p < C) & (p >= C - off)
                u = jnp.where(crosses, 0.0, u)
            y = y + u * w
        return y
    return jnp.concatenate([seg(0, BND_LO, False), seg(BND_LO, 16, True),
                            seg(BND_LO + 16, TM - BND_LO - 16, False)], axis=0)


def _lru_in_kernel(xp_ref, x_ref, xn_ref, sh_ref, sc_ref, gain_ref, w_ref, cw_ref, cb_ref,
                   gg_ref, xr_ref, rec_scr):
    b, pos = _tile_pos(TM, 0)
    _, pos_e = _tile_pos(TM + 16, -8)
    is_ctx_e = pos_e < C
    x_e = jnp.concatenate([xp_ref[...], x_ref[...], xn_ref[...]], axis=0)
    h = _modulate(x_e, gain_ref[...], _mod_rows(sh_ref, b, is_ctx_e),
                  _mod_rows(sc_ref, b, is_ctx_e))
    h = jnp.where(_dead_halo_rows(2), 0.0, h).astype(BF16)
    gate = jnp.dot(h, w_ref[0], preferred_element_type=F32)[8:8 + TM]
    gg_ref[...] = jax.nn.gelu(gate, approximate=True).astype(BF16)
    rec = jnp.dot(h, w_ref[1], preferred_element_type=F32)
    cw = cw_ref[...]
    cb = cb_ref[...]
    for s in range(D // LANES):
        ls = slice(s * LANES, (s + 1) * LANES)
        rec_scr[s] = rec[:, ls]
        taps = [(k - 2, cw[k:k + 1, ls]) for k in range(4)]
        xr_ref[:, ls] = _slab_conv(rec_scr, s, taps, cb[:, ls], pos)


def _lru_in_call(xs, mods, layer, gain, w, cw, cb):
    return pl.pallas_call(
        _lru_in_kernel,
        grid=(NROWS // TM,),
        in_specs=_halo_specs() + [
            _mods_spec(layer, 0), _mods_spec(layer, 1), _const_spec((1, D)),
            _const_spec((2, D, D)), _const_spec((4, D)), _const_spec((1, D)),
        ],
        out_specs=[_row_spec(), _row_spec()],
        out_shape=[jax.ShapeDtypeStruct((NROWS, D), BF16),
                   jax.ShapeDtypeStruct((NROWS, D), F32)],
        scratch_shapes=[pltpu.VMEM((D // LANES, TM + 16, LANES), F32)],
        compiler_params=_cparams(1),
        name="lru_in_proj",
    )(xs, xs, xs, mods, mods, gain, w, cw, cb)


def _chunk_index(g, reverse):
    if not reverse:
        return g
    return jnp.where(g < CCH, CCH - 1 - g, NCH + CCH - 1 - g)


def _scan_kernel(xr_ref, lam_ref, wg_ref, ba_ref, bx_ref, *rest, reverse):
    if reverse:
        o_ref, a_scr, b_scr, h_scr = rest
    else:
        hsb_ref, gg_ref, x_ref, wo_ref, g1_ref, o_ref, a_scr, b_scr, h_scr = rest
    g = pl.program_id(0)

    @pl.when(g == 0)
    def _():
        h_scr[...] = jnp.zeros_like(h_scr)

    lam = lam_ref[...]
    half_log_a = (-0.5 * LRU_C) * (jnp.maximum(-lam, 0.0) + jnp.log1p(jnp.exp(-jnp.abs(lam))))
    for gq in range(D // GW):
        cols = slice(gq * GW, (gq + 1) * GW)
        xg = jnp.concatenate([xr_ref[b, :, cols] for b in range(B)], axis=0)
        pre = jnp.dot(xg.astype(BF16), wg_ref[gq], preferred_element_type=F32)
        t_r = jnp.tanh(pre[:, :GW] + ba_ref[:, cols])
        t_i = jnp.tanh(pre[:, GW:] + bx_ref[:, cols])
        log_a = half_log_a[:, cols] * t_r + half_log_a[:, cols]
        a = jnp.exp(log_a)
        y = (-1.0 - a * a) * jnp.tanh(log_a)
        root = jnp.where(y > 0.0, y * lax.rsqrt(y), 0.0)
        hx = 0.5 * xg
        bb = root * (hx * t_i + hx)
        for k2 in range(GW // LANES):
            k = gq * (GW // LANES) + k2
            for b in range(B):
                rows = slice(k * B * PITCH + b * PITCH, k * B * PITCH + b * PITCH + TS)
                a_scr[rows, :] = a[b * TS:(b + 1) * TS, k2 * LANES:(k2 + 1) * LANES]
                b_scr[rows, :] = bb[b * TS:(b + 1) * TS, k2 * LANES:(k2 + 1) * LANES]

    nslab = D // LANES

    def step(_, carry):
        t, hs = carry
        out = []
        for k in range(nslab):
            idx = pl.ds(k * B * PITCH + t, B, stride=PITCH)
            hk = a_scr[idx, :] * hs[k] + b_scr[idx, :]
            b_scr[idx, :] = hk
            out.append(hk)
        return (t - 1 if reverse else t + 1), tuple(out)

    t0 = jnp.int32(TS - 1 if reverse else 0)
    _, h_fin = lax.fori_loop(0, TS, step, (t0, tuple(h_scr[k] for k in range(nslab))))
    for k in range(nslab):
        h_scr[k] = h_fin[k]

    def slab_rows(b, k):
        return slice(k * B * PITCH + b * PITCH, k * B * PITCH + b * PITCH + TS)

    if reverse:
        for b in range(B):
            for k in range(nslab):
                o_ref[b, :, k * LANES:(k + 1) * LANES] = b_scr[slab_rows(b, k), :]
    else:
        z = jnp.concatenate([
            jnp.concatenate([
                ((b_scr[slab_rows(b, k), :] + hsb_ref[b, :, k * LANES:(k + 1) * LANES])
                 * gg_ref[b, :, k * LANES:(k + 1) * LANES].astype(F32)).astype(BF16)
                for k in range(nslab)], axis=1)
            for b in range(B)], axis=0)
        proj = jnp.dot(z, wo_ref[...], preferred_element_type=F32)
        is_ctx = g < CCH
        for b in range(B):
            g1 = jnp.where(is_ctx, g1_ref[8:9, :], g1_ref[b:b + 1, :])
            o_ref[b] = x_ref[b] + g1 * proj[b * TS:(b + 1) * TS]


def _scan_call(xr3, lam, wg, ba, bx, reverse, fwd_args=None):
    main = pl.BlockSpec((B, TS, D), lambda g: (0, _chunk_index(g, reverse), 0))
    in_specs = [
        main, _const_spec((1, D)),
        _const_spec((D // GW, GW, 2 * GW)), _const_spec((1, D)), _const_spec((1, D)),
    ]
    args = [xr3, lam, wg, 0.5 * ba, 0.5 * bx]
    if not reverse:
        hsb, gg, xs3, wo, mods, layer = fwd_args
        in_specs += [main, main, main, _const_spec((D, D)), _mods_spec(layer, 2)]
        args += [hsb, gg, xs3, wo, mods]
    slab = pltpu.VMEM((D // LANES * B * PITCH, LANES), F32)
    return pl.pallas_call(
        functools.partial(_scan_kernel, reverse=reverse),
        grid=(NCH,),
        in_specs=in_specs,
        out_specs=main,
        out_shape=jax.ShapeDtypeStruct((B, LT, D), F32),
        scratch_shapes=[slab, slab, pltpu.VMEM((D // LANES, B, LANES), F32)],
        compiler_params=_cparams(1),
        name="lru_scan_bwd" if reverse else "lru_scan_fwd",
    )(*args)


def _gate_weights(wa, wx):
    def bd(w):
        w4 = w.reshape(D // GW, GW // LRU_BLOCK_W, LRU_BLOCK_W, LRU_BLOCK_W)
        eye = jnp.eye(GW // LRU_BLOCK_W, dtype=w.dtype)
        return (w4[:, :, :, None, :] * eye[None, :, None, :, None]).reshape(D // GW, GW, GW)
    return (0.5 * jnp.concatenate([bd(wa), bd(wx)], axis=-1)).astype(BF16)


FSLAB = FC // LANES


def _ffn_kernel(xp_ref, x_ref, xn_ref, sh_ref, sc_ref, g2_ref, gain_ref, cw_ref, cb_ref, wd_ref,
                *rest):
    nchunk = 2 * D_FF // FC
    wu_ref = rest[:nchunk]
    o_ref, uv_scr, ug_scr = rest[nchunk:]
    b, pos = _tile_pos(TM, 0)
    _, pos_e = _tile_pos(TM + 16, -8)
    is_ctx_e = pos_e < C
    x = x_ref[...]
    x_e = jnp.concatenate([xp_ref[...], x, xn_ref[...]], axis=0)
    h = _modulate(x_e, gain_ref[...], _mod_rows(sh_ref, b, is_ctx_e),
                  _mod_rows(sc_ref, b, is_ctx_e))
    h = jnp.where(_dead_halo_rows(1), 0.0, h).astype(BF16)

    cw = cw_ref[...]
    cb = cb_ref[...]
    def up_project(ci):
        uv = jnp.dot(h, wu_ref[ci][...], preferred_element_type=F32)
        ug = jnp.dot(h, wu_ref[D_FF // FC + ci][...], preferred_element_type=F32)
        base = (ci % 2) * FSLAB
        for s in range(FSLAB):
            uv_scr[base + s] = uv[:, s * LANES:(s + 1) * LANES]
            ug_scr[base + s] = ug[:, s * LANES:(s + 1) * LANES]

    acc = jnp.zeros((TM, D), F32)
    up_project(0)
    for ci in range(D_FF // FC):
        if ci + 1 < D_FF // FC:
            up_project(ci + 1)
        base = (ci % 2) * FSLAB
        acts = []
        for s in range(FSLAB):
            lv = slice(ci * FC + s * LANES, ci * FC + (s + 1) * LANES)
            lg = slice(D_FF + lv.start, D_FF + lv.stop)
            val = _slab_conv(uv_scr, base + s, [(k - 1, cw[k:k + 1, lv]) for k in range(3)],
                             cb[:, lv], pos)
            hg = _slab_conv(ug_scr, base + s,
                            [(k - 1, 0.5 * cw[k:k + 1, lg]) for k in range(3)],
                            0.5 * cb[:, lg], pos)
            acts.append((val * (hg + hg * jnp.tanh(hg))).astype(BF16))
        acc = acc + jnp.dot(jnp.concatenate(acts, axis=1), wd_ref[ci],
                            preferred_element_type=F32)
    o_ref[...] = x + _mod_rows(g2_ref, b, pos < C) * acc


def _ffn_call(xs, mods, layer, gain, wu, cw, cb, wd):
    nchunk = 2 * D_FF // FC
    slab = pltpu.VMEM((2 * FSLAB, TM + 16, LANES), F32)
    return pl.pallas_call(
        _ffn_kernel,
        grid=(NROWS // TM,),
        in_specs=_halo_specs() + [
            _mods_spec(layer, 3), _mods_spec(layer, 4), _mods_spec(layer, 5),
            _const_spec((1, D)), _const_spec((3, 2 * D_FF)),
            _const_spec((1, 2 * D_FF)), _const_spec((D_FF // FC, FC, D)),
        ] + [pl.BlockSpec((D, FC), lambda i, c=c: (0, c), pipeline_mode=pl.Buffered(1))
             for c in range(nchunk)],
        out_specs=_row_spec(),
        out_shape=jax.ShapeDtypeStruct((NROWS, D), F32),
        scratch_shapes=[slab, slab],
        compiler_params=_cparams(1),
        name="conv_ffn",
    )(xs, xs, xs, mods, mods, mods, gain, cw, cb, wd, *([wu] * nchunk))


def _split_cols(w, n):
    return w.astype(BF16).reshape(D, n, D).transpose(1, 0, 2)


def kernel(x, c, ctx, c_ctx, ada_w, ada_b, norm_mix, norm_ffn, na_w_qkv, na_q_gain, na_k_gain,
           na_rpb, na_w_out, lru_w_in, lru_conv_w, lru_conv_b, lru_ga_w, lru_ga_b, lru_gx_w,
           lru_gx_b, lru_lambda, lru_w_out, ffn_w_up, ffn_conv_w, ffn_conv_b, ffn_w_down):
    xs = jnp.concatenate([ctx, x], axis=1).reshape(NROWS, D)
    cc = jnp.concatenate([c, c_ctx[None], jnp.zeros((7, D), F32)], axis=0)
    mods = _mods_call(cc, ada_w, ada_b)

    head_of_lane = jnp.arange(D) // HD
    g1 = (head_of_lane[:, None] == jnp.arange(LANES)[None, :]).astype(BF16)

    for i in range(DEPTH):
        j = i // 2
        gain_mix = norm_mix[i][None]
        if i % 2 == 0:
            qg = (jnp.tile(na_q_gain[j], H) * (HD ** -0.5))[None]
            kg = jnp.tile(na_k_gain[j], H)[None]
            q, k, v = _qkv_call(xs, mods, i, gain_mix, _split_cols(na_w_qkv[j], 3), qg, kg, g1)
            o = _attn_call(q, k, v, _bias_table(na_rpb[j]))
            xs = _proj_call(xs, o, mods, i, na_w_out[j].astype(BF16))
        else:
            gg, xr = _lru_in_call(xs, mods, i, gain_mix, _split_cols(lru_w_in[j], 2),
                                  lru_conv_w[j], lru_conv_b[j][None])
            xr3 = xr.reshape(B, LT, D)
            hsb = _scan_call(xr3, lru_lambda[j, 1][None],
                             _gate_weights(lru_ga_w[j, 1], lru_gx_w[j, 1]),
                             lru_ga_b[j, 1][None], lru_gx_b[j, 1][None], True)
            xs = _scan_call(xr3, lru_lambda[j, 0][None],
                            _gate_weights(lru_ga_w[j, 0], lru_gx_w[j, 0]),
                            lru_ga_b[j, 0][None], lru_gx_b[j, 0][None], False,
                            (hsb, gg.reshape(B, LT, D), xs.reshape(B, LT, D),
                             lru_w_out[j].astype(BF16), mods, i)).reshape(NROWS, D)
        xs = _ffn_call(xs, mods, i, norm_ffn[i][None], ffn_w_up[i].astype(BF16),
                       ffn_conv_w[i], ffn_conv_b[i][None],
                       ffn_w_down[i].astype(BF16).reshape(D_FF // FC, FC, D))
    return xs.reshape(B, LT, D)[:, C:]
```

```python
import functools
from typing import NamedTuple

import jax
import jax.numpy as jnp
from jax import lax
from jax.experimental import pallas as pl
from jax.experimental.pallas import tpu as pltpu

F32 = jnp.float32
BF16 = jnp.bfloat16

D = 1024
B = 8
S = 4096
DEPTH = 4
C = 256
LT = C + S
NROWS = B * LT
GRID_W = 64
IMG_ROWS = S // GRID_W
H = 16
HD = 64
NPAIR = H // 2
LANES = 128
WIN_ROWS = 8
WIN_COLS = 16
D_FF = 3 * D
LRU_BLOCK_W = 64
LRU_C = 8.0
EPS = 1e-6
NEG_INF = float("-inf")

TM = 544
TPB = LT // TM
FC = 512
QB = 256
NQB = LT // QB
TS = 64
NCH = LT // TS
CCH = C // TS
PITCH = TS + 4
GW = 256
VMEM_LIMIT = 56 * 1024 * 1024

T2_LEFT_DEAD = 16
T2_RIGHT_DEAD = 17
T2_DEAD = 18
T2_ENTRIES = 19


def _cparams(n_axes):
    return pltpu.CompilerParams(
        dimension_semantics=("arbitrary",) * n_axes, vmem_limit_bytes=VMEM_LIMIT)


def _const_spec(shape):
    nd = len(shape)
    return pl.BlockSpec(shape, lambda *_: (0,) * nd, pipeline_mode=pl.Buffered(1))


def _mods_kernel(cc_ref, w_ref, b_ref, o_ref):
    c = cc_ref[...]
    s = c * jax.nn.sigmoid(c)
    o_ref[...] = jnp.dot(s.astype(BF16), w_ref[...].astype(BF16),
                         preferred_element_type=F32) + b_ref[...]


def _mods_call(cc, ada_w, ada_b):
    return pl.pallas_call(
        _mods_kernel,
        grid=(DEPTH, 6),
        in_specs=[
            pl.BlockSpec((16, D), lambda l, k: (0, 0)),
            pl.BlockSpec((None, D, D), lambda l, k: (l, 0, k)),
            pl.BlockSpec((None, None, 1, D), lambda l, k: (l, k, 0, 0)),
        ],
        out_specs=pl.BlockSpec((None, None, 16, D), lambda l, k: (l, k, 0, 0)),
        out_shape=jax.ShapeDtypeStruct((DEPTH, 6, 16, D), F32),
        compiler_params=_cparams(2),
        name="adaln_mods",
    )(cc, ada_w, ada_b.reshape(DEPTH, 6, 1, D))


class Tiling(NamedTuple):
    tm: int
    tpb: int
    base: int
    main_blocks: int


STREAM = Tiling(TM, TPB, 0, 1)
LATENT = Tiling(512, S // 512, C, 2)


def _tile_pos(n, lo, t=STREAM):
    i = pl.program_id(0)
    b = i // t.tpb
    j = i % t.tpb
    pos = t.base + j * t.tm + lo + lax.broadcasted_iota(jnp.int32, (n, 1), 0)
    return b, pos


def _mod_rows(m_ref, b, is_ctx):
    return jnp.where(is_ctx, m_ref[8:9, :], m_ref[pl.ds(b, 1), :])


def _modulate(x, gain, shift, scale):
    y = x * lax.rsqrt(jnp.mean(x * x, axis=-1, keepdims=True) + EPS)
    return (y * gain) * (1.0 + scale) + shift


def _mods_spec(layer, kind):
    return pl.BlockSpec((None, None, 16, D), lambda *_: (layer, kind, 0, 0))


def _row_spec(width=D):
    return pl.BlockSpec((TM, width), lambda i: (i, 0))


def _pair_spec():
    return pl.BlockSpec((NPAIR, TM, LANES), lambda i: (0, i, 0))


def _qkv_kernel(x_ref, sh_ref, sc_ref, gain_ref, w_ref, qg_ref, kg_ref, g1_ref,
                q_ref, k_ref, v_ref):
    b, pos = _tile_pos(TM, 0)
    is_ctx = pos < C
    h = _modulate(x_ref[...], gain_ref[...], _mod_rows(sh_ref, b, is_ctx),
                  _mod_rows(sc_ref, b, is_ctx)).astype(BF16)
    head_col = lax.shift_right_logical(lax.broadcasted_iota(jnp.int32, (TM, LANES), 1), 6)
    for idx, (o_ref, gn_ref) in enumerate(((q_ref, qg_ref), (k_ref, kg_ref))):
        acc = jnp.dot(h, w_ref[idx], preferred_element_type=F32)
        ss = jnp.dot((acc * acc).astype(BF16), g1_ref[...], preferred_element_type=F32)
        rstd = lax.rsqrt(ss * (1.0 / HD) + EPS)
        for p in range(NPAIR):
            ls = slice(p * LANES, (p + 1) * LANES)
            rb = jnp.take_along_axis(rstd, head_col + 2 * p, axis=1, mode="promise_in_bounds")
            o_ref[p] = (acc[:, ls] * rb * gn_ref[:, ls]).astype(BF16)
    v = jnp.dot(h, w_ref[2], preferred_element_type=F32).astype(BF16)
    first_head = lax.broadcasted_iota(jnp.int32, (1, LANES), 1) < HD
    one = jnp.ones((), BF16)
    for p in range(NPAIR):
        vp = v[:, p * LANES:(p + 1) * LANES]
        v_ref[0, p] = jnp.where(first_head, vp, one)
        v_ref[1, p] = jnp.where(first_head, one, vp)


def _qkv_call(xs, mods, layer, gain, w, qg, kg, g1):
    out = jax.ShapeDtypeStruct((NPAIR, NROWS, LANES), BF16)
    return pl.pallas_call(
        _qkv_kernel,
        grid=(NROWS // TM,),
        in_specs=[
            _row_spec(), _mods_spec(layer, 0), _mods_spec(layer, 1), _const_spec((1, D)),
            _const_spec((3, D, D)), _const_spec((1, D)), _const_spec((1, D)),
            _const_spec((D, LANES)),
        ],
        out_specs=[_pair_spec(), _pair_spec(),
                   pl.BlockSpec((2, NPAIR, TM, LANES), lambda i: (0, 0, i, 0))],
        out_shape=[out, out, jax.ShapeDtypeStruct((2, NPAIR, NROWS, LANES), BF16)],
        compiler_params=_cparams(1),
        name="na_qkv",
    )(xs, mods, mods, gain, w, qg, kg, g1)


def _dot_t(a, b):
    return lax.dot_general(a, b, (((1,), (1,)), ((), ())), preferred_element_type=F32)


def _attn_kernel(q_ref, kc_ref, vc_ref, k0_ref, k1_ref, k2_ref, v0_ref, v1_ref, v2_ref,
                 t2_ref, o_ref):
    jj = pl.program_id(1)
    rg = jj - 1
    win0 = jnp.clip(rg, 1, NQB - 3) - 1
    off = 4 * win0 - 4 * rg + (WIN_ROWS - 1)
    tile_idx = []
    for i in range(4):
        rs = jnp.clip(4 * rg + i - WIN_ROWS // 2, 0, IMG_ROWS - WIN_ROWS)
        row = []
        for pu in range(6):
            kl = 4 * win0 + 2 * pu
            lv = (kl >= rs) & (kl < rs + WIN_ROWS) & (jj > 0)
            rv = (kl + 1 >= rs) & (kl + 1 < rs + WIN_ROWS) & (jj > 0)
            d = jnp.clip(2 * pu - i + off, -1, 14) + 1
            row.append(jnp.where(lv & rv, d, jnp.where(
                rv, T2_LEFT_DEAD, jnp.where(lv, T2_RIGHT_DEAD, T2_DEAD))))
        tile_idx.append(row)
    lane = lax.broadcasted_iota(jnp.int32, (1, LANES), 1)
    k_refs = (k0_ref, k1_ref, k2_ref)
    v_refs = (v0_ref, v1_ref, v2_ref)

    def scores(head):
        p, hh = divmod(head, 2)
        qp = q_ref[p]
        sel = (lane < HD) if hh == 0 else (lane >= HD)
        qm = jnp.where(sel, qp, jnp.zeros_like(qp))
        s_all = [_dot_t(qm, kc_ref[p])]
        for j in range(3):
            bias = jnp.concatenate(
                [jnp.concatenate([t2_ref[head, tile_idx[i][2 * j + cp]] for cp in range(2)],
                                 axis=1) for i in range(4)], axis=0)
            s_all.append(_dot_t(qm, k_refs[j][p]) + bias)
        mm = jnp.maximum(jnp.maximum(s_all[0], s_all[1]), jnp.maximum(s_all[2], s_all[3]))
        m = jnp.max(jnp.maximum(mm[:, :LANES], mm[:, LANES:]), axis=-1, keepdims=True)
        return s_all, m

    def weighted_values(head, s_all, m):
        p, hh = divmod(head, 2)
        acc = jnp.dot(jnp.exp(s_all[0] - m).astype(BF16), vc_ref[hh, p],
                      preferred_element_type=F32)
        for j in range(3):
            acc = acc + jnp.dot(jnp.exp(s_all[j + 1] - m).astype(BF16), v_refs[j][hh, p],
                                preferred_element_type=F32)
        return acc / pltpu.roll(acc, HD, axis=1)

    skew = 1
    pending = [scores(h) for h in range(skew)]
    outs = []
    for head in range(H):
        if head + skew < H:
            pending.append(scores(head + skew))
        outs.append(weighted_values(head, *pending.pop(0)))
        if head % 2 == 1:
            o_ref[head // 2] = jnp.where(lane < HD, outs[0], outs[1]).astype(BF16)
            outs = []


def _attn_call(q, k, v, t2):
    def blk(fn):
        return pl.BlockSpec((NPAIR, QB, LANES), lambda b, jj: (0, fn(b, jj), 0))

    def vblk(fn):
        return pl.BlockSpec((2, NPAIR, QB, LANES), lambda b, jj: (0, 0, fn(b, jj), 0))

    def win(o):
        return lambda b, jj: b * NQB + jnp.clip(jj - 1, 1, NQB - 3) + o

    return pl.pallas_call(
        _attn_kernel,
        grid=(B, NQB),
        in_specs=[
            blk(lambda b, jj: b * NQB + jj),
            blk(lambda b, jj: b * NQB), vblk(lambda b, jj: b * NQB),
            blk(win(0)), blk(win(1)), blk(win(2)),
            vblk(win(0)), vblk(win(1)), vblk(win(2)),
            _const_spec((H, T2_ENTRIES, GRID_W, LANES)),
        ],
        out_specs=blk(lambda b, jj: b * NQB + jj),
        out_shape=jax.ShapeDtypeStruct((NPAIR, NROWS, LANES), BF16),
        compiler_params=_cparams(2),
        name="na_attention",
    )(q, k, v, k, k, k, v, v, v, t2)


def _proj_kernel(x_ref, y_ref, gate_ref, w_ref, o_ref):
    b, pos = _tile_pos(TM, 0)
    y = jnp.concatenate([y_ref[p] for p in range(NPAIR)], axis=1)
    o_ref[...] = x_ref[...] + _mod_rows(gate_ref, b, pos < C) * jnp.dot(
        y, w_ref[...], preferred_element_type=F32)


def _proj_call(xs, y, mods, layer, w):
    return pl.pallas_call(
        _proj_kernel,
        grid=(NROWS // TM,),
        in_specs=[_row_spec(), _pair_spec(), _mods_spec(layer, 2), _const_spec((D, D))],
        out_specs=_row_spec(),
        out_shape=jax.ShapeDtypeStruct((NROWS, D), F32),
        compiler_params=_cparams(1),
        name="na_out_proj",
    )(xs, y, mods, w)


def _bias_table(rpb):
    qc = jnp.arange(GRID_W)[:, None]
    kc = jnp.arange(GRID_W)[None, :]
    ws = jnp.clip(qc - WIN_COLS // 2, 0, GRID_W - WIN_COLS)
    col_ok = (kc >= ws) & (kc < ws + WIN_COLS)
    onehot = ((kc - qc + (WIN_COLS - 1))[None] == jnp.arange(2 * WIN_COLS - 1)[:, None, None])
    t = jnp.einsum("hdc,cqk->hdqk", rpb, onehot.astype(F32), precision=lax.Precision.HIGHEST)
    t = jnp.where(col_ok[None, None], t, NEG_INF)
    dead = jnp.full((H, 1, GRID_W, GRID_W), NEG_INF, F32)
    t_ext = jnp.concatenate([dead, t, dead], axis=1)
    pairs = jnp.concatenate([t_ext[:, :-1], t_ext[:, 1:]], axis=-1)
    lo, hi = WIN_ROWS // 2 - 1, WIN_ROWS // 2 - 1 + WIN_ROWS - 1
    left_dead = jnp.concatenate([dead, t[:, lo:lo + 1]], axis=-1)
    right_dead = jnp.concatenate([t[:, hi:hi + 1], dead], axis=-1)
    all_dead = jnp.concatenate([dead, dead], axis=-1)
    return jnp.concatenate([pairs, left_dead, right_dead, all_dead], axis=1)


assert 8 < C % TM < TM - 8
BND = C % TM
BND_LO = BND - 8


def _halo_specs():
    t8 = TM // 8
    return [
        pl.BlockSpec((8, D), lambda i: (jnp.maximum(i * t8 - 1, 0), 0)),
        _row_spec(),
        pl.BlockSpec((8, D), lambda i: (jnp.minimum((i + 1) * t8, NROWS // 8 - 1), 0)),
    ]


def _dead_halo_rows(reach_back, t=STREAM):
    start = t.base + (pl.program_id(0) % t.tpb) * t.tm
    e = lax.broadcasted_iota(jnp.int32, (t.tm + 16, 1), 0)
    at_start = (start == 0) | (start == C)
    at_end = (start + t.tm == C) | (start + t.tm == LT)
    return ((e >= 8 - reach_back) & (e < 8) & at_start) | ((e == t.tm + 8) & at_end)


def _slab_conv(u_scr, s, taps, cb, pos, t=STREAM):
    def seg(lo, n, masked):
        y = cb
        for off, w in taps:
            u = u_scr[s, pl.ds(lo + 8 + off, n), :]
            if masked and off != 0:
                p = pos[lo:lo + n]
                crosses = (p >= C) & (p < C - off) if off < 0 else (p < C) & (p >= C - off)
                u = jnp.where(crosses, 0.0, u)
            y = y + u * w
        return y
    if t is LATENT:
        return seg(0, t.tm, False)
    return jnp.concatenate([seg(0, BND_LO, False), seg(BND_LO, 16, True),
                            seg(BND_LO + 16, TM - BND_LO - 16, False)], axis=0)


def _lru_in_kernel(xp_ref, x_ref, xn_ref, sh_ref, sc_ref, gain_ref, w_ref, cw_ref, cb_ref,
                   gg_ref, xr_ref, rec_scr):
    b, pos = _tile_pos(TM, 0)
    _, pos_e = _tile_pos(TM + 16, -8)
    is_ctx_e = pos_e < C
    x_e = jnp.concatenate([xp_ref[...], x_ref[...], xn_ref[...]], axis=0)
    h = _modulate(x_e, gain_ref[...], _mod_rows(sh_ref, b, is_ctx_e),
                  _mod_rows(sc_ref, b, is_ctx_e))
    h = jnp.where(_dead_halo_rows(2), 0.0, h).astype(BF16)
    gate = jnp.dot(h, w_ref[0], preferred_element_type=F32)[8:8 + TM]
    gg_ref[...] = jax.nn.gelu(gate, approximate=True).astype(BF16)
    rec = jnp.dot(h, w_ref[1], preferred_element_type=F32)
    cw = cw_ref[...]
    cb = cb_ref[...]
    for s in range(D // LANES):
        ls = slice(s * LANES, (s + 1) * LANES)
        rec_scr[s] = rec[:, ls]
        taps = [(k - 2, cw[k:k + 1, ls]) for k in range(4)]
        xr_ref[:, ls] = _slab_conv(rec_scr, s, taps, cb[:, ls], pos)


def _lru_in_call(xs, mods, layer, gain, w, cw, cb):
    return pl.pallas_call(
        _lru_in_kernel,
        grid=(NROWS // TM,),
        in_specs=_halo_specs() + [
            _mods_spec(layer, 0), _mods_spec(layer, 1), _const_spec((1, D)),
            _const_spec((2, D, D)), _const_spec((4, D)), _const_spec((1, D)),
        ],
        out_specs=[_row_spec(), _row_spec()],
        out_shape=[jax.ShapeDtypeStruct((NROWS, D), BF16),
                   jax.ShapeDtypeStruct((NROWS, D), F32)],
        scratch_shapes=[pltpu.VMEM((D // LANES, TM + 16, LANES), F32)],
        compiler_params=_cparams(1),
        name="lru_in_proj",
    )(xs, xs, xs, mods, mods, gain, w, cw, cb)


def _chunk_index(g, reverse):
    if not reverse:
        return g
    return jnp.where(g < CCH, CCH - 1 - g, NCH + CCH - 1 - g)


def _scan_kernel(xr_ref, lam_ref, wg_ref, ba_ref, bx_ref, *rest, reverse):
    if reverse:
        o_ref, a_scr, b_scr, h_scr = rest
    else:
        hsb_ref, gg_ref, x_ref, wo_ref, g1_ref, o_ref, a_scr, b_scr, h_scr = rest
    g = pl.program_id(0)

    @pl.when(g == 0)
    def _():
        h_scr[...] = jnp.zeros_like(h_scr)

    lam = lam_ref[...]
    half_log_a = (-0.5 * LRU_C) * (jnp.maximum(-lam, 0.0) + jnp.log1p(jnp.exp(-jnp.abs(lam))))
    for gq in range(D // GW):
        cols = slice(gq * GW, (gq + 1) * GW)
        xg = jnp.concatenate([xr_ref[b, :, cols] for b in range(B)], axis=0)
        pre = jnp.dot(xg.astype(BF16), wg_ref[gq], preferred_element_type=F32)
        t_r = jnp.tanh(pre[:, :GW] + ba_ref[:, cols])
        t_i = jnp.tanh(pre[:, GW:] + bx_ref[:, cols])
        log_a = half_log_a[:, cols] * t_r + half_log_a[:, cols]
        a = jnp.exp(log_a)
        y = (-1.0 - a * a) * jnp.tanh(log_a)
        root = jnp.where(y > 0.0, y * lax.rsqrt(y), 0.0)
        hx = 0.5 * xg
        bb = root * (hx * t_i + hx)
        for k2 in range(GW // LANES):
            k = gq * (GW // LANES) + k2
            for b in range(B):
                rows = slice(k * B * PITCH + b * PITCH, k * B * PITCH + b * PITCH + TS)
                a_scr[rows, :] = a[b * TS:(b + 1) * TS, k2 * LANES:(k2 + 1) * LANES]
                b_scr[rows, :] = bb[b * TS:(b + 1) * TS, k2 * LANES:(k2 + 1) * LANES]

    nslab = D // LANES

    def step(_, carry):
        t, hs = carry
        out = []
        for k in range(nslab):
            idx = pl.ds(k * B * PITCH + t, B, stride=PITCH)
            hk = a_scr[idx, :] * hs[k] + b_scr[idx, :]
            b_scr[idx, :] = hk
            out.append(hk)
        return (t - 1 if reverse else t + 1), tuple(out)

    t0 = jnp.int32(TS - 1 if reverse else 0)
    _, h_fin = lax.fori_loop(0, TS, step, (t0, tuple(h_scr[k] for k in range(nslab))))
    for k in range(nslab):
        h_scr[k] = h_fin[k]

    def slab_rows(b, k):
        return slice(k * B * PITCH + b * PITCH, k * B * PITCH + b * PITCH + TS)

    if reverse:
        for b in range(B):
            for k in range(nslab):
                o_ref[b, :, k * LANES:(k + 1) * LANES] = b_scr[slab_rows(b, k), :].astype(BF16)
    else:
        z = jnp.concatenate([
            jnp.concatenate([
                ((b_scr[slab_rows(b, k), :] + hsb_ref[b, :, k * LANES:(k + 1) * LANES])
                 * gg_ref[b, :, k * LANES:(k + 1) * LANES].astype(F32)).astype(BF16)
                for k in range(nslab)], axis=1)
            for b in range(B)], axis=0)
        proj = jnp.dot(z, wo_ref[...], preferred_element_type=F32)
        is_ctx = g < CCH
        for b in range(B):
            g1 = jnp.where(is_ctx, g1_ref[8:9, :], g1_ref[b:b + 1, :])
            o_ref[b] = x_ref[b] + g1 * proj[b * TS:(b + 1) * TS]


def _scan_call(xr3, lam, wg, ba, bx, reverse, fwd_args=None):
    main = pl.BlockSpec((B, TS, D), lambda g: (0, _chunk_index(g, reverse), 0))
    in_specs = [
        main, _const_spec((1, D)),
        _const_spec((D // GW, GW, 2 * GW)), _const_spec((1, D)), _const_spec((1, D)),
    ]
    args = [xr3, lam, wg, 0.5 * ba, 0.5 * bx]
    if not reverse:
        hsb, gg, xs3, wo, mods, layer = fwd_args
        in_specs += [main, main, main, _const_spec((D, D)), _mods_spec(layer, 2)]
        args += [hsb, gg, xs3, wo, mods]
    slab = pltpu.VMEM((D // LANES * B * PITCH, LANES), F32)
    return pl.pallas_call(
        functools.partial(_scan_kernel, reverse=reverse),
        grid=(NCH,),
        in_specs=in_specs,
        out_specs=main,
        out_shape=jax.ShapeDtypeStruct((B, LT, D), BF16 if reverse else F32),
        scratch_shapes=[slab, slab, pltpu.VMEM((D // LANES, B, LANES), F32)],
        compiler_params=_cparams(1),
        name="lru_scan_bwd" if reverse else "lru_scan_fwd",
    )(*args)


def _gate_weights(wa, wx):
    def bd(w):
        w4 = w.reshape(D // GW, GW // LRU_BLOCK_W, LRU_BLOCK_W, LRU_BLOCK_W)
        eye = jnp.eye(GW // LRU_BLOCK_W, dtype=w.dtype)
        return (w4[:, :, :, None, :] * eye[None, :, None, :, None]).reshape(D // GW, GW, GW)
    return (0.5 * jnp.concatenate([bd(wa), bd(wx)], axis=-1)).astype(BF16)


FSLAB = FC // LANES


def _ffn_kernel(*refs, t):
    nchunk = 2 * D_FF // FC
    xp_ref, *x_refs = refs[:1 + t.main_blocks]
    (xn_ref, sh_ref, sc_ref, g2_ref, gain_ref, cw_ref, cb_ref,
     wd_ref) = refs[1 + t.main_blocks:9 + t.main_blocks]
    wu_ref = refs[9 + t.main_blocks:9 + t.main_blocks + nchunk]
    o_ref, uv_scr, ug_scr = refs[9 + t.main_blocks + nchunk:]
    tm = t.tm
    b, pos = _tile_pos(tm, 0, t)
    _, pos_e = _tile_pos(tm + 16, -8, t)
    is_ctx_e = pos_e < C
    x = jnp.concatenate([r[...] for r in x_refs], axis=0)
    x_e = jnp.concatenate([xp_ref[...], x, xn_ref[...]], axis=0)
    h = _modulate(x_e, gain_ref[...], _mod_rows(sh_ref, b, is_ctx_e),
                  _mod_rows(sc_ref, b, is_ctx_e))
    h = jnp.where(_dead_halo_rows(1, t), 0.0, h).astype(BF16)

    cw = cw_ref[...]
    cb = cb_ref[...]
    def up_project(ci):
        uv = jnp.dot(h, wu_ref[ci][...], preferred_element_type=F32)
        ug = jnp.dot(h, wu_ref[D_FF // FC + ci][...], preferred_element_type=F32)
        base = (ci % 2) * FSLAB
        for s in range(FSLAB):
            uv_scr[base + s] = uv[:, s * LANES:(s + 1) * LANES]
            ug_scr[base + s] = ug[:, s * LANES:(s + 1) * LANES]

    acc = jnp.zeros((tm, D), F32)
    up_project(0)
    for ci in range(D_FF // FC):
        if ci + 1 < D_FF // FC:
            up_project(ci + 1)
        base = (ci % 2) * FSLAB
        acts = []
        for s in range(FSLAB):
            lv = slice(ci * FC + s * LANES, ci * FC + (s + 1) * LANES)
            lg = slice(D_FF + lv.start, D_FF + lv.stop)
            val = _slab_conv(uv_scr, base + s, [(k - 1, cw[k:k + 1, lv]) for k in range(3)],
                             cb[:, lv], pos, t)
            hg = _slab_conv(ug_scr, base + s,
                            [(k - 1, 0.5 * cw[k:k + 1, lg]) for k in range(3)],
                            0.5 * cb[:, lg], pos, t)
            acts.append((val * (hg + hg * jnp.tanh(hg))).astype(BF16))
        acc = acc + jnp.dot(jnp.concatenate(acts, axis=1), wd_ref[ci],
                            preferred_element_type=F32)
    o_ref[...] = x + _mod_rows(g2_ref, b, pos < C) * acc


def _ffn_call(xs, mods, layer, gain, wu, cw, cb, wd, t):
    nchunk = 2 * D_FF // FC
    slab = pltpu.VMEM((2 * FSLAB, t.tm + 16, LANES), F32)
    if t is STREAM:
        x_specs, out_rows = _halo_specs(), NROWS
    else:
        def row0(i):
            return (i // t.tpb) * LT + t.base + (i % t.tpb) * t.tm
        half = t.tm // t.main_blocks
        x_specs = (
            [pl.BlockSpec((8, D), lambda i: (row0(i) // 8 - 1, 0))]
            + [pl.BlockSpec((half, D), lambda i, k=k: (row0(i) // half + k, 0))
               for k in range(t.main_blocks)]
            + [pl.BlockSpec((8, D), lambda i: (jnp.minimum((row0(i) + t.tm) // 8,
                                                            NROWS // 8 - 1), 0))])
        out_rows = B * S
    return pl.pallas_call(
        functools.partial(_ffn_kernel, t=t),
        grid=(out_rows // t.tm,),
        in_specs=x_specs + [
            _mods_spec(layer, 3), _mods_spec(layer, 4), _mods_spec(layer, 5),
            _const_spec((1, D)), _const_spec((3, 2 * D_FF)),
            _const_spec((1, 2 * D_FF)), _const_spec((D_FF // FC, FC, D)),
        ] + [pl.BlockSpec((D, FC), lambda i, c=c: (0, c), pipeline_mode=pl.Buffered(1))
             for c in range(nchunk)],
        out_specs=pl.BlockSpec((t.tm, D), lambda i: (i, 0)),
        out_shape=jax.ShapeDtypeStruct((out_rows, D), F32),
        scratch_shapes=[slab, slab],
        compiler_params=_cparams(1),
        name="conv_ffn",
    )(*([xs] * (2 + t.main_blocks)), mods, mods, mods, gain, cw, cb, wd, *([wu] * nchunk))


def _split_cols(w, n):
    return w.astype(BF16).reshape(D, n, D).transpose(1, 0, 2)


def kernel(x, c, ctx, c_ctx, ada_w, ada_b, norm_mix, norm_ffn, na_w_qkv, na_q_gain, na_k_gain,
           na_rpb, na_w_out, lru_w_in, lru_conv_w, lru_conv_b, lru_ga_w, lru_ga_b, lru_gx_w,
           lru_gx_b, lru_lambda, lru_w_out, ffn_w_up, ffn_conv_w, ffn_conv_b, ffn_w_down):
    xs = jnp.concatenate([ctx, x], axis=1).reshape(NROWS, D)
    cc = jnp.concatenate([c, c_ctx[None], jnp.zeros((7, D), F32)], axis=0)
    mods = _mods_call(cc, ada_w, ada_b)

    head_of_lane = jnp.arange(D) // HD
    g1 = (head_of_lane[:, None] == jnp.arange(LANES)[None, :]).astype(BF16)

    for i in range(DEPTH):
        j = i // 2
        gain_mix = norm_mix[i][None]
        if i % 2 == 0:
            qg = (jnp.tile(na_q_gain[j], H) * (HD ** -0.5))[None]
            kg = jnp.tile(na_k_gain[j], H)[None]
            q, k, v = _qkv_call(xs, mods, i, gain_mix, _split_cols(na_w_qkv[j], 3), qg, kg, g1)
            o = _attn_call(q, k, v, _bias_table(na_rpb[j]))
            xs = _proj_call(xs, o, mods, i, na_w_out[j].astype(BF16))
        else:
            gg, xr = _lru_in_call(xs, mods, i, gain_mix, _split_cols(lru_w_in[j], 2),
                                  lru_conv_w[j], lru_conv_b[j][None])
            xr3 = xr.reshape(B, LT, D)
            hsb = _scan_call(xr3, lru_lambda[j, 1][None],
                             _gate_weights(lru_ga_w[j, 1], lru_gx_w[j, 1]),
                             lru_ga_b[j, 1][None], lru_gx_b[j, 1][None], True)
            xs = _scan_call(xr3, lru_lambda[j, 0][None],
                            _gate_weights(lru_ga_w[j, 0], lru_gx_w[j, 0]),
                            lru_ga_b[j, 0][None], lru_gx_b[j, 0][None], False,
                            (hsb, gg.reshape(B, LT, D), xs.reshape(B, LT, D),
                             lru_w_out[j].astype(BF16), mods, i)).reshape(NROWS, D)
        xs = _ffn_call(xs, mods, i, norm_ffn[i][None], ffn_w_up[i].astype(BF16),
                       ffn_conv_w[i], ffn_conv_b[i][None],
                       ffn_w_down[i].astype(BF16).reshape(D_FF // FC, FC, D),
                       LATENT if i == DEPTH - 1 else STREAM)
    return xs.reshape(B, S, D)
```

```python
import functools
from typing import NamedTuple

import jax
import jax.numpy as jnp
from jax import lax
from jax.experimental import pallas as pl
from jax.experimental.pallas import tpu as pltpu

F32 = jnp.float32
BF16 = jnp.bfloat16

D = 1024
B = 8
S = 4096
DEPTH = 4
C = 256
LT = C + S
NROWS = B * LT
GRID_W = 64
IMG_ROWS = S // GRID_W
H = 16
HD = 64
NPAIR = H // 2
LANES = 128
WIN_ROWS = 8
WIN_COLS = 16
D_FF = 3 * D
LRU_BLOCK_W = 64
LRU_C = 8.0
EPS = 1e-6
NEG_INF = float("-inf")
LOG2_E = 1.4426950408889634

TM = 544
TPB = LT // TM
FC = 512
QB = 256
NQB = LT // QB
TS = 64
NCH = LT // TS
CCH = C // TS
PITCH = TS + 4
GW = 256
VMEM_LIMIT = 56 * 1024 * 1024

T2_LEFT_DEAD = 16
T2_RIGHT_DEAD = 17
T2_DEAD = 18
T2_ENTRIES = 19


def _cparams(n_axes):
    return pltpu.CompilerParams(
        dimension_semantics=("arbitrary",) * n_axes, vmem_limit_bytes=VMEM_LIMIT)


def _const_spec(shape):
    nd = len(shape)
    return pl.BlockSpec(shape, lambda *_: (0,) * nd, pipeline_mode=pl.Buffered(1))


def _mods_kernel(cc_ref, w_ref, b_ref, o_ref):
    c = cc_ref[...]
    s = c * jax.nn.sigmoid(c)
    o_ref[...] = jnp.dot(s.astype(BF16), w_ref[...].astype(BF16),
                         preferred_element_type=F32) + b_ref[...]


def _mods_call(cc, ada_w, ada_b):
    return pl.pallas_call(
        _mods_kernel,
        grid=(DEPTH, 6),
        in_specs=[
            pl.BlockSpec((16, D), lambda l, k: (0, 0)),
            pl.BlockSpec((None, D, D), lambda l, k: (l, 0, k)),
            pl.BlockSpec((None, None, 1, D), lambda l, k: (l, k, 0, 0)),
        ],
        out_specs=pl.BlockSpec((None, None, 16, D), lambda l, k: (l, k, 0, 0)),
        out_shape=jax.ShapeDtypeStruct((DEPTH, 6, 16, D), F32),
        compiler_params=_cparams(2),
        name="adaln_mods",
    )(cc, ada_w, ada_b.reshape(DEPTH, 6, 1, D))


class Tiling(NamedTuple):
    tm: int
    tpb: int
    base: int
    main_blocks: int


STREAM = Tiling(TM, TPB, 0, 1)
LATENT = Tiling(512, S // 512, C, 2)


def _tile_pos(n, lo, t=STREAM):
    i = pl.program_id(0)
    b = i // t.tpb
    j = i % t.tpb
    pos = t.base + j * t.tm + lo + lax.broadcasted_iota(jnp.int32, (n, 1), 0)
    return b, pos


def _mod_rows(m_ref, b, is_ctx):
    return jnp.where(is_ctx, m_ref[8:9, :], m_ref[pl.ds(b, 1), :])


def _modulate(x, gain, shift, scale):
    y = x * lax.rsqrt(jnp.mean(x * x, axis=-1, keepdims=True) + EPS)
    return (y * gain) * (1.0 + scale) + shift


def _mods_spec(layer, kind):
    return pl.BlockSpec((None, None, 16, D), lambda *_: (layer, kind, 0, 0))


def _row_spec(width=D):
    return pl.BlockSpec((TM, width), lambda i: (i, 0))


def _pair_spec():
    return pl.BlockSpec((NPAIR, TM, LANES), lambda i: (0, i, 0))


def _qkv_kernel(x_ref, sh_ref, sc_ref, gain_ref, w_ref, qg_ref, kg_ref, g1_ref,
                q_ref, k_ref, v_ref):
    b, pos = _tile_pos(TM, 0)
    is_ctx = pos < C
    h = _modulate(x_ref[...], gain_ref[...], _mod_rows(sh_ref, b, is_ctx),
                  _mod_rows(sc_ref, b, is_ctx)).astype(BF16)
    head_col = lax.shift_right_logical(lax.broadcasted_iota(jnp.int32, (TM, LANES), 1), 6)
    for idx, (o_ref, gn_ref) in enumerate(((q_ref, qg_ref), (k_ref, kg_ref))):
        acc = jnp.dot(h, w_ref[idx], preferred_element_type=F32)
        ss = jnp.dot((acc * acc).astype(BF16), g1_ref[...], preferred_element_type=F32)
        rstd = lax.rsqrt(ss * (1.0 / HD) + EPS)
        for p in range(NPAIR):
            ls = slice(p * LANES, (p + 1) * LANES)
            rb = jnp.take_along_axis(rstd, head_col + 2 * p, axis=1, mode="promise_in_bounds")
            o_ref[p] = (acc[:, ls] * rb * gn_ref[:, ls]).astype(BF16)
    v = jnp.dot(h, w_ref[2], preferred_element_type=F32).astype(BF16)
    first_head = lax.broadcasted_iota(jnp.int32, (1, LANES), 1) < HD
    one = jnp.ones((), BF16)
    for p in range(NPAIR):
        vp = v[:, p * LANES:(p + 1) * LANES]
        v_ref[0, p] = jnp.where(first_head, vp, one)
        v_ref[1, p] = jnp.where(first_head, one, vp)


def _qkv_call(xs, mods, layer, gain, w, qg, kg, g1):
    out = jax.ShapeDtypeStruct((NPAIR, NROWS, LANES), BF16)
    return pl.pallas_call(
        _qkv_kernel,
        grid=(NROWS // TM,),
        in_specs=[
            _row_spec(), _mods_spec(layer, 0), _mods_spec(layer, 1), _const_spec((1, D)),
            _const_spec((3, D, D)), _const_spec((1, D)), _const_spec((1, D)),
            _const_spec((D, LANES)),
        ],
        out_specs=[_pair_spec(), _pair_spec(),
                   pl.BlockSpec((2, NPAIR, TM, LANES), lambda i: (0, 0, i, 0))],
        out_shape=[out, out, jax.ShapeDtypeStruct((2, NPAIR, NROWS, LANES), BF16)],
        compiler_params=_cparams(1),
        name="na_qkv",
    )(xs, mods, mods, gain, w, qg, kg, g1)


def _dot_t(a, b):
    return lax.dot_general(a, b, (((1,), (1,)), ((), ())), preferred_element_type=F32)


def _attn_kernel(q_ref, kc_ref, vc_ref, k0_ref, k1_ref, k2_ref, v0_ref, v1_ref, v2_ref,
                 t2_ref, o_ref):
    jj = pl.program_id(1)
    rg = jj - 1
    win0 = jnp.clip(rg, 1, NQB - 3) - 1
    off = 4 * win0 - 4 * rg + (WIN_ROWS - 1)
    tile_idx = []
    for i in range(4):
        rs = jnp.clip(4 * rg + i - WIN_ROWS // 2, 0, IMG_ROWS - WIN_ROWS)
        row = []
        for pu in range(6):
            kl = 4 * win0 + 2 * pu
            lv = (kl >= rs) & (kl < rs + WIN_ROWS) & (jj > 0)
            rv = (kl + 1 >= rs) & (kl + 1 < rs + WIN_ROWS) & (jj > 0)
            d = jnp.clip(2 * pu - i + off, -1, 14) + 1
            row.append(jnp.where(lv & rv, d, jnp.where(
                rv, T2_LEFT_DEAD, jnp.where(lv, T2_RIGHT_DEAD, T2_DEAD))))
        tile_idx.append(row)
    lane = lax.broadcasted_iota(jnp.int32, (1, LANES), 1)
    k_refs = (k0_ref, k1_ref, k2_ref)
    v_refs = (v0_ref, v1_ref, v2_ref)

    def scores(head):
        p, hh = divmod(head, 2)
        qp = q_ref[p]
        sel = (lane < HD) if hh == 0 else (lane >= HD)
        qm = jnp.where(sel, qp, jnp.zeros_like(qp))
        s_all = [_dot_t(qm, kc_ref[p])]
        for j in range(3):
            bias = jnp.concatenate(
                [jnp.concatenate([t2_ref[head, tile_idx[i][2 * j + cp]] for cp in range(2)],
                                 axis=1) for i in range(4)], axis=0)
            s_all.append(_dot_t(qm, k_refs[j][p]) + bias)
        mm = jnp.maximum(jnp.maximum(s_all[0], s_all[1]), jnp.maximum(s_all[2], s_all[3]))
        m = jnp.max(jnp.maximum(mm[:, :LANES], mm[:, LANES:]), axis=-1, keepdims=True)
        return s_all, m

    def weighted_values(head, s_all, m):
        p, hh = divmod(head, 2)
        acc = jnp.dot(jnp.exp2(s_all[0] - m).astype(BF16), vc_ref[hh, p],
                      preferred_element_type=F32)
        for j in range(3):
            acc = acc + jnp.dot(jnp.exp2(s_all[j + 1] - m).astype(BF16), v_refs[j][hh, p],
                                preferred_element_type=F32)
        return acc / pltpu.roll(acc, HD, axis=1)

    skew = 1
    pending = [scores(h) for h in range(skew)]
    outs = []
    for head in range(H):
        if head + skew < H:
            pending.append(scores(head + skew))
        outs.append(weighted_values(head, *pending.pop(0)))
        if head % 2 == 1:
            o_ref[head // 2] = jnp.where(lane < HD, outs[0], outs[1]).astype(BF16)
            outs = []


def _attn_call(q, k, v, t2):
    def blk(fn):
        return pl.BlockSpec((NPAIR, QB, LANES), lambda b, jj: (0, fn(b, jj), 0))

    def vblk(fn):
        return pl.BlockSpec((2, NPAIR, QB, LANES), lambda b, jj: (0, 0, fn(b, jj), 0))

    def win(o):
        return lambda b, jj: b * NQB + jnp.clip(jj - 1, 1, NQB - 3) + o

    return pl.pallas_call(
        _attn_kernel,
        grid=(B, NQB),
        in_specs=[
            blk(lambda b, jj: b * NQB + jj),
            blk(lambda b, jj: b * NQB), vblk(lambda b, jj: b * NQB),
            blk(win(0)), blk(win(1)), blk(win(2)),
            vblk(win(0)), vblk(win(1)), vblk(win(2)),
            _const_spec((H, T2_ENTRIES, GRID_W, LANES)),
        ],
        out_specs=blk(lambda b, jj: b * NQB + jj),
        out_shape=jax.ShapeDtypeStruct((NPAIR, NROWS, LANES), BF16),
        compiler_params=_cparams(2),
        name="na_attention",
    )(q, k, v, k, k, k, v, v, v, t2)


def _proj_kernel(x_ref, y_ref, gate_ref, w_ref, o_ref):
    b, pos = _tile_pos(TM, 0)
    y = jnp.concatenate([y_ref[p] for p in range(NPAIR)], axis=1)
    o_ref[...] = x_ref[...] + _mod_rows(gate_ref, b, pos < C) * jnp.dot(
        y, w_ref[...], preferred_element_type=F32)


def _proj_call(xs, y, mods, layer, w):
    return pl.pallas_call(
        _proj_kernel,
        grid=(NROWS // TM,),
        in_specs=[_row_spec(), _pair_spec(), _mods_spec(layer, 2), _const_spec((D, D))],
        out_specs=_row_spec(),
        out_shape=jax.ShapeDtypeStruct((NROWS, D), F32),
        compiler_params=_cparams(1),
        name="na_out_proj",
    )(xs, y, mods, w)


def _bias_table(rpb):
    qc = jnp.arange(GRID_W)[:, None]
    kc = jnp.arange(GRID_W)[None, :]
    ws = jnp.clip(qc - WIN_COLS // 2, 0, GRID_W - WIN_COLS)
    col_ok = (kc >= ws) & (kc < ws + WIN_COLS)
    onehot = ((kc - qc + (WIN_COLS - 1))[None] == jnp.arange(2 * WIN_COLS - 1)[:, None, None])
    t = jnp.einsum("hdc,cqk->hdqk", rpb, onehot.astype(F32), precision=lax.Precision.HIGHEST)
    t = jnp.where(col_ok[None, None], t, NEG_INF)
    dead = jnp.full((H, 1, GRID_W, GRID_W), NEG_INF, F32)
    t_ext = jnp.concatenate([dead, t, dead], axis=1)
    pairs = jnp.concatenate([t_ext[:, :-1], t_ext[:, 1:]], axis=-1)
    lo, hi = WIN_ROWS // 2 - 1, WIN_ROWS // 2 - 1 + WIN_ROWS - 1
    left_dead = jnp.concatenate([dead, t[:, lo:lo + 1]], axis=-1)
    right_dead = jnp.concatenate([t[:, hi:hi + 1], dead], axis=-1)
    all_dead = jnp.concatenate([dead, dead], axis=-1)
    return jnp.concatenate([pairs, left_dead, right_dead, all_dead], axis=1)


assert 8 < C % TM < TM - 8
BND = C % TM
BND_LO = BND - 8


def _halo_specs():
    t8 = TM // 8
    return [
        pl.BlockSpec((8, D), lambda i: (jnp.maximum(i * t8 - 1, 0), 0)),
        _row_spec(),
        pl.BlockSpec((8, D), lambda i: (jnp.minimum((i + 1) * t8, NROWS // 8 - 1), 0)),
    ]


def _dead_halo_rows(reach_back, t=STREAM):
    start = t.base + (pl.program_id(0) % t.tpb) * t.tm
    e = lax.broadcasted_iota(jnp.int32, (t.tm + 16, 1), 0)
    at_start = (start == 0) | (start == C)
    at_end = (start + t.tm == C) | (start + t.tm == LT)
    return ((e >= 8 - reach_back) & (e < 8) & at_start) | ((e == t.tm + 8) & at_end)


def _slab_conv(u_scr, s, taps, cb, pos, t=STREAM):
    def seg(lo, n, masked):
        y = cb
        for off, w in taps:
            u = u_scr[s, pl.ds(lo + 8 + off, n), :]
            if masked and off != 0:
                p = pos[lo:lo + n]
                crosses = (p >= C) & (p < C - off) if off < 0 else (p < C) & (p >= C - off)
                u = jnp.where(crosses, 0.0, u)
            y = y + u * w
        return y
    if t is LATENT:
        return seg(0, t.tm, False)
    return jnp.concatenate([seg(0, BND_LO, False), seg(BND_LO, 16, True),
                            seg(BND_LO + 16, TM - BND_LO - 16, False)], axis=0)


def _lru_in_kernel(xp_ref, x_ref, xn_ref, sh_ref, sc_ref, gain_ref, w_ref, cw_ref, cb_ref,
                   gg_ref, xr_ref, rec_scr):
    b, pos = _tile_pos(TM, 0)
    _, pos_e = _tile_pos(TM + 16, -8)
    is_ctx_e = pos_e < C
    x_e = jnp.concatenate([xp_ref[...], x_ref[...], xn_ref[...]], axis=0)
    h = _modulate(x_e, gain_ref[...], _mod_rows(sh_ref, b, is_ctx_e),
                  _mod_rows(sc_ref, b, is_ctx_e))
    h = jnp.where(_dead_halo_rows(2), 0.0, h).astype(BF16)
    gate = jnp.dot(h, w_ref[0], preferred_element_type=F32)[8:8 + TM]
    gg_ref[...] = jax.nn.gelu(gate, approximate=True).astype(BF16)
    rec = jnp.dot(h, w_ref[1], preferred_element_type=F32)
    cw = cw_ref[...]
    cb = cb_ref[...]
    for s in range(D // LANES):
        ls = slice(s * LANES, (s + 1) * LANES)
        rec_scr[s] = rec[:, ls]
        taps = [(k - 2, cw[k:k + 1, ls]) for k in range(4)]
        xr_ref[:, ls] = _slab_conv(rec_scr, s, taps, cb[:, ls], pos)


def _lru_in_call(xs, mods, layer, gain, w, cw, cb):
    return pl.pallas_call(
        _lru_in_kernel,
        grid=(NROWS // TM,),
        in_specs=_halo_specs() + [
            _mods_spec(layer, 0), _mods_spec(layer, 1), _const_spec((1, D)),
            _const_spec((2, D, D)), _const_spec((4, D)), _const_spec((1, D)),
        ],
        out_specs=[_row_spec(), _row_spec()],
        out_shape=[jax.ShapeDtypeStruct((NROWS, D), BF16),
                   jax.ShapeDtypeStruct((NROWS, D), F32)],
        scratch_shapes=[pltpu.VMEM((D // LANES, TM + 16, LANES), F32)],
        compiler_params=_cparams(1),
        name="lru_in_proj",
    )(xs, xs, xs, mods, mods, gain, w, cw, cb)


def _chunk_index(g, reverse):
    if not reverse:
        return g
    return jnp.where(g < CCH, CCH - 1 - g, NCH + CCH - 1 - g)


def _scan_kernel(xr_ref, lam_ref, wg_ref, ba_ref, bx_ref, *rest, reverse):
    if reverse:
        o_ref, a_scr, b_scr, h_scr = rest
    else:
        hsb_ref, gg_ref, x_ref, wo_ref, g1_ref, o_ref, a_scr, b_scr, h_scr = rest
    g = pl.program_id(0)

    @pl.when(g == 0)
    def _():
        h_scr[...] = jnp.zeros_like(h_scr)

    lam = lam_ref[...]
    half_log_a = (-0.5 * LRU_C) * (jnp.maximum(-lam, 0.0) + jnp.log1p(jnp.exp(-jnp.abs(lam))))
    nslab = D // LANES
    group_slabs = GW // LANES

    def gates(gq):
        cols = slice(gq * GW, (gq + 1) * GW)
        xg = jnp.concatenate([xr_ref[b, :, cols] for b in range(B)], axis=0)
        pre = jnp.dot(xg.astype(BF16), wg_ref[gq], preferred_element_type=F32)
        t_r = jnp.tanh(pre[:, :GW] + ba_ref[:, cols])
        t_i = jnp.tanh(pre[:, GW:] + bx_ref[:, cols])
        log_a = half_log_a[:, cols] * t_r + half_log_a[:, cols]
        a = jnp.exp(log_a)
        y = (-1.0 - a * a) * jnp.tanh(log_a)
        root = jnp.where(y > 0.0, y * lax.rsqrt(y), 0.0)
        hx = 0.5 * xg
        bb = root * (hx * t_i + hx)
        for k2 in range(group_slabs):
            k = gq * group_slabs + k2
            for b in range(B):
                rows = slice(k * B * PITCH + b * PITCH, k * B * PITCH + b * PITCH + TS)
                a_scr[rows, :] = a[b * TS:(b + 1) * TS, k2 * LANES:(k2 + 1) * LANES]
                b_scr[rows, :] = bb[b * TS:(b + 1) * TS, k2 * LANES:(k2 + 1) * LANES]

    def recurrence(gq):
        slabs = range(gq * group_slabs, (gq + 1) * group_slabs)
        hs = {k: h_scr[k] for k in slabs}
        for t in (range(TS - 1, -1, -1) if reverse else range(TS)):
            for k in slabs:
                idx = pl.ds(k * B * PITCH + t, B, stride=PITCH)
                hs[k] = a_scr[idx, :] * hs[k] + b_scr[idx, :]
                b_scr[idx, :] = hs[k]
        for k in slabs:
            h_scr[k] = hs[k]

    gates(0)
    for gq in range(D // GW):
        if gq + 1 < D // GW:
            gates(gq + 1)
        recurrence(gq)

    def slab_rows(b, k):
        return slice(k * B * PITCH + b * PITCH, k * B * PITCH + b * PITCH + TS)

    if reverse:
        for b in range(B):
            for k in range(nslab):
                o_ref[b, :, k * LANES:(k + 1) * LANES] = b_scr[slab_rows(b, k), :].astype(BF16)
    else:
        z = jnp.concatenate([
            jnp.concatenate([
                ((b_scr[slab_rows(b, k), :] + hsb_ref[b, :, k * LANES:(k + 1) * LANES])
                 * gg_ref[b, :, k * LANES:(k + 1) * LANES].astype(F32)).astype(BF16)
                for k in range(nslab)], axis=1)
            for b in range(B)], axis=0)
        proj = jnp.dot(z, wo_ref[...], preferred_element_type=F32)
        is_ctx = g < CCH
        for b in range(B):
            g1 = jnp.where(is_ctx, g1_ref[8:9, :], g1_ref[b:b + 1, :])
            o_ref[b] = x_ref[b] + g1 * proj[b * TS:(b + 1) * TS]


def _scan_call(xr3, lam, wg, ba, bx, reverse, fwd_args=None):
    main = pl.BlockSpec((B, TS, D), lambda g: (0, _chunk_index(g, reverse), 0))
    in_specs = [
        main, _const_spec((1, D)),
        _const_spec((D // GW, GW, 2 * GW)), _const_spec((1, D)), _const_spec((1, D)),
    ]
    args = [xr3, lam, wg, 0.5 * ba, 0.5 * bx]
    if not reverse:
        hsb, gg, xs3, wo, mods, layer = fwd_args
        in_specs += [main, main, main, _const_spec((D, D)), _mods_spec(layer, 2)]
        args += [hsb, gg, xs3, wo, mods]
    slab = pltpu.VMEM((D // LANES * B * PITCH, LANES), F32)
    return pl.pallas_call(
        functools.partial(_scan_kernel, reverse=reverse),
        grid=(NCH,),
        in_specs=in_specs,
        out_specs=main,
        out_shape=jax.ShapeDtypeStruct((B, LT, D), BF16 if reverse else F32),
        scratch_shapes=[slab, slab, pltpu.VMEM((D // LANES, B, LANES), F32)],
        compiler_params=_cparams(1),
        name="lru_scan_bwd" if reverse else "lru_scan_fwd",
    )(*args)


def _gate_weights(wa, wx):
    def bd(w):
        w4 = w.reshape(D // GW, GW // LRU_BLOCK_W, LRU_BLOCK_W, LRU_BLOCK_W)
        eye = jnp.eye(GW // LRU_BLOCK_W, dtype=w.dtype)
        return (w4[:, :, :, None, :] * eye[None, :, None, :, None]).reshape(D // GW, GW, GW)
    return (0.5 * jnp.concatenate([bd(wa), bd(wx)], axis=-1)).astype(BF16)


FSLAB = FC // LANES


def _ffn_kernel(*refs, t):
    nchunk = 2 * D_FF // FC
    xp_ref, *x_refs = refs[:1 + t.main_blocks]
    (xn_ref, sh_ref, sc_ref, g2_ref, gain_ref, cw_ref, cb_ref,
     wd_ref) = refs[1 + t.main_blocks:9 + t.main_blocks]
    wu_ref = refs[9 + t.main_blocks:9 + t.main_blocks + nchunk]
    o_ref, uv_scr, ug_scr = refs[9 + t.main_blocks + nchunk:]
    tm = t.tm
    b, pos = _tile_pos(tm, 0, t)
    _, pos_e = _tile_pos(tm + 16, -8, t)
    is_ctx_e = pos_e < C
    x = jnp.concatenate([r[...] for r in x_refs], axis=0)
    x_e = jnp.concatenate([xp_ref[...], x, xn_ref[...]], axis=0)
    h = _modulate(x_e, gain_ref[...], _mod_rows(sh_ref, b, is_ctx_e),
                  _mod_rows(sc_ref, b, is_ctx_e))
    h = jnp.where(_dead_halo_rows(1, t), 0.0, h).astype(BF16)

    cw = cw_ref[...]
    cb = cb_ref[...]
    def up_project(ci):
        uv = jnp.dot(h, wu_ref[ci][...], preferred_element_type=F32)
        ug = jnp.dot(h, wu_ref[D_FF // FC + ci][...], preferred_element_type=F32)
        base = (ci % 2) * FSLAB
        for s in range(FSLAB):
            uv_scr[base + s] = uv[:, s * LANES:(s + 1) * LANES]
            ug_scr[base + s] = ug[:, s * LANES:(s + 1) * LANES]

    acc = jnp.zeros((tm, D), F32)
    up_project(0)
    for ci in range(D_FF // FC):
        if ci + 1 < D_FF // FC:
            up_project(ci + 1)
        base = (ci % 2) * FSLAB
        acts = []
        for s in range(FSLAB):
            lv = slice(ci * FC + s * LANES, ci * FC + (s + 1) * LANES)
            lg = slice(D_FF + lv.start, D_FF + lv.stop)
            val = _slab_conv(uv_scr, base + s, [(k - 1, cw[k:k + 1, lv]) for k in range(3)],
                             cb[:, lv], pos, t)
            hg = _slab_conv(ug_scr, base + s,
                            [(k - 1, 0.5 * cw[k:k + 1, lg]) for k in range(3)],
                            0.5 * cb[:, lg], pos, t)
            acts.append((val * (hg + hg * jnp.tanh(hg))).astype(BF16))
        acc = acc + jnp.dot(jnp.concatenate(acts, axis=1), wd_ref[ci],
                            preferred_element_type=F32)
    o_ref[...] = x + _mod_rows(g2_ref, b, pos < C) * acc


def _ffn_call(xs, mods, layer, gain, wu, cw, cb, wd, t):
    nchunk = 2 * D_FF // FC
    slab = pltpu.VMEM((2 * FSLAB, t.tm + 16, LANES), F32)
    if t is STREAM:
        x_specs, out_rows = _halo_specs(), NROWS
    else:
        def row0(i):
            return (i // t.tpb) * LT + t.base + (i % t.tpb) * t.tm
        half = t.tm // t.main_blocks
        x_specs = (
            [pl.BlockSpec((8, D), lambda i: (row0(i) // 8 - 1, 0))]
            + [pl.BlockSpec((half, D), lambda i, k=k: (row0(i) // half + k, 0))
               for k in range(t.main_blocks)]
            + [pl.BlockSpec((8, D), lambda i: (jnp.minimum((row0(i) + t.tm) // 8,
                                                            NROWS // 8 - 1), 0))])
        out_rows = B * S
    return pl.pallas_call(
        functools.partial(_ffn_kernel, t=t),
        grid=(out_rows // t.tm,),
        in_specs=x_specs + [
            _mods_spec(layer, 3), _mods_spec(layer, 4), _mods_spec(layer, 5),
            _const_spec((1, D)), _const_spec((3, 2 * D_FF)),
            _const_spec((1, 2 * D_FF)), _const_spec((D_FF // FC, FC, D)),
        ] + [pl.BlockSpec((D, FC), lambda i, c=c: (0, c), pipeline_mode=pl.Buffered(1))
             for c in range(nchunk)],
        out_specs=pl.BlockSpec((t.tm, D), lambda i: (i, 0)),
        out_shape=jax.ShapeDtypeStruct((out_rows, D), F32),
        scratch_shapes=[slab, slab],
        compiler_params=_cparams(1),
        name="conv_ffn",
    )(*([xs] * (2 + t.main_blocks)), mods, mods, mods, gain, cw, cb, wd, *([wu] * nchunk))


def _split_cols(w, n):
    return w.astype(BF16).reshape(D, n, D).transpose(1, 0, 2)


def kernel(x, c, ctx, c_ctx, ada_w, ada_b, norm_mix, norm_ffn, na_w_qkv, na_q_gain, na_k_gain,
           na_rpb, na_w_out, lru_w_in, lru_conv_w, lru_conv_b, lru_ga_w, lru_ga_b, lru_gx_w,
           lru_gx_b, lru_lambda, lru_w_out, ffn_w_up, ffn_conv_w, ffn_conv_b, ffn_w_down):
    xs = jnp.concatenate([ctx, x], axis=1).reshape(NROWS, D)
    cc = jnp.concatenate([c, c_ctx[None], jnp.zeros((7, D), F32)], axis=0)
    mods = _mods_call(cc, ada_w, ada_b)

    head_of_lane = jnp.arange(D) // HD
    g1 = (head_of_lane[:, None] == jnp.arange(LANES)[None, :]).astype(BF16)

    for i in range(DEPTH):
        j = i // 2
        gain_mix = norm_mix[i][None]
        if i % 2 == 0:
            qg = (jnp.tile(na_q_gain[j], H) * (HD ** -0.5 * LOG2_E))[None]
            kg = jnp.tile(na_k_gain[j], H)[None]
            q, k, v = _qkv_call(xs, mods, i, gain_mix, _split_cols(na_w_qkv[j], 3), qg, kg, g1)
            o = _attn_call(q, k, v, _bias_table(na_rpb[j] * LOG2_E))
            xs = _proj_call(xs, o, mods, i, na_w_out[j].astype(BF16))
        else:
            gg, xr = _lru_in_call(xs, mods, i, gain_mix, _split_cols(lru_w_in[j], 2),
                                  lru_conv_w[j], lru_conv_b[j][None])
            xr3 = xr.reshape(B, LT, D)
            hsb = _scan_call(xr3, lru_lambda[j, 1][None],
                             _gate_weights(lru_ga_w[j, 1], lru_gx_w[j, 1]),
                             lru_ga_b[j, 1][None], lru_gx_b[j, 1][None], True)
            xs = _scan_call(xr3, lru_lambda[j, 0][None],
                            _gate_weights(lru_ga_w[j, 0], lru_gx_w[j, 0]),
                            lru_ga_b[j, 0][None], lru_gx_b[j, 0][None], False,
                            (hsb, gg.reshape(B, LT, D), xs.reshape(B, LT, D),
                             lru_w_out[j].astype(BF16), mods, i)).reshape(NROWS, D)
        xs = _ffn_call(xs, mods, i, norm_ffn[i][None], ffn_w_up[i].astype(BF16),
                       ffn_conv_w[i], ffn_conv_b[i][None],
                       ffn_w_down[i].astype(BF16).reshape(D_FF // FC, FC, D),
                       LATENT if i == DEPTH - 1 else STREAM)
    return xs.reshape(B, S, D)
```

```python
import functools
from typing import NamedTuple

import jax
import jax.numpy as jnp
import numpy as np
from jax import lax
from jax.experimental import pallas as pl
from jax.experimental.pallas import tpu as pltpu

F32 = jnp.float32
BF16 = jnp.bfloat16

D = 1024
B = 8
S = 4096
DEPTH = 4
C = 256
LT = C + S
NROWS = B * LT
GRID_W = 64
IMG_ROWS = S // GRID_W
H = 16
HD = 64
NPAIR = H // 2
LANES = 128
WIN_ROWS = 8
WIN_COLS = 16
D_FF = 3 * D
LRU_BLOCK_W = 64
LRU_C = 8.0
EPS = 1e-6
NEG_INF = float("-inf")
LOG2_E = 1.4426950408889634

TM = 544
TPB = LT // TM
FC = 512
QB = 256
NQB = LT // QB
TS = 64
NCH = LT // TS
CCH = C // TS
PITCH = TS + 4
GW = 256
VMEM_LIMIT = 56 * 1024 * 1024

T2_LEFT_DEAD = 16
T2_RIGHT_DEAD = 17
T2_DEAD = 18
T2_ENTRIES = 19


def _cparams(n_axes):
    return pltpu.CompilerParams(
        dimension_semantics=("arbitrary",) * n_axes, vmem_limit_bytes=VMEM_LIMIT)


def _const_spec(shape):
    nd = len(shape)
    return pl.BlockSpec(shape, lambda *_: (0,) * nd, pipeline_mode=pl.Buffered(1))


def _mods_kernel(cc_ref, w_ref, b_ref, o_ref):
    c = cc_ref[...]
    s = c * jax.nn.sigmoid(c)
    o_ref[...] = jnp.dot(s.astype(BF16), w_ref[...].astype(BF16),
                         preferred_element_type=F32) + b_ref[...]


def _mods_call(cc, ada_w, ada_b):
    return pl.pallas_call(
        _mods_kernel,
        grid=(DEPTH, 6),
        in_specs=[
            pl.BlockSpec((16, D), lambda l, k: (0, 0)),
            pl.BlockSpec((None, D, D), lambda l, k: (l, 0, k)),
            pl.BlockSpec((None, None, 1, D), lambda l, k: (l, k, 0, 0)),
        ],
        out_specs=pl.BlockSpec((None, None, 16, D), lambda l, k: (l, k, 0, 0)),
        out_shape=jax.ShapeDtypeStruct((DEPTH, 6, 16, D), F32),
        compiler_params=_cparams(2),
        name="adaln_mods",
    )(cc, ada_w, ada_b.reshape(DEPTH, 6, 1, D))


class Tiling(NamedTuple):
    tm: int
    tpb: int
    base: int
    main_blocks: int


STREAM = Tiling(TM, TPB, 0, 1)
LATENT = Tiling(512, S // 512, C, 2)


def _tile_pos(n, lo, t=STREAM):
    i = pl.program_id(0)
    b = i // t.tpb
    j = i % t.tpb
    pos = t.base + j * t.tm + lo + lax.broadcasted_iota(jnp.int32, (n, 1), 0)
    return b, pos


def _mod_rows(m_ref, b, is_ctx):
    return jnp.where(is_ctx, m_ref[8:9, :], m_ref[pl.ds(b, 1), :])


def _modulate(x, gain, shift, scale):
    y = x * lax.rsqrt(jnp.mean(x * x, axis=-1, keepdims=True) + EPS)
    return (y * gain) * (1.0 + scale) + shift


def _mods_spec(layer, kind):
    return pl.BlockSpec((None, None, 16, D), lambda *_: (layer, kind, 0, 0))


def _row_spec(width=D):
    return pl.BlockSpec((TM, width), lambda i: (i, 0))


def _pair_spec():
    return pl.BlockSpec((NPAIR, TM, LANES), lambda i: (0, i, 0))


def _qkv_kernel(x_ref, sh_ref, sc_ref, gain_ref, w_ref, qg_ref, kg_ref, g1_ref,
                q_ref, k_ref, v_ref):
    b, pos = _tile_pos(TM, 0)
    is_ctx = pos < C
    h = _modulate(x_ref[...], gain_ref[...], _mod_rows(sh_ref, b, is_ctx),
                  _mod_rows(sc_ref, b, is_ctx)).astype(BF16)
    head_col = lax.shift_right_logical(lax.broadcasted_iota(jnp.int32, (TM, LANES), 1), 6)
    def head_norm(acc, o_ref, gn_ref):
        ss = jnp.dot((acc * acc).astype(BF16), g1_ref[...], preferred_element_type=F32)
        rstd = lax.rsqrt(ss * (1.0 / HD) + EPS)
        for p in range(NPAIR):
            ls = slice(p * LANES, (p + 1) * LANES)
            rb = jnp.take_along_axis(rstd, head_col + 2 * p, axis=1, mode="promise_in_bounds")
            o_ref[p] = (acc[:, ls] * rb * gn_ref[:, ls]).astype(BF16)

    head_norm(jnp.dot(h, w_ref[0], preferred_element_type=F32), q_ref, qg_ref)
    head_norm(jnp.dot(h, w_ref[1], preferred_element_type=F32), k_ref, kg_ref)
    v = jnp.dot(h, w_ref[2], preferred_element_type=F32).astype(BF16)
    first_head = lax.broadcasted_iota(jnp.int32, (1, LANES), 1) < HD
    one = jnp.ones((), BF16)
    for p in range(NPAIR):
        vp = v[:, p * LANES:(p + 1) * LANES]
        v_ref[0, p] = jnp.where(first_head, vp, one)
        v_ref[1, p] = jnp.where(first_head, one, vp)


def _qkv_call(xs, mods, layer, gain, w, qg, kg, g1):
    out = jax.ShapeDtypeStruct((NPAIR, NROWS, LANES), BF16)
    return pl.pallas_call(
        _qkv_kernel,
        grid=(NROWS // TM,),
        in_specs=[
            _row_spec(), _mods_spec(layer, 0), _mods_spec(layer, 1), _const_spec((1, D)),
            _const_spec((3, D, D)), _const_spec((1, D)), _const_spec((1, D)),
            _const_spec((D, LANES)),
        ],
        out_specs=[_pair_spec(), _pair_spec(),
                   pl.BlockSpec((2, NPAIR, TM, LANES), lambda i: (0, 0, i, 0))],
        out_shape=[out, out, jax.ShapeDtypeStruct((2, NPAIR, NROWS, LANES), BF16)],
        compiler_params=_cparams(1),
        name="na_qkv",
    )(xs, mods, mods, gain, w, qg, kg, g1)


def _dot_t(a, b):
    return lax.dot_general(a, b, (((1,), (1,)), ((), ())), preferred_element_type=F32)


def _bias_tile_table():
    tab = np.zeros((NQB, 24), np.int32)
    for jj in range(NQB):
        rg = jj - 1
        win0 = min(max(rg, 1), NQB - 3) - 1
        off = 4 * win0 - 4 * rg + (WIN_ROWS - 1)
        for i in range(4):
            rs = min(max(4 * rg + i - WIN_ROWS // 2, 0), IMG_ROWS - WIN_ROWS)
            for pu in range(6):
                kl = 4 * win0 + 2 * pu
                lv = jj > 0 and rs <= kl < rs + WIN_ROWS
                rv = jj > 0 and rs <= kl + 1 < rs + WIN_ROWS
                d = 2 * pu - i + off
                if lv and rv:
                    assert 0 <= d <= 2 * WIN_ROWS - 3
                    tab[jj, i * 6 + pu] = d + 1
                elif rv:
                    assert d + 1 == WIN_ROWS // 2 - 1
                    tab[jj, i * 6 + pu] = T2_LEFT_DEAD
                elif lv:
                    assert d == WIN_ROWS // 2 - 1 + WIN_ROWS - 1
                    tab[jj, i * 6 + pu] = T2_RIGHT_DEAD
                else:
                    tab[jj, i * 6 + pu] = T2_DEAD
    return tab


def _attn_kernel(idx_ref, q_ref, kc_ref, vc_ref, k0_ref, k1_ref, k2_ref, v0_ref, v1_ref, v2_ref,
                 t2_ref, o_ref):
    jj = pl.program_id(1)
    tile_idx = [[idx_ref[jj, i * 6 + pu] for pu in range(6)] for i in range(4)]
    lane = lax.broadcasted_iota(jnp.int32, (1, LANES), 1)
    k_refs = (k0_ref, k1_ref, k2_ref)
    v_refs = (v0_ref, v1_ref, v2_ref)

    def scores(head):
        p, hh = divmod(head, 2)
        qp = q_ref[p]
        sel = (lane < HD) if hh == 0 else (lane >= HD)
        qm = jnp.where(sel, qp, jnp.zeros_like(qp))
        s_all = [_dot_t(qm, kc_ref[p])]
        for j in range(3):
            bias = jnp.concatenate(
                [jnp.concatenate([t2_ref[head, tile_idx[i][2 * j + cp]] for cp in range(2)],
                                 axis=1) for i in range(4)], axis=0)
            s_all.append(_dot_t(qm, k_refs[j][p]) + bias)
        mm = jnp.maximum(jnp.maximum(s_all[0], s_all[1]), jnp.maximum(s_all[2], s_all[3]))
        m = jnp.max(jnp.maximum(mm[:, :LANES], mm[:, LANES:]), axis=-1, keepdims=True)
        return s_all, m

    def weighted_values(head, s_all, m):
        p, hh = divmod(head, 2)
        acc = jnp.dot(jnp.exp2(s_all[0] - m).astype(BF16), vc_ref[hh, p],
                      preferred_element_type=F32)
        for j in range(3):
            acc = acc + jnp.dot(jnp.exp2(s_all[j + 1] - m).astype(BF16), v_refs[j][hh, p],
                                preferred_element_type=F32)
        return acc / pltpu.roll(acc, HD, axis=1)

    pending = scores(0)
    outs = []
    for head in range(H):
        nxt = scores(head + 1) if head + 1 < H else None
        outs.append(weighted_values(head, *pending))
        pending = nxt
        if head % 2 == 1:
            o_ref[head // 2] = jnp.where(lane < HD, outs[0], outs[1]).astype(BF16)
            outs = []


def _attn_call(q, k, v, t2):
    def blk(fn):
        return pl.BlockSpec((NPAIR, QB, LANES), lambda b, jj, idx: (0, fn(b, jj), 0))

    def vblk(fn):
        return pl.BlockSpec((2, NPAIR, QB, LANES), lambda b, jj, idx: (0, 0, fn(b, jj), 0))

    def win(o):
        return lambda b, jj: b * NQB + jnp.clip(jj - 1, 1, NQB - 3) + o

    return pl.pallas_call(
        _attn_kernel,
        grid_spec=pltpu.PrefetchScalarGridSpec(
            num_scalar_prefetch=1,
            grid=(B, NQB),
            in_specs=[
                blk(lambda b, jj: b * NQB + jj),
                blk(lambda b, jj: b * NQB), vblk(lambda b, jj: b * NQB),
                blk(win(0)), blk(win(1)), blk(win(2)),
                vblk(win(0)), vblk(win(1)), vblk(win(2)),
                _const_spec((H, T2_ENTRIES, GRID_W, LANES)),
            ],
            out_specs=blk(lambda b, jj: b * NQB + jj),
        ),
        out_shape=jax.ShapeDtypeStruct((NPAIR, NROWS, LANES), BF16),
        compiler_params=_cparams(2),
        name="na_attention",
    )(jnp.asarray(_bias_tile_table()), q, k, v, k, k, k, v, v, v, t2)


def _proj_kernel(x_ref, y_ref, gate_ref, w_ref, o_ref):
    b, pos = _tile_pos(TM, 0)
    y = jnp.concatenate([y_ref[p] for p in range(NPAIR)], axis=1)
    o_ref[...] = x_ref[...] + _mod_rows(gate_ref, b, pos < C) * jnp.dot(
        y, w_ref[...], preferred_element_type=F32)


def _proj_call(xs, y, mods, layer, w):
    return pl.pallas_call(
        _proj_kernel,
        grid=(NROWS // TM,),
        in_specs=[_row_spec(), _pair_spec(), _mods_spec(layer, 2), _const_spec((D, D))],
        out_specs=_row_spec(),
        out_shape=jax.ShapeDtypeStruct((NROWS, D), F32),
        compiler_params=_cparams(1),
        name="na_out_proj",
    )(xs, y, mods, w)


def _bias_table(rpb):
    qc = jnp.arange(GRID_W)[:, None]
    kc = jnp.arange(GRID_W)[None, :]
    ws = jnp.clip(qc - WIN_COLS // 2, 0, GRID_W - WIN_COLS)
    col_ok = (kc >= ws) & (kc < ws + WIN_COLS)
    onehot = ((kc - qc + (WIN_COLS - 1))[None] == jnp.arange(2 * WIN_COLS - 1)[:, None, None])
    t = jnp.einsum("hdc,cqk->hdqk", rpb, onehot.astype(F32), precision=lax.Precision.HIGHEST)
    t = jnp.where(col_ok[None, None], t, NEG_INF)
    dead = jnp.full((H, 1, GRID_W, GRID_W), NEG_INF, F32)
    t_ext = jnp.concatenate([dead, t, dead], axis=1)
    pairs = jnp.concatenate([t_ext[:, :-1], t_ext[:, 1:]], axis=-1)
    lo, hi = WIN_ROWS // 2 - 1, WIN_ROWS // 2 - 1 + WIN_ROWS - 1
    left_dead = jnp.concatenate([dead, t[:, lo:lo + 1]], axis=-1)
    right_dead = jnp.concatenate([t[:, hi:hi + 1], dead], axis=-1)
    all_dead = jnp.concatenate([dead, dead], axis=-1)
    return jnp.concatenate([pairs, left_dead, right_dead, all_dead], axis=1)


assert 8 < C % TM < TM - 8
BND = C % TM
BND_LO = BND - 8


def _halo_specs():
    t8 = TM // 8
    return [
        pl.BlockSpec((8, D), lambda i: (jnp.maximum(i * t8 - 1, 0), 0)),
        _row_spec(),
        pl.BlockSpec((8, D), lambda i: (jnp.minimum((i + 1) * t8, NROWS // 8 - 1), 0)),
    ]


def _dead_halo_rows(reach_back, t=STREAM):
    start = t.base + (pl.program_id(0) % t.tpb) * t.tm
    e = lax.broadcasted_iota(jnp.int32, (t.tm + 16, 1), 0)
    at_start = (start == 0) | (start == C)
    at_end = (start + t.tm == C) | (start + t.tm == LT)
    return ((e >= 8 - reach_back) & (e < 8) & at_start) | ((e == t.tm + 8) & at_end)


def _slab_conv(u_scr, s, taps, cb, pos, t=STREAM):
    def seg(lo, n, masked):
        y = cb
        for off, w in taps:
            u = u_scr[s, pl.ds(lo + 8 + off, n), :]
            if masked and off != 0:
                p = pos[lo:lo + n]
                crosses = (p >= C) & (p < C - off) if off < 0 else (p < C) & (p >= C - off)
                u = jnp.where(crosses, 0.0, u)
            y = y + u * w
        return y
    if t is LATENT:
        return seg(0, t.tm, False)
    return jnp.concatenate([seg(0, BND_LO, False), seg(BND_LO, 16, True),
                            seg(BND_LO + 16, TM - BND_LO - 16, False)], axis=0)


def _lru_in_kernel(xp_ref, x_ref, xn_ref, sh_ref, sc_ref, gain_ref, w_ref, cw_ref, cb_ref,
                   gg_ref, xr_ref, rec_scr):
    b, pos = _tile_pos(TM, 0)
    _, pos_e = _tile_pos(TM + 16, -8)
    is_ctx_e = pos_e < C
    x_e = jnp.concatenate([xp_ref[...], x_ref[...], xn_ref[...]], axis=0)
    h = _modulate(x_e, gain_ref[...], _mod_rows(sh_ref, b, is_ctx_e),
                  _mod_rows(sc_ref, b, is_ctx_e))
    h = jnp.where(_dead_halo_rows(2), 0.0, h).astype(BF16)
    cw = cw_ref[...]
    cb = cb_ref[...]
    ngroup = D // GW

    def project(c):
        return (jnp.dot(h, w_ref[c], preferred_element_type=F32),
                jnp.dot(h, w_ref[ngroup + c], preferred_element_type=F32))

    pending = project(0)
    for c in range(ngroup):
        nxt = project(c + 1) if c + 1 < ngroup else None
        gate, rec = pending
        gg_ref[:, c * GW:(c + 1) * GW] = jax.nn.gelu(
            gate[8:8 + TM], approximate=True).astype(BF16)
        for s2 in range(GW // LANES):
            s = c * (GW // LANES) + s2
            ls = slice(s * LANES, (s + 1) * LANES)
            rec_scr[s] = rec[:, s2 * LANES:(s2 + 1) * LANES]
            taps = [(k - 2, cw[k:k + 1, ls]) for k in range(4)]
            xr_ref[:, ls] = _slab_conv(rec_scr, s, taps, cb[:, ls], pos)
        pending = nxt


def _lru_in_call(xs, mods, layer, gain, w, cw, cb):
    return pl.pallas_call(
        _lru_in_kernel,
        grid=(NROWS // TM,),
        in_specs=_halo_specs() + [
            _mods_spec(layer, 0), _mods_spec(layer, 1), _const_spec((1, D)),
            _const_spec((2 * D // GW, D, GW)), _const_spec((4, D)), _const_spec((1, D)),
        ],
        out_specs=[_row_spec(), _row_spec()],
        out_shape=[jax.ShapeDtypeStruct((NROWS, D), BF16),
                   jax.ShapeDtypeStruct((NROWS, D), F32)],
        scratch_shapes=[pltpu.VMEM((D // LANES, TM + 16, LANES), F32)],
        compiler_params=_cparams(1),
        name="lru_in_proj",
    )(xs, xs, xs, mods, mods, gain, w, cw, cb)


def _chunk_index(g, reverse):
    if not reverse:
        return g
    return jnp.where(g < CCH, CCH - 1 - g, NCH + CCH - 1 - g)


def _scan_kernel(xr_ref, lam_ref, wg_ref, ba_ref, bx_ref, *rest, reverse):
    if reverse:
        o_ref, a_scr, b_scr, h_scr = rest
    else:
        hsb_ref, gg_ref, x_ref, wo_ref, g1_ref, o_ref, a_scr, b_scr, h_scr = rest
    g = pl.program_id(0)

    @pl.when(g == 0)
    def _():
        h_scr[...] = jnp.zeros_like(h_scr)

    lam = lam_ref[...]
    half_log_a = (-0.5 * LRU_C) * (jnp.maximum(-lam, 0.0) + jnp.log1p(jnp.exp(-jnp.abs(lam))))
    nslab = D // LANES
    group_slabs = GW // LANES

    def gates(gq):
        cols = slice(gq * GW, (gq + 1) * GW)
        xg = jnp.concatenate([xr_ref[b, :, cols] for b in range(B)], axis=0)
        pre = jnp.dot(xg.astype(BF16), wg_ref[gq], preferred_element_type=F32)
        t_r = jnp.tanh(pre[:, :GW] + ba_ref[:, cols])
        t_i = jnp.tanh(pre[:, GW:] + bx_ref[:, cols])
        log_a = half_log_a[:, cols] * t_r + half_log_a[:, cols]
        a = jnp.exp(log_a)
        y = (-1.0 - a * a) * jnp.tanh(log_a)
        root = jnp.where(y > 0.0, y * lax.rsqrt(y), 0.0)
        hx = 0.5 * xg
        bb = root * (hx * t_i + hx)
        for k2 in range(group_slabs):
            k = gq * group_slabs + k2
            for b in range(B):
                rows = slice(k * B * PITCH + b * PITCH, k * B * PITCH + b * PITCH + TS)
                a_scr[rows, :] = a[b * TS:(b + 1) * TS, k2 * LANES:(k2 + 1) * LANES]
                b_scr[rows, :] = bb[b * TS:(b + 1) * TS, k2 * LANES:(k2 + 1) * LANES]

    def recurrence(gq):
        slabs = range(gq * group_slabs, (gq + 1) * group_slabs)
        hs = {k: h_scr[k] for k in slabs}
        for t in (range(TS - 1, -1, -1) if reverse else range(TS)):
            for k in slabs:
                idx = pl.ds(k * B * PITCH + t, B, stride=PITCH)
                hs[k] = a_scr[idx, :] * hs[k] + b_scr[idx, :]
                b_scr[idx, :] = hs[k]
        for k in slabs:
            h_scr[k] = hs[k]

    def slab_rows(b, k):
        return slice(k * B * PITCH + b * PITCH, k * B * PITCH + b * PITCH + TS)

    gates(0)
    for gq in range(D // GW):
        if gq + 1 < D // GW:
            gates(gq + 1)
        recurrence(gq)

    if reverse:
        for b in range(B):
            for k in range(nslab):
                o_ref[b, :, k * LANES:(k + 1) * LANES] = b_scr[slab_rows(b, k), :].astype(BF16)
    else:
        z = jnp.concatenate([
            jnp.concatenate([
                ((b_scr[slab_rows(b, k), :] + hsb_ref[b, :, k * LANES:(k + 1) * LANES])
                 * gg_ref[b, :, k * LANES:(k + 1) * LANES].astype(F32)).astype(BF16)
                for k in range(nslab)], axis=1)
            for b in range(B)], axis=0)
        proj = jnp.dot(z, wo_ref[...], preferred_element_type=F32)
        is_ctx = g < CCH
        for b in range(B):
            g1 = jnp.where(is_ctx, g1_ref[8:9, :], g1_ref[b:b + 1, :])
            o_ref[b] = x_ref[b] + g1 * proj[b * TS:(b + 1) * TS]


def _scan_call(xr3, lam, wg, ba, bx, reverse, fwd_args=None):
    main = pl.BlockSpec((B, TS, D), lambda g: (0, _chunk_index(g, reverse), 0))
    in_specs = [
        main, _const_spec((1, D)),
        _const_spec((D // GW, GW, 2 * GW)), _const_spec((1, D)), _const_spec((1, D)),
    ]
    args = [xr3, lam, wg, 0.5 * ba, 0.5 * bx]
    if not reverse:
        hsb, gg, xs3, wo, mods, layer = fwd_args
        in_specs += [main, main, main, _const_spec((D, D)), _mods_spec(layer, 2)]
        args += [hsb, gg, xs3, wo, mods]
    slab = pltpu.VMEM((D // LANES * B * PITCH, LANES), F32)
    return pl.pallas_call(
        functools.partial(_scan_kernel, reverse=reverse),
        grid=(NCH,),
        in_specs=in_specs,
        out_specs=main,
        out_shape=jax.ShapeDtypeStruct((B, LT, D), BF16 if reverse else F32),
        scratch_shapes=[slab, slab, pltpu.VMEM((D // LANES, B, LANES), F32)],
        compiler_params=_cparams(1),
        name="lru_scan_bwd" if reverse else "lru_scan_fwd",
    )(*args)


def _gate_weights(wa, wx):
    def bd(w):
        w4 = w.reshape(D // GW, GW // LRU_BLOCK_W, LRU_BLOCK_W, LRU_BLOCK_W)
        eye = jnp.eye(GW // LRU_BLOCK_W, dtype=w.dtype)
        return (w4[:, :, :, None, :] * eye[None, :, None, :, None]).reshape(D // GW, GW, GW)
    return (0.5 * jnp.concatenate([bd(wa), bd(wx)], axis=-1)).astype(BF16)


FSLAB = FC // LANES


def _ffn_kernel(*refs, t):
    nchunk = 2 * D_FF // FC
    xp_ref, *x_refs = refs[:1 + t.main_blocks]
    (xn_ref, sh_ref, sc_ref, g2_ref, gain_ref, cw_ref, cb_ref,
     wd_ref) = refs[1 + t.main_blocks:9 + t.main_blocks]
    wu_ref = refs[9 + t.main_blocks:9 + t.main_blocks + nchunk]
    o_ref, uv_scr, ug_scr = refs[9 + t.main_blocks + nchunk:]
    tm = t.tm
    b, pos = _tile_pos(tm, 0, t)
    _, pos_e = _tile_pos(tm + 16, -8, t)
    is_ctx_e = pos_e < C
    x = jnp.concatenate([r[...] for r in x_refs], axis=0)
    x_e = jnp.concatenate([xp_ref[...], x, xn_ref[...]], axis=0)
    h = _modulate(x_e, gain_ref[...], _mod_rows(sh_ref, b, is_ctx_e),
                  _mod_rows(sc_ref, b, is_ctx_e))
    h = jnp.where(_dead_halo_rows(1, t), 0.0, h).astype(BF16)

    cw = cw_ref[...]
    cb = cb_ref[...]
    def up_project(ci):
        uv = jnp.dot(h, wu_ref[ci][...], preferred_element_type=F32)
        ug = jnp.dot(h, wu_ref[D_FF // FC + ci][...], preferred_element_type=F32)
        base = (ci % 2) * FSLAB
        for s in range(FSLAB):
            uv_scr[base + s] = uv[:, s * LANES:(s + 1) * LANES]
            ug_scr[base + s] = ug[:, s * LANES:(s + 1) * LANES]

    acc = jnp.zeros((tm, D), F32)
    up_project(0)
    for ci in range(D_FF // FC):
        if ci + 1 < D_FF // FC:
            up_project(ci + 1)
        base = (ci % 2) * FSLAB
        acts = []
        for s in range(FSLAB):
            lv = slice(ci * FC + s * LANES, ci * FC + (s + 1) * LANES)
            lg = slice(D_FF + lv.start, D_FF + lv.stop)
            val = _slab_conv(uv_scr, base + s, [(k - 1, cw[k:k + 1, lv]) for k in range(3)],
                             cb[:, lv], pos, t)
            hg = _slab_conv(ug_scr, base + s,
                            [(k - 1, 0.5 * cw[k:k + 1, lg]) for k in range(3)],
                            0.5 * cb[:, lg], pos, t)
            acts.append((val * (hg + hg * jnp.tanh(hg))).astype(BF16))
        acc = acc + jnp.dot(jnp.concatenate(acts, axis=1), wd_ref[ci],
                            preferred_element_type=F32)
    o_ref[...] = x + _mod_rows(g2_ref, b, pos < C) * acc


def _ffn_call(xs, mods, layer, gain, wu, cw, cb, wd, t):
    nchunk = 2 * D_FF // FC

    def layer_spec(shape):
        return pl.BlockSpec((None,) + shape, lambda i: (layer,) + (0,) * len(shape),
                            pipeline_mode=pl.Buffered(1))

    slab = pltpu.VMEM((2 * FSLAB, t.tm + 16, LANES), F32)
    if t is STREAM:
        x_specs, out_rows = _halo_specs(), NROWS
    else:
        def row0(i):
            return (i // t.tpb) * LT + t.base + (i % t.tpb) * t.tm
        half = t.tm // t.main_blocks
        x_specs = (
            [pl.BlockSpec((8, D), lambda i: (row0(i) // 8 - 1, 0))]
            + [pl.BlockSpec((half, D), lambda i, k=k: (row0(i) // half + k, 0))
               for k in range(t.main_blocks)]
            + [pl.BlockSpec((8, D), lambda i: (jnp.minimum((row0(i) + t.tm) // 8,
                                                            NROWS // 8 - 1), 0))])
        out_rows = B * S
    return pl.pallas_call(
        functools.partial(_ffn_kernel, t=t),
        grid=(out_rows // t.tm,),
        in_specs=x_specs + [
            _mods_spec(layer, 3), _mods_spec(layer, 4), _mods_spec(layer, 5),
            _const_spec((1, D)), layer_spec((3, 2 * D_FF)),
            layer_spec((1, 2 * D_FF)), layer_spec((D_FF // FC, FC, D)),
        ] + [pl.BlockSpec((None, D, FC), lambda i, c=c: (layer, 0, c),
                          pipeline_mode=pl.Buffered(1)) for c in range(nchunk)],
        out_specs=pl.BlockSpec((t.tm, D), lambda i: (i, 0)),
        out_shape=jax.ShapeDtypeStruct((out_rows, D), F32),
        scratch_shapes=[slab, slab],
        compiler_params=_cparams(1),
        name="conv_ffn",
    )(*([xs] * (2 + t.main_blocks)), mods, mods, mods, gain, cw, cb, wd, *([wu] * nchunk))


def _split_cols(w, n):
    return w.astype(BF16).reshape(D, n, w.shape[1] // n).transpose(1, 0, 2)


def kernel(x, c, ctx, c_ctx, ada_w, ada_b, norm_mix, norm_ffn, na_w_qkv, na_q_gain, na_k_gain,
           na_rpb, na_w_out, lru_w_in, lru_conv_w, lru_conv_b, lru_ga_w, lru_ga_b, lru_gx_w,
           lru_gx_b, lru_lambda, lru_w_out, ffn_w_up, ffn_conv_w, ffn_conv_b, ffn_w_down):
    xs = jnp.concatenate([ctx, x], axis=1).reshape(NROWS, D)
    cc = jnp.concatenate([c, c_ctx[None], jnp.zeros((7, D), F32)], axis=0)
    mods = _mods_call(cc, ada_w, ada_b)

    head_of_lane = jnp.arange(D) // HD
    g1 = (head_of_lane[:, None] == jnp.arange(LANES)[None, :]).astype(BF16)

    w_up = ffn_w_up.astype(BF16)
    w_down = ffn_w_down.astype(BF16).reshape(DEPTH, D_FF // FC, FC, D)
    conv_b = ffn_conv_b[:, None]

    for i in range(DEPTH):
        j = i // 2
        gain_mix = norm_mix[i][None]
        if i % 2 == 0:
            qg = (jnp.tile(na_q_gain[j], H) * (HD ** -0.5 * LOG2_E))[None]
            kg = jnp.tile(na_k_gain[j], H)[None]
            q, k, v = _qkv_call(xs, mods, i, gain_mix, _split_cols(na_w_qkv[j], 3), qg, kg, g1)
            o = _attn_call(q, k, v, _bias_table(na_rpb[j] * LOG2_E))
            xs = _proj_call(xs, o, mods, i, na_w_out[j].astype(BF16))
        else:
            gg, xr = _lru_in_call(xs, mods, i, gain_mix, _split_cols(lru_w_in[j], 2 * D // GW),
                                  lru_conv_w[j], lru_conv_b[j][None])
            xr3 = xr.reshape(B, LT, D)
            hsb = _scan_call(xr3, lru_lambda[j, 1][None],
                             _gate_weights(lru_ga_w[j, 1], lru_gx_w[j, 1]),
                             lru_ga_b[j, 1][None], lru_gx_b[j, 1][None], True)
            xs = _scan_call(xr3, lru_lambda[j, 0][None],
                            _gate_weights(lru_ga_w[j, 0], lru_gx_w[j, 0]),
                            lru_ga_b[j, 0][None], lru_gx_b[j, 0][None], False,
                            (hsb, gg.reshape(B, LT, D), xs.reshape(B, LT, D),
                             lru_w_out[j].astype(BF16), mods, i)).reshape(NROWS, D)
        xs = _ffn_call(xs, mods, i, norm_ffn[i][None], w_up, ffn_conv_w, conv_b, w_down,
                       LATENT if i == DEPTH - 1 else STREAM)
    return xs.reshape(B, S, D)
```

```python
import functools
from typing import NamedTuple

import jax
import jax.numpy as jnp
import numpy as np
from jax import lax
from jax.experimental import pallas as pl
from jax.experimental.pallas import tpu as pltpu

F32 = jnp.float32
BF16 = jnp.bfloat16

D = 1024
B = 8
S = 4096
DEPTH = 4
C = 256
LT = C + S
NROWS = B * LT
GRID_W = 64
IMG_ROWS = S // GRID_W
H = 16
HD = 64
NPAIR = H // 2
LANES = 128
WIN_ROWS = 8
WIN_COLS = 16
D_FF = 3 * D
LRU_BLOCK_W = 64
LRU_C = 8.0
EPS = 1e-6
NEG_INF = float("-inf")
LOG2_E = 1.4426950408889634

TM = 544
TPB = LT // TM
FC = 512
QB = 256
NQB = LT // QB
TS = 64
NCH = LT // TS
CCH = C // TS
PITCH = TS + 4
GW = 256
VMEM_LIMIT = 56 * 1024 * 1024

T2_LEFT_DEAD = 16
T2_RIGHT_DEAD = 17
T2_DEAD = 18
T2_ENTRIES = 19


def _cparams(n_axes):
    return pltpu.CompilerParams(
        dimension_semantics=("arbitrary",) * n_axes, vmem_limit_bytes=VMEM_LIMIT)


def _const_spec(shape):
    nd = len(shape)
    return pl.BlockSpec(shape, lambda *_: (0,) * nd, pipeline_mode=pl.Buffered(1))


def _mods_kernel(cc_ref, w_ref, b_ref, o_ref):
    c = cc_ref[...]
    s = c * jax.nn.sigmoid(c)
    o_ref[...] = jnp.dot(s.astype(BF16), w_ref[...].astype(BF16),
                         preferred_element_type=F32) + b_ref[...]


def _mods_call(cc, ada_w, ada_b):
    return pl.pallas_call(
        _mods_kernel,
        grid=(DEPTH, 6),
        in_specs=[
            pl.BlockSpec((16, D), lambda l, k: (0, 0)),
            pl.BlockSpec((None, D, D), lambda l, k: (l, 0, k)),
            pl.BlockSpec((None, None, 1, D), lambda l, k: (l, k, 0, 0)),
        ],
        out_specs=pl.BlockSpec((None, None, 16, D), lambda l, k: (l, k, 0, 0)),
        out_shape=jax.ShapeDtypeStruct((DEPTH, 6, 16, D), F32),
        compiler_params=_cparams(2),
        name="adaln_mods",
    )(cc, ada_w, ada_b.reshape(DEPTH, 6, 1, D))


class Tiling(NamedTuple):
    tm: int
    tpb: int
    base: int
    main_blocks: int


STREAM = Tiling(TM, TPB, 0, 1)
LATENT = Tiling(512, S // 512, C, 2)


def _tile_pos(n, lo, t=STREAM):
    i = pl.program_id(0)
    b = i // t.tpb
    j = i % t.tpb
    pos = t.base + j * t.tm + lo + lax.broadcasted_iota(jnp.int32, (n, 1), 0)
    return b, pos


def _mod_rows(m_ref, b, is_ctx):
    return jnp.where(is_ctx, m_ref[8:9, :], m_ref[pl.ds(b, 1), :])


def _modulate(x, gain, shift, scale):
    y = x * lax.rsqrt(jnp.mean(x * x, axis=-1, keepdims=True) + EPS)
    return (y * gain) * (1.0 + scale) + shift


def _col_chunk_specs(j, width, n):
    return [pl.BlockSpec((None, D, width), lambda *_, c=c: (j, 0, c),
                         pipeline_mode=pl.Buffered(1)) for c in range(n)]


def _mods_spec(layer, kind):
    return pl.BlockSpec((None, None, 16, D), lambda *_: (layer, kind, 0, 0))


def _row_spec(width=D):
    return pl.BlockSpec((TM, width), lambda i: (i, 0))


def _pair_spec():
    return pl.BlockSpec((NPAIR, TM, LANES), lambda i: (0, i, 0))


def _qkv_kernel(x_ref, sh_ref, sc_ref, gain_ref, qg_ref, kg_ref, g1_ref, wq_ref, wk_ref, wv_ref,
                q_ref, k_ref, v_ref):
    b, pos = _tile_pos(TM, 0)
    is_ctx = pos < C
    h = _modulate(x_ref[...], gain_ref[...], _mod_rows(sh_ref, b, is_ctx),
                  _mod_rows(sc_ref, b, is_ctx)).astype(BF16)
    head_col = lax.shift_right_logical(lax.broadcasted_iota(jnp.int32, (TM, LANES), 1), 6)
    def head_norm(acc, o_ref, gn_ref):
        ss = jnp.dot((acc * acc).astype(BF16), g1_ref[...], preferred_element_type=F32)
        rstd = lax.rsqrt(ss * (1.0 / HD) + EPS)
        for p in range(NPAIR):
            ls = slice(p * LANES, (p + 1) * LANES)
            rb = jnp.take_along_axis(rstd, head_col + 2 * p, axis=1, mode="promise_in_bounds")
            o_ref[p] = (acc[:, ls] * rb * gn_ref[:, ls]).astype(BF16)

    head_norm(jnp.dot(h, wq_ref[...], preferred_element_type=F32), q_ref, qg_ref)
    head_norm(jnp.dot(h, wk_ref[...], preferred_element_type=F32), k_ref, kg_ref)
    v = jnp.dot(h, wv_ref[...], preferred_element_type=F32).astype(BF16)
    first_head = lax.broadcasted_iota(jnp.int32, (1, LANES), 1) < HD
    one = jnp.ones((), BF16)
    for p in range(NPAIR):
        vp = v[:, p * LANES:(p + 1) * LANES]
        v_ref[0, p] = jnp.where(first_head, vp, one)
        v_ref[1, p] = jnp.where(first_head, one, vp)


def _qkv_call(xs, mods, layer, gain, w, j, qg, kg, g1):
    out = jax.ShapeDtypeStruct((NPAIR, NROWS, LANES), BF16)
    return pl.pallas_call(
        _qkv_kernel,
        grid=(NROWS // TM,),
        in_specs=[
            _row_spec(), _mods_spec(layer, 0), _mods_spec(layer, 1), _const_spec((1, D)),
            _const_spec((1, D)), _const_spec((1, D)), _const_spec((D, LANES)),
        ] + _col_chunk_specs(j, D, 3),
        out_specs=[_pair_spec(), _pair_spec(),
                   pl.BlockSpec((2, NPAIR, TM, LANES), lambda i: (0, 0, i, 0))],
        out_shape=[out, out, jax.ShapeDtypeStruct((2, NPAIR, NROWS, LANES), BF16)],
        compiler_params=_cparams(1),
        name="na_qkv",
    )(xs, mods, mods, gain, qg, kg, g1, w, w, w)


def _dot_t(a, b):
    return lax.dot_general(a, b, (((1,), (1,)), ((), ())), preferred_element_type=F32)


def _bias_tile_table():
    tab = np.zeros((NQB, 24), np.int32)
    for jj in range(NQB):
        rg = jj - 1
        win0 = min(max(rg, 1), NQB - 3) - 1
        off = 4 * win0 - 4 * rg + (WIN_ROWS - 1)
        for i in range(4):
            rs = min(max(4 * rg + i - WIN_ROWS // 2, 0), IMG_ROWS - WIN_ROWS)
            for pu in range(6):
                kl = 4 * win0 + 2 * pu
                lv = jj > 0 and rs <= kl < rs + WIN_ROWS
                rv = jj > 0 and rs <= kl + 1 < rs + WIN_ROWS
                d = 2 * pu - i + off
                if lv and rv:
                    assert 0 <= d <= 2 * WIN_ROWS - 3
                    tab[jj, i * 6 + pu] = d + 1
                elif rv:
                    assert d + 1 == WIN_ROWS // 2 - 1
                    tab[jj, i * 6 + pu] = T2_LEFT_DEAD
                elif lv:
                    assert d == WIN_ROWS // 2 - 1 + WIN_ROWS - 1
                    tab[jj, i * 6 + pu] = T2_RIGHT_DEAD
                else:
                    tab[jj, i * 6 + pu] = T2_DEAD
    return tab


def _attn_kernel(idx_ref, q_ref, kc_ref, vc_ref, k0_ref, k1_ref, k2_ref, v0_ref, v1_ref, v2_ref,
                 t2_ref, o_ref):
    jj = pl.program_id(1)
    tile_idx = [[idx_ref[jj, i * 6 + pu] for pu in range(6)] for i in range(4)]
    lane = lax.broadcasted_iota(jnp.int32, (1, LANES), 1)
    k_refs = (k0_ref, k1_ref, k2_ref)
    v_refs = (v0_ref, v1_ref, v2_ref)

    def scores(head, n_lat):
        p, hh = divmod(head, 2)
        qp = q_ref[p]
        sel = (lane < HD) if hh == 0 else (lane >= HD)
        qm = jnp.where(sel, qp, jnp.zeros_like(qp))
        s_all = [_dot_t(qm, kc_ref[p])]
        for j in range(n_lat):
            bias = jnp.concatenate(
                [jnp.concatenate([t2_ref[head, tile_idx[i][2 * j + cp]] for cp in range(2)],
                                 axis=1) for i in range(4)], axis=0)
            s_all.append(_dot_t(qm, k_refs[j][p]) + bias)
        mm = s_all[0]
        if n_lat:
            mm = jnp.maximum(jnp.maximum(mm, s_all[1]), jnp.maximum(s_all[2], s_all[3]))
        m = jnp.max(jnp.maximum(mm[:, :LANES], mm[:, LANES:]), axis=-1, keepdims=True)
        return s_all, m

    def weighted_values(head, s_all, m):
        p, hh = divmod(head, 2)
        acc = jnp.dot(jnp.exp2(s_all[0] - m).astype(BF16), vc_ref[hh, p],
                      preferred_element_type=F32)
        for j in range(len(s_all) - 1):
            acc = acc + jnp.dot(jnp.exp2(s_all[j + 1] - m).astype(BF16), v_refs[j][hh, p],
                                preferred_element_type=F32)
        return acc / pltpu.roll(acc, HD, axis=1)

    def all_heads(n_lat):
        pending = scores(0, n_lat)
        outs = []
        for head in range(H):
            nxt = scores(head + 1, n_lat) if head + 1 < H else None
            outs.append(weighted_values(head, *pending))
            pending = nxt
            if head % 2 == 1:
                o_ref[head // 2] = jnp.where(lane < HD, outs[0], outs[1]).astype(BF16)
                outs = []

    @pl.when(jj == 0)
    def _():
        all_heads(0)

    @pl.when(jj > 0)
    def _():
        all_heads(3)


def _attn_call(q, k, v, t2):
    def blk(fn):
        return pl.BlockSpec((NPAIR, QB, LANES), lambda b, jj, idx: (0, fn(b, jj), 0))

    def vblk(fn):
        return pl.BlockSpec((2, NPAIR, QB, LANES), lambda b, jj, idx: (0, 0, fn(b, jj), 0))

    def win(o):
        return lambda b, jj: b * NQB + jnp.clip(jj - 1, 1, NQB - 3) + o

    return pl.pallas_call(
        _attn_kernel,
        grid_spec=pltpu.PrefetchScalarGridSpec(
            num_scalar_prefetch=1,
            grid=(B, NQB),
            in_specs=[
                blk(lambda b, jj: b * NQB + jj),
                blk(lambda b, jj: b * NQB), vblk(lambda b, jj: b * NQB),
                blk(win(0)), blk(win(1)), blk(win(2)),
                vblk(win(0)), vblk(win(1)), vblk(win(2)),
                _const_spec((H, T2_ENTRIES, GRID_W, LANES)),
            ],
            out_specs=blk(lambda b, jj: b * NQB + jj),
        ),
        out_shape=jax.ShapeDtypeStruct((NPAIR, NROWS, LANES), BF16),
        compiler_params=_cparams(2),
        name="na_attention",
    )(jnp.asarray(_bias_tile_table()), q, k, v, k, k, k, v, v, v, t2)


def _proj_kernel(x_ref, y_ref, gate_ref, w_ref, o_ref):
    b, pos = _tile_pos(TM, 0)
    y = jnp.concatenate([y_ref[p] for p in range(NPAIR)], axis=1)
    o_ref[...] = x_ref[...] + _mod_rows(gate_ref, b, pos < C) * jnp.dot(
        y, w_ref[...], preferred_element_type=F32)


def _proj_call(xs, y, mods, layer, w):
    return pl.pallas_call(
        _proj_kernel,
        grid=(NROWS // TM,),
        in_specs=[_row_spec(), _pair_spec(), _mods_spec(layer, 2), _const_spec((D, D))],
        out_specs=_row_spec(),
        out_shape=jax.ShapeDtypeStruct((NROWS, D), F32),
        compiler_params=_cparams(1),
        name="na_out_proj",
    )(xs, y, mods, w)


def _bias_table(rpb):
    qc = jnp.arange(GRID_W)[:, None]
    kc = jnp.arange(GRID_W)[None, :]
    ws = jnp.clip(qc - WIN_COLS // 2, 0, GRID_W - WIN_COLS)
    col_ok = (kc >= ws) & (kc < ws + WIN_COLS)
    onehot = ((kc - qc + (WIN_COLS - 1))[None] == jnp.arange(2 * WIN_COLS - 1)[:, None, None])
    t = jnp.einsum("hdc,cqk->hdqk", rpb, onehot.astype(F32), precision=lax.Precision.HIGHEST)
    t = jnp.where(col_ok[None, None], t, NEG_INF)
    dead = jnp.full((H, 1, GRID_W, GRID_W), NEG_INF, F32)
    t_ext = jnp.concatenate([dead, t, dead], axis=1)
    pairs = jnp.concatenate([t_ext[:, :-1], t_ext[:, 1:]], axis=-1)
    lo, hi = WIN_ROWS // 2 - 1, WIN_ROWS // 2 - 1 + WIN_ROWS - 1
    left_dead = jnp.concatenate([dead, t[:, lo:lo + 1]], axis=-1)
    right_dead = jnp.concatenate([t[:, hi:hi + 1], dead], axis=-1)
    all_dead = jnp.concatenate([dead, dead], axis=-1)
    return jnp.concatenate([pairs, left_dead, right_dead, all_dead], axis=1)


assert 8 < C % TM < TM - 8
BND = C % TM
BND_LO = BND - 8


def _halo_specs():
    t8 = TM // 8
    return [
        pl.BlockSpec((8, D), lambda i: (jnp.maximum(i * t8 - 1, 0), 0)),
        _row_spec(),
        pl.BlockSpec((8, D), lambda i: (jnp.minimum((i + 1) * t8, NROWS // 8 - 1), 0)),
    ]


def _dead_halo_rows(reach_back, t=STREAM):
    start = t.base + (pl.program_id(0) % t.tpb) * t.tm
    e = lax.broadcasted_iota(jnp.int32, (t.tm + 16, 1), 0)
    at_start = (start == 0) | (start == C)
    at_end = (start + t.tm == C) | (start + t.tm == LT)
    return ((e >= 8 - reach_back) & (e < 8) & at_start) | ((e == t.tm + 8) & at_end)


def _slab_conv(u_scr, s, taps, cb, pos, t=STREAM):
    def seg(lo, n, masked):
        y = cb
        for off, w in taps:
            u = u_scr[s, pl.ds(lo + 8 + off, n), :]
            if masked and off != 0:
                p = pos[lo:lo + n]
                crosses = (p >= C) & (p < C - off) if off < 0 else (p < C) & (p >= C - off)
                u = jnp.where(crosses, 0.0, u)
            y = y + u * w
        return y
    if t is LATENT:
        return seg(0, t.tm, False)
    return jnp.concatenate([seg(0, BND_LO, False), seg(BND_LO, 16, True),
                            seg(BND_LO + 16, TM - BND_LO - 16, False)], axis=0)


def _lru_in_kernel(xp_ref, x_ref, xn_ref, sh_ref, sc_ref, gain_ref, cw_ref, cb_ref, *rest):
    w_ref = rest[:2 * D // GW]
    gg_ref, xr_ref, rec_scr = rest[2 * D // GW:]
    b, pos = _tile_pos(TM, 0)
    _, pos_e = _tile_pos(TM + 16, -8)
    is_ctx_e = pos_e < C
    x_e = jnp.concatenate([xp_ref[...], x_ref[...], xn_ref[...]], axis=0)
    h = _modulate(x_e, gain_ref[...], _mod_rows(sh_ref, b, is_ctx_e),
                  _mod_rows(sc_ref, b, is_ctx_e))
    h = jnp.where(_dead_halo_rows(2), 0.0, h).astype(BF16)
    cw = cw_ref[...]
    cb = cb_ref[...]
    ngroup = D // GW

    def project(c):
        return (jnp.dot(h, w_ref[c][...], preferred_element_type=F32),
                jnp.dot(h, w_ref[ngroup + c][...], preferred_element_type=F32))

    pending = project(0)
    for c in range(ngroup):
        nxt = project(c + 1) if c + 1 < ngroup else None
        gate, rec = pending
        gg_ref[:, c * GW:(c + 1) * GW] = jax.nn.gelu(
            gate[8:8 + TM], approximate=True).astype(BF16)
        for s2 in range(GW // LANES):
            s = c * (GW // LANES) + s2
            ls = slice(s * LANES, (s + 1) * LANES)
            rec_scr[s] = rec[:, s2 * LANES:(s2 + 1) * LANES]
            taps = [(k - 2, cw[k:k + 1, ls]) for k in range(4)]
            xr_ref[:, ls] = _slab_conv(rec_scr, s, taps, cb[:, ls], pos)
        pending = nxt


def _lru_in_call(xs, mods, layer, gain, w, j, cw, cb):
    return pl.pallas_call(
        _lru_in_kernel,
        grid=(NROWS // TM,),
        in_specs=_halo_specs() + [
            _mods_spec(layer, 0), _mods_spec(layer, 1), _const_spec((1, D)),
            _const_spec((4, D)), _const_spec((1, D)),
        ] + _col_chunk_specs(j, GW, 2 * D // GW),
        out_specs=[_row_spec(), _row_spec()],
        out_shape=[jax.ShapeDtypeStruct((NROWS, D), BF16),
                   jax.ShapeDtypeStruct((NROWS, D), F32)],
        scratch_shapes=[pltpu.VMEM((D // LANES, TM + 16, LANES), F32)],
        compiler_params=_cparams(1),
        name="lru_in_proj",
    )(xs, xs, xs, mods, mods, gain, cw, cb, *([w] * (2 * D // GW)))


def _chunk_index(g, reverse):
    if not reverse:
        return g
    return jnp.where(g < CCH, CCH - 1 - g, NCH + CCH - 1 - g)


def _scan_kernel(xr_ref, lam_ref, wg_ref, ba_ref, bx_ref, *rest, reverse):
    if reverse:
        o_ref, a_scr, b_scr, h_scr = rest
    else:
        hsb_ref, gg_ref, x_ref, wo_ref, g1_ref, o_ref, a_scr, b_scr, h_scr = rest
    g = pl.program_id(0)

    @pl.when(g == 0)
    def _():
        h_scr[...] = jnp.zeros_like(h_scr)

    lam = lam_ref[...]
    half_log_a = (-0.5 * LRU_C) * (jnp.maximum(-lam, 0.0) + jnp.log1p(jnp.exp(-jnp.abs(lam))))
    nslab = D // LANES
    group_slabs = GW // LANES

    def gates(gq):
        cols = slice(gq * GW, (gq + 1) * GW)
        xg = jnp.concatenate([xr_ref[b, :, cols] for b in range(B)], axis=0)
        pre = jnp.dot(xg.astype(BF16), wg_ref[gq], preferred_element_type=F32)
        t_r = jnp.tanh(pre[:, :GW] + ba_ref[:, cols])
        t_i = jnp.tanh(pre[:, GW:] + bx_ref[:, cols])
        log_a = half_log_a[:, cols] * t_r + half_log_a[:, cols]
        a = jnp.exp(log_a)
        y = (-1.0 - a * a) * jnp.tanh(log_a)
        root = jnp.where(y > 0.0, y * lax.rsqrt(y), 0.0)
        hx = 0.5 * xg
        bb = root * (hx * t_i + hx)
        for k2 in range(group_slabs):
            k = gq * group_slabs + k2
            for b in range(B):
                rows = slice(k * B * PITCH + b * PITCH, k * B * PITCH + b * PITCH + TS)
                a_scr[rows, :] = a[b * TS:(b + 1) * TS, k2 * LANES:(k2 + 1) * LANES]
                b_scr[rows, :] = bb[b * TS:(b + 1) * TS, k2 * LANES:(k2 + 1) * LANES]

    def recurrence(gq):
        slabs = range(gq * group_slabs, (gq + 1) * group_slabs)
        hs = {k: h_scr[k] for k in slabs}
        for t in (range(TS - 1, -1, -1) if reverse else range(TS)):
            for k in slabs:
                idx = pl.ds(k * B * PITCH + t, B, stride=PITCH)
                hs[k] = a_scr[idx, :] * hs[k] + b_scr[idx, :]
                b_scr[idx, :] = hs[k]
        for k in slabs:
            h_scr[k] = hs[k]

    def slab_rows(b, k):
        return slice(k * B * PITCH + b * PITCH, k * B * PITCH + b * PITCH + TS)

    gates(0)
    for gq in range(D // GW):
        if gq + 1 < D // GW:
            gates(gq + 1)
        recurrence(gq)

    if reverse:
        for b in range(B):
            for k in range(nslab):
                o_ref[b, :, k * LANES:(k + 1) * LANES] = b_scr[slab_rows(b, k), :].astype(BF16)
    else:
        z = jnp.concatenate([
            jnp.concatenate([
                ((b_scr[slab_rows(b, k), :] + hsb_ref[b, :, k * LANES:(k + 1) * LANES])
                 * gg_ref[b, :, k * LANES:(k + 1) * LANES].astype(F32)).astype(BF16)
                for k in range(nslab)], axis=1)
            for b in range(B)], axis=0)
        proj = jnp.dot(z, wo_ref[...], preferred_element_type=F32)
        is_ctx = g < CCH
        for b in range(B):
            g1 = jnp.where(is_ctx, g1_ref[8:9, :], g1_ref[b:b + 1, :])
            o_ref[b] = x_ref[b] + g1 * proj[b * TS:(b + 1) * TS]


def _scan_call(xr3, lam, wg, ba, bx, reverse, fwd_args=None):
    main = pl.BlockSpec((B, TS, D), lambda g: (0, _chunk_index(g, reverse), 0))
    in_specs = [
        main, _const_spec((1, D)),
        _const_spec((D // GW, GW, 2 * GW)), _const_spec((1, D)), _const_spec((1, D)),
    ]
    args = [xr3, lam, wg, 0.5 * ba, 0.5 * bx]
    if not reverse:
        hsb, gg, xs3, wo, mods, layer = fwd_args
        in_specs += [main, main, main, _const_spec((D, D)), _mods_spec(layer, 2)]
        args += [hsb, gg, xs3, wo, mods]
    slab = pltpu.VMEM((D // LANES * B * PITCH, LANES), F32)
    return pl.pallas_call(
        functools.partial(_scan_kernel, reverse=reverse),
        grid=(NCH,),
        in_specs=in_specs,
        out_specs=main,
        out_shape=jax.ShapeDtypeStruct((B, LT, D), BF16 if reverse else F32),
        scratch_shapes=[slab, slab, pltpu.VMEM((D // LANES, B, LANES), F32)],
        compiler_params=_cparams(1),
        name="lru_scan_bwd" if reverse else "lru_scan_fwd",
    )(*args)


def _gate_weights(wa, wx):
    def bd(w):
        w4 = w.reshape(D // GW, GW // LRU_BLOCK_W, LRU_BLOCK_W, LRU_BLOCK_W)
        eye = jnp.eye(GW // LRU_BLOCK_W, dtype=w.dtype)
        return (w4[:, :, :, None, :] * eye[None, :, None, :, None]).reshape(D // GW, GW, GW)
    return (0.5 * jnp.concatenate([bd(wa), bd(wx)], axis=-1)).astype(BF16)


FSLAB = FC // LANES


def _ffn_kernel(*refs, t):
    nchunk = 2 * D_FF // FC
    xp_ref, *x_refs = refs[:1 + t.main_blocks]
    (xn_ref, sh_ref, sc_ref, g2_ref, gain_ref, cw_ref, cb_ref,
     wd_ref) = refs[1 + t.main_blocks:9 + t.main_blocks]
    wu_ref = refs[9 + t.main_blocks:9 + t.main_blocks + nchunk]
    o_ref, uv_scr, ug_scr = refs[9 + t.main_blocks + nchunk:]
    tm = t.tm
    b, pos = _tile_pos(tm, 0, t)
    _, pos_e = _tile_pos(tm + 16, -8, t)
    is_ctx_e = pos_e < C
    x = jnp.concatenate([r[...] for r in x_refs], axis=0)
    x_e = jnp.concatenate([xp_ref[...], x, xn_ref[...]], axis=0)
    h = _modulate(x_e, gain_ref[...], _mod_rows(sh_ref, b, is_ctx_e),
                  _mod_rows(sc_ref, b, is_ctx_e))
    h = jnp.where(_dead_halo_rows(1, t), 0.0, h).astype(BF16)

    cw = cw_ref[...]
    cb = cb_ref[...]
    def up_project(ci):
        uv = jnp.dot(h, wu_ref[ci][...], preferred_element_type=F32)
        ug = jnp.dot(h, wu_ref[D_FF // FC + ci][...], preferred_element_type=F32)
        base = (ci % 2) * FSLAB
        for s in range(FSLAB):
            uv_scr[base + s] = uv[:, s * LANES:(s + 1) * LANES]
            ug_scr[base + s] = ug[:, s * LANES:(s + 1) * LANES]

    acc = jnp.zeros((tm, D), F32)
    up_project(0)
    for ci in range(D_FF // FC):
        if ci + 1 < D_FF // FC:
            up_project(ci + 1)
        base = (ci % 2) * FSLAB
        acts = []
        for s in range(FSLAB):
            lv = slice(ci * FC + s * LANES, ci * FC + (s + 1) * LANES)
            lg = slice(D_FF + lv.start, D_FF + lv.stop)
            val = _slab_conv(uv_scr, base + s, [(k - 1, cw[k:k + 1, lv]) for k in range(3)],
                             cb[:, lv], pos, t)
            hg = _slab_conv(ug_scr, base + s,
                            [(k - 1, 0.5 * cw[k:k + 1, lg]) for k in range(3)],
                            0.5 * cb[:, lg], pos, t)
            acts.append((val * (hg + hg * jnp.tanh(hg))).astype(BF16))
        acc = acc + jnp.dot(jnp.concatenate(acts, axis=1), wd_ref[ci],
                            preferred_element_type=F32)
    o_ref[...] = x + _mod_rows(g2_ref, b, pos < C) * acc


def _ffn_call(xs, mods, layer, gain, wu, cw, cb, wd, t):
    nchunk = 2 * D_FF // FC

    def layer_spec(shape):
        return pl.BlockSpec((None,) + shape, lambda i: (layer,) + (0,) * len(shape),
                            pipeline_mode=pl.Buffered(1))

    slab = pltpu.VMEM((2 * FSLAB, t.tm + 16, LANES), F32)
    if t is STREAM:
        x_specs, out_rows = _halo_specs(), NROWS
    else:
        def row0(i):
            return (i // t.tpb) * LT + t.base + (i % t.tpb) * t.tm
        half = t.tm // t.main_blocks
        x_specs = (
            [pl.BlockSpec((8, D), lambda i: (row0(i) // 8 - 1, 0))]
            + [pl.BlockSpec((half, D), lambda i, k=k: (row0(i) // half + k, 0))
               for k in range(t.main_blocks)]
            + [pl.BlockSpec((8, D), lambda i: (jnp.minimum((row0(i) + t.tm) // 8,
                                                            NROWS // 8 - 1), 0))])
        out_rows = B * S
    return pl.pallas_call(
        functools.partial(_ffn_kernel, t=t),
        grid=(out_rows // t.tm,),
        in_specs=x_specs + [
            _mods_spec(layer, 3), _mods_spec(layer, 4), _mods_spec(layer, 5),
            _const_spec((1, D)), layer_spec((3, 2 * D_FF)),
            layer_spec((1, 2 * D_FF)), layer_spec((D_FF // FC, FC, D)),
        ] + _col_chunk_specs(layer, FC, nchunk),
        out_specs=pl.BlockSpec((t.tm, D), lambda i: (i, 0)),
        out_shape=jax.ShapeDtypeStruct((out_rows, D), F32),
        scratch_shapes=[slab, slab],
        compiler_params=_cparams(1),
        name="conv_ffn",
    )(*([xs] * (2 + t.main_blocks)), mods, mods, mods, gain, cw, cb, wd, *([wu] * nchunk))


def kernel(x, c, ctx, c_ctx, ada_w, ada_b, norm_mix, norm_ffn, na_w_qkv, na_q_gain, na_k_gain,
           na_rpb, na_w_out, lru_w_in, lru_conv_w, lru_conv_b, lru_ga_w, lru_ga_b, lru_gx_w,
           lru_gx_b, lru_lambda, lru_w_out, ffn_w_up, ffn_conv_w, ffn_conv_b, ffn_w_down):
    xs = jnp.concatenate([ctx, x], axis=1).reshape(NROWS, D)
    cc = jnp.concatenate([c, c_ctx[None], jnp.zeros((7, D), F32)], axis=0)
    mods = _mods_call(cc, ada_w, ada_b)

    head_of_lane = jnp.arange(D) // HD
    g1 = (head_of_lane[:, None] == jnp.arange(LANES)[None, :]).astype(BF16)

    w_qkv = na_w_qkv.astype(BF16)
    w_in = lru_w_in.astype(BF16)
    w_up = ffn_w_up.astype(BF16)
    w_down = ffn_w_down.astype(BF16).reshape(DEPTH, D_FF // FC, FC, D)
    conv_b = ffn_conv_b[:, None]

    for i in range(DEPTH):
        j = i // 2
        gain_mix = norm_mix[i][None]
        if i % 2 == 0:
            qg = (jnp.tile(na_q_gain[j], H) * (HD ** -0.5 * LOG2_E))[None]
            kg = jnp.tile(na_k_gain[j], H)[None]
            q, k, v = _qkv_call(xs, mods, i, gain_mix, w_qkv, j, qg, kg, g1)
            o = _attn_call(q, k, v, _bias_table(na_rpb[j] * LOG2_E))
            xs = _proj_call(xs, o, mods, i, na_w_out[j].astype(BF16))
        else:
            gg, xr = _lru_in_call(xs, mods, i, gain_mix, w_in, j,
                                  lru_conv_w[j], lru_conv_b[j][None])
            xr3 = xr.reshape(B, LT, D)
            hsb = _scan_call(xr3, lru_lambda[j, 1][None],
                             _gate_weights(lru_ga_w[j, 1], lru_gx_w[j, 1]),
                             lru_ga_b[j, 1][None], lru_gx_b[j, 1][None], True)
            xs = _scan_call(xr3, lru_lambda[j, 0][None],
                            _gate_weights(lru_ga_w[j, 0], lru_gx_w[j, 0]),
                            lru_ga_b[j, 0][None], lru_gx_b[j, 0][None], False,
                            (hsb, gg.reshape(B, LT, D), xs.reshape(B, LT, D),
                             lru_w_out[j].astype(BF16), mods, i)).reshape(NROWS, D)
        xs = _ffn_call(xs, mods, i, norm_ffn[i][None], w_up, ffn_conv_w, conv_b, w_down,
                       LATENT if i == DEPTH - 1 else STREAM)
    return xs.reshape(B, S, D)
```

```python
import functools
from typing import NamedTuple

import jax
import jax.numpy as jnp
import numpy as np
from jax import lax
from jax.experimental import pallas as pl
from jax.experimental.pallas import tpu as pltpu

F32 = jnp.float32
BF16 = jnp.bfloat16

D = 1024
B = 8
S = 4096
DEPTH = 4
C = 256
LT = C + S
NROWS = B * LT
GRID_W = 64
IMG_ROWS = S // GRID_W
H = 16
HD = 64
NPAIR = H // 2
LANES = 128
WIN_ROWS = 8
WIN_COLS = 16
D_FF = 3 * D
LRU_BLOCK_W = 64
LRU_C = 8.0
EPS = 1e-6
NEG_INF = float("-inf")
LOG2_E = 1.4426950408889634

TM = 544
TPB = LT // TM
FC = 512
QB = 256
NQB = LT // QB
TS = 64
NCH = LT // TS
CCH = C // TS
PITCH = TS + 4
GW = 256
VMEM_LIMIT = 56 * 1024 * 1024

T2_LEFT_DEAD = 16
T2_RIGHT_DEAD = 17
T2_DEAD = 18
T2_ENTRIES = 19


def _cparams(n_axes):
    return pltpu.CompilerParams(
        dimension_semantics=("arbitrary",) * n_axes, vmem_limit_bytes=VMEM_LIMIT)


def _const_spec(shape):
    nd = len(shape)
    return pl.BlockSpec(shape, lambda *_: (0,) * nd, pipeline_mode=pl.Buffered(1))


def _mods_kernel(cc_ref, w_ref, b_ref, o_ref):
    c = cc_ref[...]
    s = c * jax.nn.sigmoid(c)
    o_ref[...] = jnp.dot(s.astype(BF16), w_ref[...].astype(BF16),
                         preferred_element_type=F32) + b_ref[...]


def _mods_call(cc, ada_w, ada_b):
    return pl.pallas_call(
        _mods_kernel,
        grid=(DEPTH, 6),
        in_specs=[
            pl.BlockSpec((16, D), lambda l, k: (0, 0)),
            pl.BlockSpec((None, D, D), lambda l, k: (l, 0, k)),
            pl.BlockSpec((None, None, 1, D), lambda l, k: (l, k, 0, 0)),
        ],
        out_specs=pl.BlockSpec((None, None, 16, D), lambda l, k: (l, k, 0, 0)),
        out_shape=jax.ShapeDtypeStruct((DEPTH, 6, 16, D), F32),
        compiler_params=_cparams(2),
        name="adaln_mods",
    )(cc, ada_w, ada_b.reshape(DEPTH, 6, 1, D))


class Tiling(NamedTuple):
    tm: int
    tpb: int
    base: int
    main_blocks: int


STREAM = Tiling(TM, TPB, 0, 1)
LATENT = Tiling(512, S // 512, C, 2)
BLOCKS = Tiling(QB, NQB, 0, 1)


def _tile_pos(n, lo, t=STREAM):
    i = pl.program_id(0)
    b = i // t.tpb
    j = i % t.tpb
    pos = t.base + j * t.tm + lo + lax.broadcasted_iota(jnp.int32, (n, 1), 0)
    return b, pos


def _mod_rows(m_ref, b, is_ctx):
    return jnp.where(is_ctx, m_ref[8:9, :], m_ref[pl.ds(b, 1), :])


def _modulate(x, gain, shift, scale):
    y = x * lax.rsqrt(jnp.mean(x * x, axis=-1, keepdims=True) + EPS)
    return (y * gain) * (1.0 + scale) + shift


def _col_chunk_specs(j, width, n):
    return [pl.BlockSpec((None, D, width), lambda *_, c=c: (j, 0, c),
                         pipeline_mode=pl.Buffered(1)) for c in range(n)]


def _mods_spec(layer, kind):
    return pl.BlockSpec((None, None, 16, D), lambda *_: (layer, kind, 0, 0))


def _row_spec(width=D, t=STREAM):
    return pl.BlockSpec((t.tm, width), lambda i: (i, 0))


def _pair_spec(t=STREAM):
    return pl.BlockSpec((NPAIR, t.tm, LANES), lambda i: (0, i, 0))


def _source_specs():
    return [pl.BlockSpec((QB, D), lambda i: (i // NQB, 0)),
            pl.BlockSpec((QB, D),
                         lambda i: ((i // NQB) * (S // QB) + jnp.maximum(i % NQB - 1, 0), 0))]


def _stream_rows(x_refs):
    if len(x_refs) == 1:
        return x_refs[0][...]
    ctx_ref, lat_ref = x_refs
    return jnp.where(pl.program_id(0) % NQB == 0, ctx_ref[...], lat_ref[...])


def _qkv_kernel(*refs, t):
    x_refs = refs[:-12]
    (sh_ref, sc_ref, gain_ref, qg_ref, kg_ref, g1_ref, wq_ref, wk_ref, wv_ref,
     q_ref, k_ref, v_ref) = refs[-12:]
    b, pos = _tile_pos(t.tm, 0, t)
    is_ctx = pos < C
    h = _modulate(_stream_rows(x_refs), gain_ref[...], _mod_rows(sh_ref, b, is_ctx),
                  _mod_rows(sc_ref, b, is_ctx)).astype(BF16)
    head_col = lax.shift_right_logical(lax.broadcasted_iota(jnp.int32, (t.tm, LANES), 1), 6)
    def head_norm(acc, o_ref, gn_ref):
        ss = jnp.dot((acc * acc).astype(BF16), g1_ref[...], preferred_element_type=F32)
        rstd = lax.rsqrt(ss * (1.0 / HD) + EPS)
        for p in range(NPAIR):
            ls = slice(p * LANES, (p + 1) * LANES)
            rb = jnp.take_along_axis(rstd, head_col + 2 * p, axis=1, mode="promise_in_bounds")
            o_ref[p] = (acc[:, ls] * rb * gn_ref[:, ls]).astype(BF16)

    head_norm(jnp.dot(h, wq_ref[...], preferred_element_type=F32), q_ref, qg_ref)
    head_norm(jnp.dot(h, wk_ref[...], preferred_element_type=F32), k_ref, kg_ref)
    v = jnp.dot(h, wv_ref[...], preferred_element_type=F32).astype(BF16)
    first_head = lax.broadcasted_iota(jnp.int32, (1, LANES), 1) < HD
    one = jnp.ones((), BF16)
    for p in range(NPAIR):
        vp = v[:, p * LANES:(p + 1) * LANES]
        v_ref[0, p] = jnp.where(first_head, vp, one)
        v_ref[1, p] = jnp.where(first_head, one, vp)


def _stream_source(src):
    if isinstance(src, tuple):
        return BLOCKS, list(src), _source_specs()
    return STREAM, [src], [_row_spec()]


def _qkv_call(src, mods, layer, gain, w, j, qg, kg, g1):
    t, xs, x_specs = _stream_source(src)
    out = jax.ShapeDtypeStruct((NPAIR, NROWS, LANES), BF16)
    return pl.pallas_call(
        functools.partial(_qkv_kernel, t=t),
        grid=(NROWS // t.tm,),
        in_specs=x_specs + [
            _mods_spec(layer, 0), _mods_spec(layer, 1), _const_spec((1, D)),
            _const_spec((1, D)), _const_spec((1, D)), _const_spec((D, LANES)),
        ] + _col_chunk_specs(j, D, 3),
        out_specs=[_pair_spec(t), _pair_spec(t),
                   pl.BlockSpec((2, NPAIR, t.tm, LANES), lambda i: (0, 0, i, 0))],
        out_shape=[out, out, jax.ShapeDtypeStruct((2, NPAIR, NROWS, LANES), BF16)],
        compiler_params=_cparams(1),
        name="na_qkv",
    )(*xs, mods, mods, gain, qg, kg, g1, w, w, w)


def _dot_t(a, b):
    return lax.dot_general(a, b, (((1,), (1,)), ((), ())), preferred_element_type=F32)


def _bias_tile_table():
    tab = np.zeros((NQB, 24), np.int32)
    for jj in range(NQB):
        rg = jj - 1
        win0 = min(max(rg, 1), NQB - 3) - 1
        off = 4 * win0 - 4 * rg + (WIN_ROWS - 1)
        for i in range(4):
            rs = min(max(4 * rg + i - WIN_ROWS // 2, 0), IMG_ROWS - WIN_ROWS)
            for pu in range(6):
                kl = 4 * win0 + 2 * pu
                lv = jj > 0 and rs <= kl < rs + WIN_ROWS
                rv = jj > 0 and rs <= kl + 1 < rs + WIN_ROWS
                d = 2 * pu - i + off
                if lv and rv:
                    assert 0 <= d <= 2 * WIN_ROWS - 3
                    tab[jj, i * 6 + pu] = d + 1
                elif rv:
                    assert d + 1 == WIN_ROWS // 2 - 1
                    tab[jj, i * 6 + pu] = T2_LEFT_DEAD
                elif lv:
                    assert d == WIN_ROWS // 2 - 1 + WIN_ROWS - 1
                    tab[jj, i * 6 + pu] = T2_RIGHT_DEAD
                else:
                    tab[jj, i * 6 + pu] = T2_DEAD
    return tab


def _attn_kernel(idx_ref, q_ref, kc_ref, vc_ref, k0_ref, k1_ref, k2_ref, v0_ref, v1_ref, v2_ref,
                 t2_ref, o_ref):
    jj = pl.program_id(1)
    tile_idx = [[idx_ref[jj, i * 6 + pu] for pu in range(6)] for i in range(4)]
    lane = lax.broadcasted_iota(jnp.int32, (1, LANES), 1)
    k_refs = (k0_ref, k1_ref, k2_ref)
    v_refs = (v0_ref, v1_ref, v2_ref)

    def scores(head, n_lat):
        p, hh = divmod(head, 2)
        qp = q_ref[p]
        sel = (lane < HD) if hh == 0 else (lane >= HD)
        qm = jnp.where(sel, qp, jnp.zeros_like(qp))
        s_all = [_dot_t(qm, kc_ref[p])]
        for j in range(n_lat):
            bias = jnp.concatenate(
                [jnp.concatenate([t2_ref[head, tile_idx[i][2 * j + cp]] for cp in range(2)],
                                 axis=1) for i in range(4)], axis=0)
            s_all.append(_dot_t(qm, k_refs[j][p]) + bias)
        mm = s_all[0]
        if n_lat:
            mm = jnp.maximum(jnp.maximum(mm, s_all[1]), jnp.maximum(s_all[2], s_all[3]))
        m = jnp.max(jnp.maximum(mm[:, :LANES], mm[:, LANES:]), axis=-1, keepdims=True)
        return s_all, m

    def weighted_values(head, s_all, m):
        p, hh = divmod(head, 2)
        acc = jnp.dot(jnp.exp2(s_all[0] - m).astype(BF16), vc_ref[hh, p],
                      preferred_element_type=F32)
        for j in range(len(s_all) - 1):
            acc = acc + jnp.dot(jnp.exp2(s_all[j + 1] - m).astype(BF16), v_refs[j][hh, p],
                                preferred_element_type=F32)
        return acc / pltpu.roll(acc, HD, axis=1)

    def all_heads(n_lat):
        pending = scores(0, n_lat)
        outs = []
        for head in range(H):
            nxt = scores(head + 1, n_lat) if head + 1 < H else None
            outs.append(weighted_values(head, *pending))
            pending = nxt
            if head % 2 == 1:
                o_ref[head // 2] = jnp.where(lane < HD, outs[0], outs[1]).astype(BF16)
                outs = []

    @pl.when(jj == 0)
    def _():
        all_heads(0)

    @pl.when(jj > 0)
    def _():
        all_heads(3)


def _attn_call(q, k, v, t2):
    def blk(fn):
        return pl.BlockSpec((NPAIR, QB, LANES), lambda b, jj, idx: (0, fn(b, jj), 0))

    def vblk(fn):
        return pl.BlockSpec((2, NPAIR, QB, LANES), lambda b, jj, idx: (0, 0, fn(b, jj), 0))

    def win(o):
        return lambda b, jj: b * NQB + jnp.clip(jj - 1, 1, NQB - 3) + o

    return pl.pallas_call(
        _attn_kernel,
        grid_spec=pltpu.PrefetchScalarGridSpec(
            num_scalar_prefetch=1,
            grid=(B, NQB),
            in_specs=[
                blk(lambda b, jj: b * NQB + jj),
                blk(lambda b, jj: b * NQB), vblk(lambda b, jj: b * NQB),
                blk(win(0)), blk(win(1)), blk(win(2)),
                vblk(win(0)), vblk(win(1)), vblk(win(2)),
                _const_spec((H, T2_ENTRIES, GRID_W, LANES)),
            ],
            out_specs=blk(lambda b, jj: b * NQB + jj),
        ),
        out_shape=jax.ShapeDtypeStruct((NPAIR, NROWS, LANES), BF16),
        compiler_params=_cparams(2),
        name="na_attention",
    )(jnp.asarray(_bias_tile_table()), q, k, v, k, k, k, v, v, v, t2)


def _proj_kernel(*refs, t):
    x_refs = refs[:-4]
    y_ref, gate_ref, w_ref, o_ref = refs[-4:]
    b, pos = _tile_pos(t.tm, 0, t)
    y = jnp.concatenate([y_ref[p] for p in range(NPAIR)], axis=1)
    o_ref[...] = _stream_rows(x_refs) + _mod_rows(gate_ref, b, pos < C) * jnp.dot(
        y, w_ref[...], preferred_element_type=F32)


def _proj_call(src, y, mods, layer, w):
    t, xs, x_specs = _stream_source(src)
    return pl.pallas_call(
        functools.partial(_proj_kernel, t=t),
        grid=(NROWS // t.tm,),
        in_specs=x_specs + [_pair_spec(t), _mods_spec(layer, 2), _const_spec((D, D))],
        out_specs=_row_spec(t=t),
        out_shape=jax.ShapeDtypeStruct((NROWS, D), F32),
        compiler_params=_cparams(1),
        name="na_out_proj",
    )(*xs, y, mods, w)


def _bias_table(rpb):
    qc = jnp.arange(GRID_W)[:, None]
    kc = jnp.arange(GRID_W)[None, :]
    ws = jnp.clip(qc - WIN_COLS // 2, 0, GRID_W - WIN_COLS)
    col_ok = (kc >= ws) & (kc < ws + WIN_COLS)
    onehot = ((kc - qc + (WIN_COLS - 1))[None] == jnp.arange(2 * WIN_COLS - 1)[:, None, None])
    t = jnp.einsum("hdc,cqk->hdqk", rpb, onehot.astype(F32), precision=lax.Precision.HIGHEST)
    t = jnp.where(col_ok[None, None], t, NEG_INF)
    dead = jnp.full((H, 1, GRID_W, GRID_W), NEG_INF, F32)
    t_ext = jnp.concatenate([dead, t, dead], axis=1)
    pairs = jnp.concatenate([t_ext[:, :-1], t_ext[:, 1:]], axis=-1)
    lo, hi = WIN_ROWS // 2 - 1, WIN_ROWS // 2 - 1 + WIN_ROWS - 1
    left_dead = jnp.concatenate([dead, t[:, lo:lo + 1]], axis=-1)
    right_dead = jnp.concatenate([t[:, hi:hi + 1], dead], axis=-1)
    all_dead = jnp.concatenate([dead, dead], axis=-1)
    return jnp.concatenate([pairs, left_dead, right_dead, all_dead], axis=1)


assert 8 < C % TM < TM - 8
BND = C % TM
BND_LO = BND - 8


def _halo_specs():
    t8 = TM // 8
    return [
        pl.BlockSpec((8, D), lambda i: (jnp.maximum(i * t8 - 1, 0), 0)),
        _row_spec(),
        pl.BlockSpec((8, D), lambda i: (jnp.minimum((i + 1) * t8, NROWS // 8 - 1), 0)),
    ]


def _dead_halo_rows(reach_back, t=STREAM):
    start = t.base + (pl.program_id(0) % t.tpb) * t.tm
    e = lax.broadcasted_iota(jnp.int32, (t.tm + 16, 1), 0)
    at_start = (start == 0) | (start == C)
    at_end = (start + t.tm == C) | (start + t.tm == LT)
    return ((e >= 8 - reach_back) & (e < 8) & at_start) | ((e == t.tm + 8) & at_end)


def _slab_conv(u_scr, s, taps, cb, pos, t=STREAM):
    def seg(lo, n, masked):
        y = cb
        for off, w in taps:
            u = u_scr[s, pl.ds(lo + 8 + off, n), :]
            if masked and off != 0:
                p = pos[lo:lo + n]
                crosses = (p >= C) & (p < C - off) if off < 0 else (p < C) & (p >= C - off)
                u = jnp.where(crosses, 0.0, u)
            y = y + u * w
        return y
    if t is LATENT:
        return seg(0, t.tm, False)
    return jnp.concatenate([seg(0, BND_LO, False), seg(BND_LO, 16, True),
                            seg(BND_LO + 16, TM - BND_LO - 16, False)], axis=0)


def _lru_in_kernel(xp_ref, x_ref, xn_ref, sh_ref, sc_ref, gain_ref, cw_ref, cb_ref, *rest):
    w_ref = rest[:2 * D // GW]
    gg_ref, xr_ref, rec_scr = rest[2 * D // GW:]
    b, pos = _tile_pos(TM, 0)
    _, pos_e = _tile_pos(TM + 16, -8)
    is_ctx_e = pos_e < C
    x_e = jnp.concatenate([xp_ref[...], x_ref[...], xn_ref[...]], axis=0)
    h = _modulate(x_e, gain_ref[...], _mod_rows(sh_ref, b, is_ctx_e),
                  _mod_rows(sc_ref, b, is_ctx_e))
    h = jnp.where(_dead_halo_rows(2), 0.0, h).astype(BF16)
    cw = cw_ref[...]
    cb = cb_ref[...]
    ngroup = D // GW

    def project(c):
        return (jnp.dot(h, w_ref[c][...], preferred_element_type=F32),
                jnp.dot(h, w_ref[ngroup + c][...], preferred_element_type=F32))

    pending = project(0)
    for c in range(ngroup):
        nxt = project(c + 1) if c + 1 < ngroup else None
        gate, rec = pending
        gg_ref[:, c * GW:(c + 1) * GW] = jax.nn.gelu(
            gate[8:8 + TM], approximate=True).astype(BF16)
        for s2 in range(GW // LANES):
            s = c * (GW // LANES) + s2
            ls = slice(s * LANES, (s + 1) * LANES)
            rec_scr[s] = rec[:, s2 * LANES:(s2 + 1) * LANES]
            taps = [(k - 2, cw[k:k + 1, ls]) for k in range(4)]
            xr_ref[:, ls] = _slab_conv(rec_scr, s, taps, cb[:, ls], pos)
        pending = nxt


def _lru_in_call(xs, mods, layer, gain, w, j, cw, cb):
    return pl.pallas_call(
        _lru_in_kernel,
        grid=(NROWS // TM,),
        in_specs=_halo_specs() + [
            _mods_spec(layer, 0), _mods_spec(layer, 1), _const_spec((1, D)),
            _const_spec((4, D)), _const_spec((1, D)),
        ] + _col_chunk_specs(j, GW, 2 * D // GW),
        out_specs=[_row_spec(), _row_spec()],
        out_shape=[jax.ShapeDtypeStruct((NROWS, D), BF16),
                   jax.ShapeDtypeStruct((NROWS, D), F32)],
        scratch_shapes=[pltpu.VMEM((D // LANES, TM + 16, LANES), F32)],
        compiler_params=_cparams(1),
        name="lru_in_proj",
    )(xs, xs, xs, mods, mods, gain, cw, cb, *([w] * (2 * D // GW)))


def _chunk_index(g, reverse):
    if not reverse:
        return g
    return jnp.where(g < CCH, CCH - 1 - g, NCH + CCH - 1 - g)


def _scan_kernel(xr_ref, lam_ref, wg_ref, ba_ref, bx_ref, *rest, reverse):
    if reverse:
        o_ref, a_scr, b_scr, h_scr = rest
    else:
        hsb_ref, gg_ref, x_ref, wo_ref, g1_ref, o_ref, a_scr, b_scr, h_scr = rest
    g = pl.program_id(0)

    @pl.when(g == 0)
    def _():
        h_scr[...] = jnp.zeros_like(h_scr)

    lam = lam_ref[...]
    half_log_a = (-0.5 * LRU_C) * (jnp.maximum(-lam, 0.0) + jnp.log1p(jnp.exp(-jnp.abs(lam))))
    nslab = D // LANES
    group_slabs = GW // LANES

    def gates(gq):
        cols = slice(gq * GW, (gq + 1) * GW)
        xg = jnp.concatenate([xr_ref[b, :, cols] for b in range(B)], axis=0)
        pre = jnp.dot(xg.astype(BF16), wg_ref[gq], preferred_element_type=F32)
        t_r = jnp.tanh(pre[:, :GW] + ba_ref[:, cols])
        t_i = jnp.tanh(pre[:, GW:] + bx_ref[:, cols])
        log_a = half_log_a[:, cols] * t_r + half_log_a[:, cols]
        a = jnp.exp(log_a)
        y = (-1.0 - a * a) * jnp.tanh(log_a)
        root = jnp.where(y > 0.0, y * lax.rsqrt(y), 0.0)
        hx = 0.5 * xg
        bb = root * (hx * t_i + hx)
        for k2 in range(group_slabs):
            k = gq * group_slabs + k2
            for b in range(B):
                rows = slice(k * B * PITCH + b * PITCH, k * B * PITCH + b * PITCH + TS)
                a_scr[rows, :] = a[b * TS:(b + 1) * TS, k2 * LANES:(k2 + 1) * LANES]
                b_scr[rows, :] = bb[b * TS:(b + 1) * TS, k2 * LANES:(k2 + 1) * LANES]

    def recurrence(gq):
        slabs = range(gq * group_slabs, (gq + 1) * group_slabs)
        hs = {k: h_scr[k] for k in slabs}
        for t in (range(TS - 1, -1, -1) if reverse else range(TS)):
            for k in slabs:
                idx = pl.ds(k * B * PITCH + t, B, stride=PITCH)
                hs[k] = a_scr[idx, :] * hs[k] + b_scr[idx, :]
                b_scr[idx, :] = hs[k]
        for k in slabs:
            h_scr[k] = hs[k]

    def slab_rows(b, k):
        return slice(k * B * PITCH + b * PITCH, k * B * PITCH + b * PITCH + TS)

    gates(0)
    for gq in range(D // GW):
        if gq + 1 < D // GW:
            gates(gq + 1)
        recurrence(gq)

    if reverse:
        for b in range(B):
            for k in range(nslab):
                o_ref[b, :, k * LANES:(k + 1) * LANES] = b_scr[slab_rows(b, k), :].astype(BF16)
    else:
        z = jnp.concatenate([
            jnp.concatenate([
                ((b_scr[slab_rows(b, k), :] + hsb_ref[b, :, k * LANES:(k + 1) * LANES])
                 * gg_ref[b, :, k * LANES:(k + 1) * LANES].astype(F32)).astype(BF16)
                for k in range(nslab)], axis=1)
            for b in range(B)], axis=0)
        proj = jnp.dot(z, wo_ref[...], preferred_element_type=F32)
        is_ctx = g < CCH
        for b in range(B):
            g1 = jnp.where(is_ctx, g1_ref[8:9, :], g1_ref[b:b + 1, :])
            o_ref[b] = x_ref[b] + g1 * proj[b * TS:(b + 1) * TS]


def _scan_call(xr3, lam, wg, ba, bx, reverse, fwd_args=None):
    main = pl.BlockSpec((B, TS, D), lambda g: (0, _chunk_index(g, reverse), 0))
    in_specs = [
        main, _const_spec((1, D)),
        _const_spec((D // GW, GW, 2 * GW)), _const_spec((1, D)), _const_spec((1, D)),
    ]
    args = [xr3, lam, wg, 0.5 * ba, 0.5 * bx]
    if not reverse:
        hsb, gg, xs3, wo, mods, layer = fwd_args
        in_specs += [main, main, main, _const_spec((D, D)), _mods_spec(layer, 2)]
        args += [hsb, gg, xs3, wo, mods]
    slab = pltpu.VMEM((D // LANES * B * PITCH, LANES), F32)
    return pl.pallas_call(
        functools.partial(_scan_kernel, reverse=reverse),
        grid=(NCH,),
        in_specs=in_specs,
        out_specs=main,
        out_shape=jax.ShapeDtypeStruct((B, LT, D), BF16 if reverse else F32),
        scratch_shapes=[slab, slab, pltpu.VMEM((D // LANES, B, LANES), F32)],
        compiler_params=_cparams(1),
        name="lru_scan_bwd" if reverse else "lru_scan_fwd",
    )(*args)


def _gate_weights(wa, wx):
    def bd(w):
        w4 = w.reshape(D // GW, GW // LRU_BLOCK_W, LRU_BLOCK_W, LRU_BLOCK_W)
        eye = jnp.eye(GW // LRU_BLOCK_W, dtype=w.dtype)
        return (w4[:, :, :, None, :] * eye[None, :, None, :, None]).reshape(D // GW, GW, GW)
    return (0.5 * jnp.concatenate([bd(wa), bd(wx)], axis=-1)).astype(BF16)


FSLAB = FC // LANES


def _ffn_kernel(*refs, t):
    nchunk = 2 * D_FF // FC
    xp_ref, *x_refs = refs[:1 + t.main_blocks]
    (xn_ref, sh_ref, sc_ref, g2_ref, gain_ref, cw_ref, cb_ref,
     wd_ref) = refs[1 + t.main_blocks:9 + t.main_blocks]
    wu_ref = refs[9 + t.main_blocks:9 + t.main_blocks + nchunk]
    o_ref, uv_scr, ug_scr = refs[9 + t.main_blocks + nchunk:]
    tm = t.tm
    b, pos = _tile_pos(tm, 0, t)
    _, pos_e = _tile_pos(tm + 16, -8, t)
    is_ctx_e = pos_e < C
    x = jnp.concatenate([r[...] for r in x_refs], axis=0)
    x_e = jnp.concatenate([xp_ref[...], x, xn_ref[...]], axis=0)
    h = _modulate(x_e, gain_ref[...], _mod_rows(sh_ref, b, is_ctx_e),
                  _mod_rows(sc_ref, b, is_ctx_e))
    h = jnp.where(_dead_halo_rows(1, t), 0.0, h).astype(BF16)

    cw = cw_ref[...]
    cb = cb_ref[...]
    def up_project(ci):
        uv = jnp.dot(h, wu_ref[ci][...], preferred_element_type=F32)
        ug = jnp.dot(h, wu_ref[D_FF // FC + ci][...], preferred_element_type=F32)
        base = (ci % 2) * FSLAB
        for s in range(FSLAB):
            uv_scr[base + s] = uv[:, s * LANES:(s + 1) * LANES]
            ug_scr[base + s] = ug[:, s * LANES:(s + 1) * LANES]

    acc = jnp.zeros((tm, D), F32)
    up_project(0)
    for ci in range(D_FF // FC):
        if ci + 1 < D_FF // FC:
            up_project(ci + 1)
        base = (ci % 2) * FSLAB
        acts = []
        for s in range(FSLAB):
            lv = slice(ci * FC + s * LANES, ci * FC + (s + 1) * LANES)
            lg = slice(D_FF + lv.start, D_FF + lv.stop)
            val = _slab_conv(uv_scr, base + s, [(k - 1, cw[k:k + 1, lv]) for k in range(3)],
                             cb[:, lv], pos, t)
            hg = _slab_conv(ug_scr, base + s,
                            [(k - 1, 0.5 * cw[k:k + 1, lg]) for k in range(3)],
                            0.5 * cb[:, lg], pos, t)
            acts.append((val * (hg + hg * jnp.tanh(hg))).astype(BF16))
        acc = acc + jnp.dot(jnp.concatenate(acts, axis=1), wd_ref[ci],
                            preferred_element_type=F32)
    o_ref[...] = x + _mod_rows(g2_ref, b, pos < C) * acc


def _ffn_call(xs, mods, layer, gain, wu, cw, cb, wd, t):
    nchunk = 2 * D_FF // FC

    def layer_spec(shape):
        return pl.BlockSpec((None,) + shape, lambda i: (layer,) + (0,) * len(shape),
                            pipeline_mode=pl.Buffered(1))

    slab = pltpu.VMEM((2 * FSLAB, t.tm + 16, LANES), F32)
    if t is STREAM:
        x_specs, out_rows = _halo_specs(), NROWS
    else:
        def row0(i):
            return (i // t.tpb) * LT + t.base + (i % t.tpb) * t.tm
        half = t.tm // t.main_blocks
        x_specs = (
            [pl.BlockSpec((8, D), lambda i: (row0(i) // 8 - 1, 0))]
            + [pl.BlockSpec((half, D), lambda i, k=k: (row0(i) // half + k, 0))
               for k in range(t.main_blocks)]
            + [pl.BlockSpec((8, D), lambda i: (jnp.minimum((row0(i) + t.tm) // 8,
                                                            NROWS // 8 - 1), 0))])
        out_rows = B * S
    return pl.pallas_call(
        functools.partial(_ffn_kernel, t=t),
        grid=(out_rows // t.tm,),
        in_specs=x_specs + [
            _mods_spec(layer, 3), _mods_spec(layer, 4), _mods_spec(layer, 5),
            _const_spec((1, D)), layer_spec((3, 2 * D_FF)),
            layer_spec((1, 2 * D_FF)), layer_spec((D_FF // FC, FC, D)),
        ] + _col_chunk_specs(layer, FC, nchunk),
        out_specs=pl.BlockSpec((t.tm, D), lambda i: (i, 0)),
        out_shape=jax.ShapeDtypeStruct((out_rows, D), F32),
        scratch_shapes=[slab, slab],
        compiler_params=_cparams(1),
        name="conv_ffn",
    )(*([xs] * (2 + t.main_blocks)), mods, mods, mods, gain, cw, cb, wd, *([wu] * nchunk))


def kernel(x, c, ctx, c_ctx, ada_w, ada_b, norm_mix, norm_ffn, na_w_qkv, na_q_gain, na_k_gain,
           na_rpb, na_w_out, lru_w_in, lru_conv_w, lru_conv_b, lru_ga_w, lru_ga_b, lru_gx_w,
           lru_gx_b, lru_lambda, lru_w_out, ffn_w_up, ffn_conv_w, ffn_conv_b, ffn_w_down):
    xs = (ctx.reshape(B * C, D), x.reshape(B * S, D))
    cc = jnp.concatenate([c, c_ctx[None], jnp.zeros((7, D), F32)], axis=0)
    mods = _mods_call(cc, ada_w, ada_b)

    head_of_lane = jnp.arange(D) // HD
    g1 = (head_of_lane[:, None] == jnp.arange(LANES)[None, :]).astype(BF16)

    w_qkv = na_w_qkv.astype(BF16)
    w_in = lru_w_in.astype(BF16)
    w_up = ffn_w_up.astype(BF16)
    w_down = ffn_w_down.astype(BF16).reshape(DEPTH, D_FF // FC, FC, D)
    conv_b = ffn_conv_b[:, None]

    for i in range(DEPTH):
        j = i // 2
        gain_mix = norm_mix[i][None]
        if i % 2 == 0:
            qg = (jnp.tile(na_q_gain[j], H) * (HD ** -0.5 * LOG2_E))[None]
            kg = jnp.tile(na_k_gain[j], H)[None]
            q, k, v = _qkv_call(xs, mods, i, gain_mix, w_qkv, j, qg, kg, g1)
            o = _attn_call(q, k, v, _bias_table(na_rpb[j] * LOG2_E))
            xs = _proj_call(xs, o, mods, i, na_w_out[j].astype(BF16))
        else:
            gg, xr = _lru_in_call(xs, mods, i, gain_mix, w_in, j,
                                  lru_conv_w[j], lru_conv_b[j][None])
            xr3 = xr.reshape(B, LT, D)
            hsb = _scan_call(xr3, lru_lambda[j, 1][None],
                             _gate_weights(lru_ga_w[j, 1], lru_gx_w[j, 1]),
                             lru_ga_b[j, 1][None], lru_gx_b[j, 1][None], True)
            xs = _scan_call(xr3, lru_lambda[j, 0][None],
                            _gate_weights(lru_ga_w[j, 0], lru_gx_w[j, 0]),
                            lru_ga_b[j, 0][None], lru_gx_b[j, 0][None], False,
                            (hsb, gg.reshape(B, LT, D), xs.reshape(B, LT, D),
                             lru_w_out[j].astype(BF16), mods, i)).reshape(NROWS, D)
        xs = _ffn_call(xs, mods, i, norm_ffn[i][None], w_up, ffn_conv_w, conv_b, w_down,
                       LATENT if i == DEPTH - 1 else STREAM)
    return xs.reshape(B, S, D)
```

```python
import functools
from typing import NamedTuple

import jax
import jax.numpy as jnp
import numpy as np
from jax import lax
from jax.experimental import pallas as pl
from jax.experimental.pallas import tpu as pltpu

F32 = jnp.float32
BF16 = jnp.bfloat16

D = 1024
B = 8
S = 4096
DEPTH = 4
C = 256
LT = C + S
NROWS = B * LT
GRID_W = 64
IMG_ROWS = S // GRID_W
H = 16
HD = 64
NPAIR = H // 2
LANES = 128
WIN_ROWS = 8
WIN_COLS = 16
D_FF = 3 * D
LRU_BLOCK_W = 64
LRU_C = 8.0
EPS = 1e-6
NEG_INF = float("-inf")
LOG2_E = 1.4426950408889634

TM = 544
TPB = LT // TM
FC = 512
QB = 256
NQB = LT // QB
TS = 64
NCH = LT // TS
CCH = C // TS
PITCH = TS + 4
GW = 256
VMEM_LIMIT = 56 * 1024 * 1024

T2_LEFT_DEAD = 16
T2_RIGHT_DEAD = 17
T2_DEAD = 18
T2_ENTRIES = 19


def _cparams(n_axes):
    return pltpu.CompilerParams(
        dimension_semantics=("arbitrary",) * n_axes, vmem_limit_bytes=VMEM_LIMIT)


def _const_spec(shape):
    nd = len(shape)
    return pl.BlockSpec(shape, lambda *_: (0,) * nd, pipeline_mode=pl.Buffered(1))


def _mods_kernel(cc_ref, w_ref, b_ref, o_ref):
    c = cc_ref[...]
    s = c * jax.nn.sigmoid(c)
    o_ref[...] = jnp.dot(s.astype(BF16), w_ref[...].astype(BF16),
                         preferred_element_type=F32) + b_ref[...]


def _mods_call(cc, ada_w, ada_b):
    return pl.pallas_call(
        _mods_kernel,
        grid=(DEPTH, 6),
        in_specs=[
            pl.BlockSpec((16, D), lambda l, k: (0, 0)),
            pl.BlockSpec((None, D, D), lambda l, k: (l, 0, k)),
            pl.BlockSpec((None, None, 1, D), lambda l, k: (l, k, 0, 0)),
        ],
        out_specs=pl.BlockSpec((None, None, 16, D), lambda l, k: (l, k, 0, 0)),
        out_shape=jax.ShapeDtypeStruct((DEPTH, 6, 16, D), F32),
        compiler_params=_cparams(2),
        name="adaln_mods",
    )(cc, ada_w, ada_b.reshape(DEPTH, 6, 1, D))


class Tiling(NamedTuple):
    tm: int
    tpb: int
    base: int
    main_blocks: int


STREAM = Tiling(TM, TPB, 0, 1)
LATENT = Tiling(512, S // 512, C, 2)
WIDE = Tiling(2 * TM, TPB // 2, 0, 1)


def _tile_pos(n, lo, t=STREAM):
    i = pl.program_id(0)
    b = i // t.tpb
    j = i % t.tpb
    pos = t.base + j * t.tm + lo + lax.broadcasted_iota(jnp.int32, (n, 1), 0)
    return b, pos


def _mod_rows(m_ref, b, is_ctx):
    return jnp.where(is_ctx, m_ref[8:9, :], m_ref[pl.ds(b, 1), :])


def _modulate(x, gain, shift, scale):
    y = x * lax.rsqrt(jnp.mean(x * x, axis=-1, keepdims=True) + EPS)
    return (y * gain) * (1.0 + scale) + shift


def _col_chunk_specs(j, width, n):
    return [pl.BlockSpec((None, D, width), lambda *_, c=c: (j, 0, c),
                         pipeline_mode=pl.Buffered(1)) for c in range(n)]


def _mods_spec(layer, kind):
    return pl.BlockSpec((None, None, 16, D), lambda *_: (layer, kind, 0, 0))


def _row_spec(width=D, t=STREAM):
    return pl.BlockSpec((t.tm, width), lambda i: (i, 0))


def _pair_spec(t=STREAM):
    return pl.BlockSpec((NPAIR, t.tm, LANES), lambda i: (0, i, 0))


def _qkv_kernel(x_ref, sh_ref, sc_ref, gain_ref, qg_ref, kg_ref, g1_ref, wq_ref, wk_ref, wv_ref,
                q_ref, k_ref, v_ref, *, t):
    b, pos = _tile_pos(t.tm, 0, t)
    is_ctx = pos < C
    h = _modulate(x_ref[...], gain_ref[...], _mod_rows(sh_ref, b, is_ctx),
                  _mod_rows(sc_ref, b, is_ctx)).astype(BF16)
    head_col = lax.shift_right_logical(lax.broadcasted_iota(jnp.int32, (t.tm, LANES), 1), 6)
    def head_norm(acc, o_ref, gn_ref):
        ss = jnp.dot((acc * acc).astype(BF16), g1_ref[...], preferred_element_type=F32)
        rstd = lax.rsqrt(ss * (1.0 / HD) + EPS)
        for p in range(NPAIR):
            ls = slice(p * LANES, (p + 1) * LANES)
            rb = jnp.take_along_axis(rstd, head_col + 2 * p, axis=1, mode="promise_in_bounds")
            o_ref[p] = (acc[:, ls] * rb * gn_ref[:, ls]).astype(BF16)

    head_norm(jnp.dot(h, wq_ref[...], preferred_element_type=F32), q_ref, qg_ref)
    head_norm(jnp.dot(h, wk_ref[...], preferred_element_type=F32), k_ref, kg_ref)
    v = jnp.dot(h, wv_ref[...], preferred_element_type=F32).astype(BF16)
    first_head = lax.broadcasted_iota(jnp.int32, (1, LANES), 1) < HD
    one = jnp.ones((), BF16)
    for p in range(NPAIR):
        vp = v[:, p * LANES:(p + 1) * LANES]
        v_ref[0, p] = jnp.where(first_head, vp, one)
        v_ref[1, p] = jnp.where(first_head, one, vp)


def _qkv_call(xs, mods, layer, gain, w, j, qg, kg, g1):
    t = WIDE
    out = jax.ShapeDtypeStruct((NPAIR, NROWS, LANES), BF16)
    return pl.pallas_call(
        functools.partial(_qkv_kernel, t=t),
        grid=(NROWS // t.tm,),
        in_specs=[
            _row_spec(t=t), _mods_spec(layer, 0), _mods_spec(layer, 1), _const_spec((1, D)),
            _const_spec((1, D)), _const_spec((1, D)), _const_spec((D, LANES)),
        ] + _col_chunk_specs(j, D, 3),
        out_specs=[_pair_spec(t), _pair_spec(t),
                   pl.BlockSpec((2, NPAIR, t.tm, LANES), lambda i: (0, 0, i, 0))],
        out_shape=[out, out, jax.ShapeDtypeStruct((2, NPAIR, NROWS, LANES), BF16)],
        compiler_params=_cparams(1),
        name="na_qkv",
    )(xs, mods, mods, gain, qg, kg, g1, w, w, w)


def _dot_t(a, b):
    return lax.dot_general(a, b, (((1,), (1,)), ((), ())), preferred_element_type=F32)


def _bias_tile_table():
    tab = np.zeros((NQB, 24), np.int32)
    for jj in range(NQB):
        rg = jj - 1
        win0 = min(max(rg, 1), NQB - 3) - 1
        off = 4 * win0 - 4 * rg + (WIN_ROWS - 1)
        for i in range(4):
            rs = min(max(4 * rg + i - WIN_ROWS // 2, 0), IMG_ROWS - WIN_ROWS)
            for pu in range(6):
                kl = 4 * win0 + 2 * pu
                lv = jj > 0 and rs <= kl < rs + WIN_ROWS
                rv = jj > 0 and rs <= kl + 1 < rs + WIN_ROWS
                d = 2 * pu - i + off
                if lv and rv:
                    assert 0 <= d <= 2 * WIN_ROWS - 3
                    tab[jj, i * 6 + pu] = d + 1
                elif rv:
                    assert d + 1 == WIN_ROWS // 2 - 1
                    tab[jj, i * 6 + pu] = T2_LEFT_DEAD
                elif lv:
                    assert d == WIN_ROWS // 2 - 1 + WIN_ROWS - 1
                    tab[jj, i * 6 + pu] = T2_RIGHT_DEAD
                else:
                    tab[jj, i * 6 + pu] = T2_DEAD
    return tab


def _attn_kernel(idx_ref, q_ref, kc_ref, vc_ref, k0_ref, k1_ref, k2_ref, v0_ref, v1_ref, v2_ref,
                 t2_ref, o_ref):
    jj = pl.program_id(1)
    tile_idx = [[idx_ref[jj, i * 6 + pu] for pu in range(6)] for i in range(4)]
    lane = lax.broadcasted_iota(jnp.int32, (1, LANES), 1)
    k_refs = (k0_ref, k1_ref, k2_ref)
    v_refs = (v0_ref, v1_ref, v2_ref)

    def scores(head, n_lat):
        p, hh = divmod(head, 2)
        qp = q_ref[p]
        sel = (lane < HD) if hh == 0 else (lane >= HD)
        qm = jnp.where(sel, qp, jnp.zeros_like(qp))
        s_all = [_dot_t(qm, kc_ref[p])]
        for j in range(n_lat):
            bias = jnp.concatenate(
                [jnp.concatenate([t2_ref[head, tile_idx[i][2 * j + cp]] for cp in range(2)],
                                 axis=1) for i in range(4)], axis=0)
            s_all.append(_dot_t(qm, k_refs[j][p]) + bias)
        mm = s_all[0]
        if n_lat:
            mm = jnp.maximum(jnp.maximum(mm, s_all[1]), jnp.maximum(s_all[2], s_all[3]))
        m = jnp.max(jnp.maximum(mm[:, :LANES], mm[:, LANES:]), axis=-1, keepdims=True)
        return s_all, m

    def weighted_values(head, s_all, m):
        p, hh = divmod(head, 2)
        acc = jnp.dot(jnp.exp2(s_all[0] - m).astype(BF16), vc_ref[hh, p],
                      preferred_element_type=F32)
        for j in range(len(s_all) - 1):
            acc = acc + jnp.dot(jnp.exp2(s_all[j + 1] - m).astype(BF16), v_refs[j][hh, p],
                                preferred_element_type=F32)
        return acc / pltpu.roll(acc, HD, axis=1)

    def all_heads(n_lat):
        pending = scores(0, n_lat)
        outs = []
        for head in range(H):
            nxt = scores(head + 1, n_lat) if head + 1 < H else None
            outs.append(weighted_values(head, *pending))
            pending = nxt
            if head % 2 == 1:
                o_ref[head // 2] = jnp.where(lane < HD, outs[0], outs[1]).astype(BF16)
                outs = []

    @pl.when(jj == 0)
    def _():
        all_heads(0)

    @pl.when(jj > 0)
    def _():
        all_heads(3)


def _attn_call(q, k, v, t2):
    def blk(fn):
        return pl.BlockSpec((NPAIR, QB, LANES), lambda b, jj, idx: (0, fn(b, jj), 0))

    def vblk(fn):
        return pl.BlockSpec((2, NPAIR, QB, LANES), lambda b, jj, idx: (0, 0, fn(b, jj), 0))

    def win(o):
        return lambda b, jj: b * NQB + jnp.clip(jj - 1, 1, NQB - 3) + o

    return pl.pallas_call(
        _attn_kernel,
        grid_spec=pltpu.PrefetchScalarGridSpec(
            num_scalar_prefetch=1,
            grid=(B, NQB),
            in_specs=[
                blk(lambda b, jj: b * NQB + jj),
                blk(lambda b, jj: b * NQB), vblk(lambda b, jj: b * NQB),
                blk(win(0)), blk(win(1)), blk(win(2)),
                vblk(win(0)), vblk(win(1)), vblk(win(2)),
                _const_spec((H, T2_ENTRIES, GRID_W, LANES)),
            ],
            out_specs=blk(lambda b, jj: b * NQB + jj),
        ),
        out_shape=jax.ShapeDtypeStruct((NPAIR, NROWS, LANES), BF16),
        compiler_params=_cparams(2),
        name="na_attention",
    )(jnp.asarray(_bias_tile_table()), q, k, v, k, k, k, v, v, v, t2)


def _proj_kernel(x_ref, y_ref, gate_ref, w_ref, o_ref, *, t):
    b, pos = _tile_pos(t.tm, 0, t)
    y = jnp.concatenate([y_ref[p] for p in range(NPAIR)], axis=1)
    o_ref[...] = x_ref[...] + _mod_rows(gate_ref, b, pos < C) * jnp.dot(
        y, w_ref[...], preferred_element_type=F32)


def _proj_call(xs, y, mods, layer, w):
    t = WIDE
    return pl.pallas_call(
        functools.partial(_proj_kernel, t=t),
        grid=(NROWS // t.tm,),
        in_specs=[_row_spec(t=t), _pair_spec(t), _mods_spec(layer, 2), _const_spec((D, D))],
        out_specs=_row_spec(t=t),
        out_shape=jax.ShapeDtypeStruct((NROWS, D), F32),
        compiler_params=_cparams(1),
        name="na_out_proj",
    )(xs, y, mods, w)


def _bias_table(rpb):
    qc = jnp.arange(GRID_W)[:, None]
    kc = jnp.arange(GRID_W)[None, :]
    ws = jnp.clip(qc - WIN_COLS // 2, 0, GRID_W - WIN_COLS)
    col_ok = (kc >= ws) & (kc < ws + WIN_COLS)
    onehot = ((kc - qc + (WIN_COLS - 1))[None] == jnp.arange(2 * WIN_COLS - 1)[:, None, None])
    t = jnp.einsum("hdc,cqk->hdqk", rpb, onehot.astype(F32), precision=lax.Precision.HIGHEST)
    t = jnp.where(col_ok[None, None], t, NEG_INF)
    dead = jnp.full((H, 1, GRID_W, GRID_W), NEG_INF, F32)
    t_ext = jnp.concatenate([dead, t, dead], axis=1)
    pairs = jnp.concatenate([t_ext[:, :-1], t_ext[:, 1:]], axis=-1)
    lo, hi = WIN_ROWS // 2 - 1, WIN_ROWS // 2 - 1 + WIN_ROWS - 1
    left_dead = jnp.concatenate([dead, t[:, lo:lo + 1]], axis=-1)
    right_dead = jnp.concatenate([t[:, hi:hi + 1], dead], axis=-1)
    all_dead = jnp.concatenate([dead, dead], axis=-1)
    return jnp.concatenate([pairs, left_dead, right_dead, all_dead], axis=1)


assert 8 < C % TM < TM - 8
BND = C % TM
BND_LO = BND - 8


def _halo_specs():
    t8 = TM // 8
    return [
        pl.BlockSpec((8, D), lambda i: (jnp.maximum(i * t8 - 1, 0), 0)),
        _row_spec(),
        pl.BlockSpec((8, D), lambda i: (jnp.minimum((i + 1) * t8, NROWS // 8 - 1), 0)),
    ]


def _dead_halo_rows(reach_back, t=STREAM):
    start = t.base + (pl.program_id(0) % t.tpb) * t.tm
    e = lax.broadcasted_iota(jnp.int32, (t.tm + 16, 1), 0)
    at_start = (start == 0) | (start == C)
    at_end = (start + t.tm == C) | (start + t.tm == LT)
    return ((e >= 8 - reach_back) & (e < 8) & at_start) | ((e == t.tm + 8) & at_end)


def _slab_conv(u_scr, s, taps, cb, pos, t=STREAM):
    def seg(lo, n, masked):
        y = cb
        for off, w in taps:
            u = u_scr[s, pl.ds(lo + 8 + off, n), :]
            if masked and off != 0:
                p = pos[lo:lo + n]
                crosses = (p >= C) & (p < C - off) if off < 0 else (p < C) & (p >= C - off)
                u = jnp.where(crosses, 0.0, u)
            y = y + u * w
        return y
    if t is LATENT:
        return seg(0, t.tm, False)
    return jnp.concatenate([seg(0, BND_LO, False), seg(BND_LO, 16, True),
                            seg(BND_LO + 16, TM - BND_LO - 16, False)], axis=0)


def _lru_in_kernel(xp_ref, x_ref, xn_ref, sh_ref, sc_ref, gain_ref, cw_ref, cb_ref, *rest):
    w_ref = rest[:2 * D // GW]
    gg_ref, xr_ref, rec_scr = rest[2 * D // GW:]
    b, pos = _tile_pos(TM, 0)
    _, pos_e = _tile_pos(TM + 16, -8)
    is_ctx_e = pos_e < C
    x_e = jnp.concatenate([xp_ref[...], x_ref[...], xn_ref[...]], axis=0)
    h = _modulate(x_e, gain_ref[...], _mod_rows(sh_ref, b, is_ctx_e),
                  _mod_rows(sc_ref, b, is_ctx_e))
    h = jnp.where(_dead_halo_rows(2), 0.0, h).astype(BF16)
    cw = cw_ref[...]
    cb = cb_ref[...]
    ngroup = D // GW

    def project(c):
        return (jnp.dot(h, w_ref[c][...], preferred_element_type=F32),
                jnp.dot(h, w_ref[ngroup + c][...], preferred_element_type=F32))

    pending = project(0)
    for c in range(ngroup):
        nxt = project(c + 1) if c + 1 < ngroup else None
        gate, rec = pending
        gg_ref[:, c * GW:(c + 1) * GW] = jax.nn.gelu(
            gate[8:8 + TM], approximate=True).astype(BF16)
        for s2 in range(GW // LANES):
            s = c * (GW // LANES) + s2
            ls = slice(s * LANES, (s + 1) * LANES)
            rec_scr[s] = rec[:, s2 * LANES:(s2 + 1) * LANES]
            taps = [(k - 2, cw[k:k + 1, ls]) for k in range(4)]
            xr_ref[:, ls] = _slab_conv(rec_scr, s, taps, cb[:, ls], pos)
        pending = nxt


def _lru_in_call(xs, mods, layer, gain, w, j, cw, cb):
    return pl.pallas_call(
        _lru_in_kernel,
        grid=(NROWS // TM,),
        in_specs=_halo_specs() + [
            _mods_spec(layer, 0), _mods_spec(layer, 1), _const_spec((1, D)),
            _const_spec((4, D)), _const_spec((1, D)),
        ] + _col_chunk_specs(j, GW, 2 * D // GW),
        out_specs=[_row_spec(), _row_spec()],
        out_shape=[jax.ShapeDtypeStruct((NROWS, D), BF16),
                   jax.ShapeDtypeStruct((NROWS, D), F32)],
        scratch_shapes=[pltpu.VMEM((D // LANES, TM + 16, LANES), F32)],
        compiler_params=_cparams(1),
        name="lru_in_proj",
    )(xs, xs, xs, mods, mods, gain, cw, cb, *([w] * (2 * D // GW)))


def _chunk_index(g, reverse):
    if not reverse:
        return g
    return jnp.where(g < CCH, CCH - 1 - g, NCH + CCH - 1 - g)


def _scan_kernel(xr_ref, lam_ref, wg_ref, ba_ref, bx_ref, *rest, reverse):
    if reverse:
        o_ref, a_scr, b_scr, h_scr = rest
    else:
        hsb_ref, gg_ref, x_ref, wo_ref, g1_ref, o_ref, a_scr, b_scr, h_scr = rest
    g = pl.program_id(0)

    @pl.when(g == 0)
    def _():
        h_scr[...] = jnp.zeros_like(h_scr)

    lam = lam_ref[...]
    half_log_a = (-0.5 * LRU_C) * (jnp.maximum(-lam, 0.0) + jnp.log1p(jnp.exp(-jnp.abs(lam))))
    nslab = D // LANES
    group_slabs = GW // LANES

    def gates(gq):
        cols = slice(gq * GW, (gq + 1) * GW)
        xg = jnp.concatenate([xr_ref[b, :, cols] for b in range(B)], axis=0)
        pre = jnp.dot(xg.astype(BF16), wg_ref[gq], preferred_element_type=F32)
        t_r = jnp.tanh(pre[:, :GW] + ba_ref[:, cols])
        t_i = jnp.tanh(pre[:, GW:] + bx_ref[:, cols])
        log_a = half_log_a[:, cols] * t_r + half_log_a[:, cols]
        a = jnp.exp(log_a)
        y = (-1.0 - a * a) * jnp.tanh(log_a)
        root = jnp.where(y > 0.0, y * lax.rsqrt(y), 0.0)
        hx = 0.5 * xg
        bb = root * (hx * t_i + hx)
        for k2 in range(group_slabs):
            k = gq * group_slabs + k2
            for b in range(B):
                rows = slice(k * B * PITCH + b * PITCH, k * B * PITCH + b * PITCH + TS)
                a_scr[rows, :] = a[b * TS:(b + 1) * TS, k2 * LANES:(k2 + 1) * LANES]
                b_scr[rows, :] = bb[b * TS:(b + 1) * TS, k2 * LANES:(k2 + 1) * LANES]

    def recurrence(gq):
        slabs = range(gq * group_slabs, (gq + 1) * group_slabs)
        hs = {k: h_scr[k] for k in slabs}
        for t in (range(TS - 1, -1, -1) if reverse else range(TS)):
            for k in slabs:
                idx = pl.ds(k * B * PITCH + t, B, stride=PITCH)
                hs[k] = a_scr[idx, :] * hs[k] + b_scr[idx, :]
                b_scr[idx, :] = hs[k]
        for k in slabs:
            h_scr[k] = hs[k]

    def slab_rows(b, k):
        return slice(k * B * PITCH + b * PITCH, k * B * PITCH + b * PITCH + TS)

    gates(0)
    for gq in range(D // GW):
        if gq + 1 < D // GW:
            gates(gq + 1)
        recurrence(gq)

    if reverse:
        for b in range(B):
            for k in range(nslab):
                o_ref[b, :, k * LANES:(k + 1) * LANES] = b_scr[slab_rows(b, k), :].astype(BF16)
    else:
        z = jnp.concatenate([
            jnp.concatenate([
                ((b_scr[slab_rows(b, k), :] + hsb_ref[b, :, k * LANES:(k + 1) * LANES])
                 * gg_ref[b, :, k * LANES:(k + 1) * LANES].astype(F32)).astype(BF16)
                for k in range(nslab)], axis=1)
            for b in range(B)], axis=0)
        proj = jnp.dot(z, wo_ref[...], preferred_element_type=F32)
        is_ctx = g < CCH
        for b in range(B):
            g1 = jnp.where(is_ctx, g1_ref[8:9, :], g1_ref[b:b + 1, :])
            o_ref[b] = x_ref[b] + g1 * proj[b * TS:(b + 1) * TS]


def _scan_call(xr3, lam, wg, ba, bx, reverse, fwd_args=None):
    main = pl.BlockSpec((B, TS, D), lambda g: (0, _chunk_index(g, reverse), 0))
    in_specs = [
        main, _const_spec((1, D)),
        _const_spec((D // GW, GW, 2 * GW)), _const_spec((1, D)), _const_spec((1, D)),
    ]
    args = [xr3, lam, wg, 0.5 * ba, 0.5 * bx]
    if not reverse:
        hsb, gg, xs3, wo, mods, layer = fwd_args
        in_specs += [main, main, main, _const_spec((D, D)), _mods_spec(layer, 2)]
        args += [hsb, gg, xs3, wo, mods]
    slab = pltpu.VMEM((D // LANES * B * PITCH, LANES), F32)
    return pl.pallas_call(
        functools.partial(_scan_kernel, reverse=reverse),
        grid=(NCH,),
        in_specs=in_specs,
        out_specs=main,
        out_shape=jax.ShapeDtypeStruct((B, LT, D), BF16 if reverse else F32),
        scratch_shapes=[slab, slab, pltpu.VMEM((D // LANES, B, LANES), F32)],
        compiler_params=_cparams(1),
        name="lru_scan_bwd" if reverse else "lru_scan_fwd",
    )(*args)


def _gate_weights(wa, wx):
    def bd(w):
        w4 = w.reshape(D // GW, GW // LRU_BLOCK_W, LRU_BLOCK_W, LRU_BLOCK_W)
        eye = jnp.eye(GW // LRU_BLOCK_W, dtype=w.dtype)
        return (w4[:, :, :, None, :] * eye[None, :, None, :, None]).reshape(D // GW, GW, GW)
    return (0.5 * jnp.concatenate([bd(wa), bd(wx)], axis=-1)).astype(BF16)


FSLAB = FC // LANES


def _ffn_kernel(*refs, t):
    nchunk = 2 * D_FF // FC
    xp_ref, *x_refs = refs[:1 + t.main_blocks]
    (xn_ref, sh_ref, sc_ref, g2_ref, gain_ref, cw_ref, cb_ref,
     wd_ref) = refs[1 + t.main_blocks:9 + t.main_blocks]
    wu_ref = refs[9 + t.main_blocks:9 + t.main_blocks + nchunk]
    o_ref, uv_scr, ug_scr = refs[9 + t.main_blocks + nchunk:]
    tm = t.tm
    b, pos = _tile_pos(tm, 0, t)
    _, pos_e = _tile_pos(tm + 16, -8, t)
    is_ctx_e = pos_e < C
    x = jnp.concatenate([r[...] for r in x_refs], axis=0)
    x_e = jnp.concatenate([xp_ref[...], x, xn_ref[...]], axis=0)
    h = _modulate(x_e, gain_ref[...], _mod_rows(sh_ref, b, is_ctx_e),
                  _mod_rows(sc_ref, b, is_ctx_e))
    h = jnp.where(_dead_halo_rows(1, t), 0.0, h).astype(BF16)

    cw = cw_ref[...]
    cb = cb_ref[...]
    def up_project(ci):
        uv = jnp.dot(h, wu_ref[ci][...], preferred_element_type=F32)
        ug = jnp.dot(h, wu_ref[D_FF // FC + ci][...], preferred_element_type=F32)
        base = (ci % 2) * FSLAB
        for s in range(FSLAB):
            uv_scr[base + s] = uv[:, s * LANES:(s + 1) * LANES]
            ug_scr[base + s] = ug[:, s * LANES:(s + 1) * LANES]

    acc = jnp.zeros((tm, D), F32)
    up_project(0)
    for ci in range(D_FF // FC):
        if ci + 1 < D_FF // FC:
            up_project(ci + 1)
        base = (ci % 2) * FSLAB
        acts = []
        for s in range(FSLAB):
            lv = slice(ci * FC + s * LANES, ci * FC + (s + 1) * LANES)
            lg = slice(D_FF + lv.start, D_FF + lv.stop)
            val = _slab_conv(uv_scr, base + s, [(k - 1, cw[k:k + 1, lv]) for k in range(3)],
                             cb[:, lv], pos, t)
            hg = _slab_conv(ug_scr, base + s,
                            [(k - 1, 0.5 * cw[k:k + 1, lg]) for k in range(3)],
                            0.5 * cb[:, lg], pos, t)
            acts.append((val * (hg + hg * jnp.tanh(hg))).astype(BF16))
        acc = acc + jnp.dot(jnp.concatenate(acts, axis=1), wd_ref[ci],
                            preferred_element_type=F32)
    o_ref[...] = x + _mod_rows(g2_ref, b, pos < C) * acc


def _ffn_call(xs, mods, layer, gain, wu, cw, cb, wd, t):
    nchunk = 2 * D_FF // FC

    def layer_spec(shape):
        return pl.BlockSpec((None,) + shape, lambda i: (layer,) + (0,) * len(shape),
                            pipeline_mode=pl.Buffered(1))

    slab = pltpu.VMEM((2 * FSLAB, t.tm + 16, LANES), F32)
    if t is STREAM:
        x_specs, out_rows = _halo_specs(), NROWS
    else:
        def row0(i):
            return (i // t.tpb) * LT + t.base + (i % t.tpb) * t.tm
        half = t.tm // t.main_blocks
        x_specs = (
            [pl.BlockSpec((8, D), lambda i: (row0(i) // 8 - 1, 0))]
            + [pl.BlockSpec((half, D), lambda i, k=k: (row0(i) // half + k, 0))
               for k in range(t.main_blocks)]
            + [pl.BlockSpec((8, D), lambda i: (jnp.minimum((row0(i) + t.tm) // 8,
                                                            NROWS // 8 - 1), 0))])
        out_rows = B * S
    return pl.pallas_call(
        functools.partial(_ffn_kernel, t=t),
        grid=(out_rows // t.tm,),
        in_specs=x_specs + [
            _mods_spec(layer, 3), _mods_spec(layer, 4), _mods_spec(layer, 5),
            _const_spec((1, D)), layer_spec((3, 2 * D_FF)),
            layer_spec((1, 2 * D_FF)), layer_spec((D_FF // FC, FC, D)),
        ] + _col_chunk_specs(layer, FC, nchunk),
        out_specs=pl.BlockSpec((t.tm, D), lambda i: (i, 0)),
        out_shape=jax.ShapeDtypeStruct((out_rows, D), F32),
        scratch_shapes=[slab, slab],
        compiler_params=_cparams(1),
        name="conv_ffn",
    )(*([xs] * (2 + t.main_blocks)), mods, mods, mods, gain, cw, cb, wd, *([wu] * nchunk))


def kernel(x, c, ctx, c_ctx, ada_w, ada_b, norm_mix, norm_ffn, na_w_qkv, na_q_gain, na_k_gain,
           na_rpb, na_w_out, lru_w_in, lru_conv_w, lru_conv_b, lru_ga_w, lru_ga_b, lru_gx_w,
           lru_gx_b, lru_lambda, lru_w_out, ffn_w_up, ffn_conv_w, ffn_conv_b, ffn_w_down):
    xs = jnp.concatenate([ctx, x], axis=1).reshape(NROWS, D)
    cc = jnp.concatenate([c, c_ctx[None], jnp.zeros((7, D), F32)], axis=0)
    mods = _mods_call(cc, ada_w, ada_b)

    head_of_lane = jnp.arange(D) // HD
    g1 = (head_of_lane[:, None] == jnp.arange(LANES)[None, :]).astype(BF16)

    w_qkv = na_w_qkv.astype(BF16)
    w_in = lru_w_in.astype(BF16)
    w_up = ffn_w_up.astype(BF16)
    w_down = ffn_w_down.astype(BF16).reshape(DEPTH, D_FF // FC, FC, D)
    conv_b = ffn_conv_b[:, None]

    for i in range(DEPTH):
        j = i // 2
        gain_mix = norm_mix[i][None]
        if i % 2 == 0:
            qg = (jnp.tile(na_q_gain[j], H) * (HD ** -0.5 * LOG2_E))[None]
            kg = jnp.tile(na_k_gain[j], H)[None]
            q, k, v = _qkv_call(xs, mods, i, gain_mix, w_qkv, j, qg, kg, g1)
            o = _attn_call(q, k, v, _bias_table(na_rpb[j] * LOG2_E))
            xs = _proj_call(xs, o, mods, i, na_w_out[j].astype(BF16))
        else:
            gg, xr = _lru_in_call(xs, mods, i, gain_mix, w_in, j,
                                  lru_conv_w[j], lru_conv_b[j][None])
            xr3 = xr.reshape(B, LT, D)
            hsb = _scan_call(xr3, lru_lambda[j, 1][None],
                             _gate_weights(lru_ga_w[j, 1], lru_gx_w[j, 1]),
                             lru_ga_b[j, 1][None], lru_gx_b[j, 1][None], True)
            xs = _scan_call(xr3, lru_lambda[j, 0][None],
                            _gate_weights(lru_ga_w[j, 0], lru_gx_w[j, 0]),
                            lru_ga_b[j, 0][None], lru_gx_b[j, 0][None], False,
                            (hsb, gg.reshape(B, LT, D), xs.reshape(B, LT, D),
                             lru_w_out[j].astype(BF16), mods, i)).reshape(NROWS, D)
        xs = _ffn_call(xs, mods, i, norm_ffn[i][None], w_up, ffn_conv_w, conv_b, w_down,
                       LATENT if i == DEPTH - 1 else STREAM)
    return xs.reshape(B, S, D)
```

```python
import functools
from typing import NamedTuple

import jax
import jax.numpy as jnp
import numpy as np
from jax import lax
from jax.experimental import pallas as pl
from jax.experimental.pallas import tpu as pltpu

F32 = jnp.float32
BF16 = jnp.bfloat16

D = 1024
B = 8
S = 4096
DEPTH = 4
C = 256
LT = C + S
NROWS = B * LT
GRID_W = 64
IMG_ROWS = S // GRID_W
H = 16
HD = 64
NPAIR = H // 2
LANES = 128
WIN_ROWS = 8
WIN_COLS = 16
D_FF = 3 * D
LRU_BLOCK_W = 64
LRU_C = 8.0
EPS = 1e-6
NEG_INF = float("-inf")
LOG2_E = 1.4426950408889634

TM = 544
TPB = LT // TM
FC = 512
QB = 256
NQB = LT // QB
TS = 64
NCH = LT // TS
CCH = C // TS
PITCH = TS + 4
GW = 256
VMEM_LIMIT = 56 * 1024 * 1024

T2_LEFT_DEAD = 16
T2_RIGHT_DEAD = 17
T2_DEAD = 18
T2_ENTRIES = 19


def _cparams(n_axes):
    return pltpu.CompilerParams(
        dimension_semantics=("arbitrary",) * n_axes, vmem_limit_bytes=VMEM_LIMIT)


def _const_spec(shape):
    nd = len(shape)
    return pl.BlockSpec(shape, lambda *_: (0,) * nd, pipeline_mode=pl.Buffered(1))


def _mods_kernel(cc_ref, w_ref, b_ref, o_ref):
    c = cc_ref[...]
    s = c * jax.nn.sigmoid(c)
    o_ref[...] = jnp.dot(s.astype(BF16), w_ref[...].astype(BF16),
                         preferred_element_type=F32) + b_ref[...]


def _mods_call(cc, ada_w, ada_b):
    return pl.pallas_call(
        _mods_kernel,
        grid=(DEPTH, 6),
        in_specs=[
            pl.BlockSpec((16, D), lambda l, k: (0, 0)),
            pl.BlockSpec((None, D, D), lambda l, k: (l, 0, k)),
            pl.BlockSpec((None, None, 1, D), lambda l, k: (l, k, 0, 0)),
        ],
        out_specs=pl.BlockSpec((None, None, 16, D), lambda l, k: (l, k, 0, 0)),
        out_shape=jax.ShapeDtypeStruct((DEPTH, 6, 16, D), F32),
        compiler_params=_cparams(2),
        name="adaln_mods",
    )(cc, ada_w, ada_b.reshape(DEPTH, 6, 1, D))


class Tiling(NamedTuple):
    tm: int
    tpb: int
    base: int
    main_blocks: int


STREAM = Tiling(TM, TPB, 0, 1)
LATENT = Tiling(512, S // 512, C, 2)
WIDE = Tiling(2 * TM, TPB // 2, 0, 1)


def _tile_pos(n, lo, t=STREAM):
    i = pl.program_id(0)
    b = i // t.tpb
    j = i % t.tpb
    pos = t.base + j * t.tm + lo + lax.broadcasted_iota(jnp.int32, (n, 1), 0)
    return b, pos


def _mod_rows(m_ref, b, is_ctx):
    return jnp.where(is_ctx, m_ref[8:9, :], m_ref[pl.ds(b, 1), :])


def _modulate(x, gain, shift, scale):
    y = x * lax.rsqrt(jnp.mean(x * x, axis=-1, keepdims=True) + EPS)
    return (y * gain) * (1.0 + scale) + shift


def _col_chunk_specs(j, width, n):
    return [pl.BlockSpec((None, D, width), lambda *_, c=c: (j, 0, c),
                         pipeline_mode=pl.Buffered(1)) for c in range(n)]


def _mods_spec(layer, kind):
    return pl.BlockSpec((None, None, 16, D), lambda *_: (layer, kind, 0, 0))


def _row_spec(width=D, t=STREAM):
    return pl.BlockSpec((t.tm, width), lambda i: (i, 0))


def _pair_spec(t=STREAM):
    return pl.BlockSpec((NPAIR, t.tm, LANES), lambda i: (0, i, 0))


def _qkv_kernel(x_ref, sh_ref, sc_ref, gain_ref, qg_ref, kg_ref, g1_ref, wq_ref, wk_ref, wv_ref,
                q_ref, k_ref, v_ref, *, t):
    b, pos = _tile_pos(t.tm, 0, t)
    is_ctx = pos < C
    h = _modulate(x_ref[...], gain_ref[...], _mod_rows(sh_ref, b, is_ctx),
                  _mod_rows(sc_ref, b, is_ctx)).astype(BF16)
    head_col = lax.shift_right_logical(lax.broadcasted_iota(jnp.int32, (t.tm, LANES), 1), 6)
    def head_norm(acc, o_ref, gn_ref):
        ss = jnp.dot((acc * acc).astype(BF16), g1_ref[...], preferred_element_type=F32)
        rstd = lax.rsqrt(ss * (1.0 / HD) + EPS)
        for p in range(NPAIR):
            ls = slice(p * LANES, (p + 1) * LANES)
            rb = jnp.take_along_axis(rstd, head_col + 2 * p, axis=1, mode="promise_in_bounds")
            o_ref[p] = (acc[:, ls] * rb * gn_ref[:, ls]).astype(BF16)

    head_norm(jnp.dot(h, wq_ref[...], preferred_element_type=F32), q_ref, qg_ref)
    head_norm(jnp.dot(h, wk_ref[...], preferred_element_type=F32), k_ref, kg_ref)
    v = jnp.dot(h, wv_ref[...], preferred_element_type=F32).astype(BF16)
    first_head = lax.broadcasted_iota(jnp.int32, (1, LANES), 1) < HD
    one = jnp.ones((), BF16)
    for p in range(NPAIR):
        vp = v[:, p * LANES:(p + 1) * LANES]
        v_ref[0, p] = jnp.where(first_head, vp, one)
        v_ref[1, p] = jnp.where(first_head, one, vp)


def _qkv_call(xs, mods, layer, gain, w, j, qg, kg, g1):
    t = WIDE
    out = jax.ShapeDtypeStruct((NPAIR, NROWS, LANES), BF16)
    return pl.pallas_call(
        functools.partial(_qkv_kernel, t=t),
        grid=(NROWS // t.tm,),
        in_specs=[
            _row_spec(t=t), _mods_spec(layer, 0), _mods_spec(layer, 1), _const_spec((1, D)),
            _const_spec((1, D)), _const_spec((1, D)), _const_spec((D, LANES)),
        ] + _col_chunk_specs(j, D, 3),
        out_specs=[_pair_spec(t), _pair_spec(t),
                   pl.BlockSpec((2, NPAIR, t.tm, LANES), lambda i: (0, 0, i, 0))],
        out_shape=[out, out, jax.ShapeDtypeStruct((2, NPAIR, NROWS, LANES), BF16)],
        compiler_params=_cparams(1),
        name="na_qkv",
    )(xs, mods, mods, gain, qg, kg, g1, w, w, w)


def _dot_t(a, b):
    return lax.dot_general(a, b, (((1,), (1,)), ((), ())), preferred_element_type=F32)


def _bias_tile_table():
    tab = np.zeros((NQB, 24), np.int32)
    for jj in range(NQB):
        rg = jj - 1
        win0 = min(max(rg, 1), NQB - 3) - 1
        off = 4 * win0 - 4 * rg + (WIN_ROWS - 1)
        for i in range(4):
            rs = min(max(4 * rg + i - WIN_ROWS // 2, 0), IMG_ROWS - WIN_ROWS)
            for pu in range(6):
                kl = 4 * win0 + 2 * pu
                lv = jj > 0 and rs <= kl < rs + WIN_ROWS
                rv = jj > 0 and rs <= kl + 1 < rs + WIN_ROWS
                d = 2 * pu - i + off
                if lv and rv:
                    assert 0 <= d <= 2 * WIN_ROWS - 3
                    tab[jj, i * 6 + pu] = d + 1
                elif rv:
                    assert d + 1 == WIN_ROWS // 2 - 1
                    tab[jj, i * 6 + pu] = T2_LEFT_DEAD
                elif lv:
                    assert d == WIN_ROWS // 2 - 1 + WIN_ROWS - 1
                    tab[jj, i * 6 + pu] = T2_RIGHT_DEAD
                else:
                    tab[jj, i * 6 + pu] = T2_DEAD
    return tab


def _attn_kernel(idx_ref, q_ref, kc_ref, vc_ref, k0_ref, k1_ref, k2_ref, v0_ref, v1_ref, v2_ref,
                 t2_ref, o_ref):
    jj = pl.program_id(1)
    tile_idx = [[idx_ref[jj, i * 6 + pu] for pu in range(6)] for i in range(4)]
    lane = lax.broadcasted_iota(jnp.int32, (1, LANES), 1)
    k_refs = (k0_ref, k1_ref, k2_ref)
    v_refs = (v0_ref, v1_ref, v2_ref)

    def scores(head, n_lat):
        p, hh = divmod(head, 2)
        qp = q_ref[p]
        sel = (lane < HD) if hh == 0 else (lane >= HD)
        qm = jnp.where(sel, qp, jnp.zeros_like(qp))
        s_all = [_dot_t(qm, kc_ref[p])]
        for j in range(n_lat):
            bias = jnp.concatenate(
                [jnp.concatenate([t2_ref[head, tile_idx[i][2 * j + cp]] for cp in range(2)],
                                 axis=1) for i in range(4)], axis=0)
            s_all.append(_dot_t(qm, k_refs[j][p]) + bias)
        mm = s_all[0]
        if n_lat:
            mm = jnp.maximum(jnp.maximum(mm, s_all[1]), jnp.maximum(s_all[2], s_all[3]))
        m = jnp.max(jnp.maximum(mm[:, :LANES], mm[:, LANES:]), axis=-1, keepdims=True)
        return s_all, m

    def weighted_values(head, s_all, m):
        p, hh = divmod(head, 2)
        acc = jnp.dot(jnp.exp2(s_all[0] - m).astype(BF16), vc_ref[hh, p],
                      preferred_element_type=F32)
        for j in range(len(s_all) - 1):
            acc = acc + jnp.dot(jnp.exp2(s_all[j + 1] - m).astype(BF16), v_refs[j][hh, p],
                                preferred_element_type=F32)
        return acc / pltpu.roll(acc, HD, axis=1)

    def all_heads(n_lat):
        pending = scores(0, n_lat)
        outs = []
        for head in range(H):
            nxt = scores(head + 1, n_lat) if head + 1 < H else None
            outs.append(weighted_values(head, *pending))
            pending = nxt
            if head % 2 == 1:
                o_ref[head // 2] = jnp.where(lane < HD, outs[0], outs[1]).astype(BF16)
                outs = []

    @pl.when(jj == 0)
    def _():
        all_heads(0)

    @pl.when(jj > 0)
    def _():
        all_heads(3)


def _attn_call(q, k, v, t2):
    def blk(fn):
        return pl.BlockSpec((NPAIR, QB, LANES), lambda b, jj, idx: (0, fn(b, jj), 0))

    def vblk(fn):
        return pl.BlockSpec((2, NPAIR, QB, LANES), lambda b, jj, idx: (0, 0, fn(b, jj), 0))

    def win(o):
        return lambda b, jj: b * NQB + jnp.clip(jj - 1, 1, NQB - 3) + o

    return pl.pallas_call(
        _attn_kernel,
        grid_spec=pltpu.PrefetchScalarGridSpec(
            num_scalar_prefetch=1,
            grid=(B, NQB),
            in_specs=[
                blk(lambda b, jj: b * NQB + jj),
                blk(lambda b, jj: b * NQB), vblk(lambda b, jj: b * NQB),
                blk(win(0)), blk(win(1)), blk(win(2)),
                vblk(win(0)), vblk(win(1)), vblk(win(2)),
                _const_spec((H, T2_ENTRIES, GRID_W, LANES)),
            ],
            out_specs=blk(lambda b, jj: b * NQB + jj),
        ),
        out_shape=jax.ShapeDtypeStruct((NPAIR, NROWS, LANES), BF16),
        compiler_params=_cparams(2),
        name="na_attention",
    )(jnp.asarray(_bias_tile_table()), q, k, v, k, k, k, v, v, v, t2)


def _proj_kernel(x_ref, y_ref, gate_ref, w_ref, o_ref, *, t):
    b, pos = _tile_pos(t.tm, 0, t)
    y = jnp.concatenate([y_ref[p] for p in range(NPAIR)], axis=1)
    o_ref[...] = x_ref[...] + _mod_rows(gate_ref, b, pos < C) * jnp.dot(
        y, w_ref[...], preferred_element_type=F32)


def _proj_call(xs, y, mods, layer, w):
    t = WIDE
    return pl.pallas_call(
        functools.partial(_proj_kernel, t=t),
        grid=(NROWS // t.tm,),
        in_specs=[_row_spec(t=t), _pair_spec(t), _mods_spec(layer, 2), _const_spec((D, D))],
        out_specs=_row_spec(t=t),
        out_shape=jax.ShapeDtypeStruct((NROWS, D), F32),
        compiler_params=_cparams(1),
        name="na_out_proj",
    )(xs, y, mods, w)


def _bias_table(rpb):
    qc = jnp.arange(GRID_W)[:, None]
    kc = jnp.arange(GRID_W)[None, :]
    ws = jnp.clip(qc - WIN_COLS // 2, 0, GRID_W - WIN_COLS)
    col_ok = (kc >= ws) & (kc < ws + WIN_COLS)
    onehot = ((kc - qc + (WIN_COLS - 1))[None] == jnp.arange(2 * WIN_COLS - 1)[:, None, None])
    t = jnp.einsum("hdc,cqk->hdqk", rpb, onehot.astype(F32), precision=lax.Precision.HIGHEST)
    t = jnp.where(col_ok[None, None], t, NEG_INF)
    dead = jnp.full((H, 1, GRID_W, GRID_W), NEG_INF, F32)
    t_ext = jnp.concatenate([dead, t, dead], axis=1)
    pairs = jnp.concatenate([t_ext[:, :-1], t_ext[:, 1:]], axis=-1)
    lo, hi = WIN_ROWS // 2 - 1, WIN_ROWS // 2 - 1 + WIN_ROWS - 1
    left_dead = jnp.concatenate([dead, t[:, lo:lo + 1]], axis=-1)
    right_dead = jnp.concatenate([t[:, hi:hi + 1], dead], axis=-1)
    all_dead = jnp.concatenate([dead, dead], axis=-1)
    return jnp.concatenate([pairs, left_dead, right_dead, all_dead], axis=1)


assert 8 < C % TM < TM - 8
BND = C % TM
BND_LO = BND - 8


def _halo_specs(t=STREAM):
    t8 = t.tm // 8
    return [
        pl.BlockSpec((8, D), lambda i: (jnp.maximum(i * t8 - 1, 0), 0)),
        _row_spec(t=t),
        pl.BlockSpec((8, D), lambda i: (jnp.minimum((i + 1) * t8, NROWS // 8 - 1), 0)),
    ]


def _dead_halo_rows(reach_back, t=STREAM):
    start = t.base + (pl.program_id(0) % t.tpb) * t.tm
    e = lax.broadcasted_iota(jnp.int32, (t.tm + 16, 1), 0)
    at_start = (start == 0) | (start == C)
    at_end = (start + t.tm == C) | (start + t.tm == LT)
    return ((e >= 8 - reach_back) & (e < 8) & at_start) | ((e == t.tm + 8) & at_end)


def _slab_conv(u_scr, s, taps, cb, pos, t=STREAM):
    def seg(lo, n, masked):
        y = cb
        for off, w in taps:
            u = u_scr[s, pl.ds(lo + 8 + off, n), :]
            if masked and off != 0:
                p = pos[lo:lo + n]
                crosses = (p >= C) & (p < C - off) if off < 0 else (p < C) & (p >= C - off)
                u = jnp.where(crosses, 0.0, u)
            y = y + u * w
        return y
    if t is LATENT:
        return seg(0, t.tm, False)
    return jnp.concatenate([seg(0, BND_LO, False), seg(BND_LO, 16, True),
                            seg(BND_LO + 16, t.tm - BND_LO - 16, False)], axis=0)


def _lru_in_kernel(xp_ref, x_ref, xn_ref, sh_ref, sc_ref, gain_ref, cw_ref, cb_ref, *rest, t):
    w_ref = rest[:2 * D // GW]
    gg_ref, xr_ref, rec_scr = rest[2 * D // GW:]
    b, pos = _tile_pos(t.tm, 0, t)
    _, pos_e = _tile_pos(t.tm + 16, -8, t)
    is_ctx_e = pos_e < C
    x_e = jnp.concatenate([xp_ref[...], x_ref[...], xn_ref[...]], axis=0)
    h = _modulate(x_e, gain_ref[...], _mod_rows(sh_ref, b, is_ctx_e),
                  _mod_rows(sc_ref, b, is_ctx_e))
    h = jnp.where(_dead_halo_rows(2, t), 0.0, h).astype(BF16)
    cw = cw_ref[...]
    cb = cb_ref[...]
    ngroup = D // GW

    def project(c):
        return (jnp.dot(h, w_ref[c][...], preferred_element_type=F32),
                jnp.dot(h, w_ref[ngroup + c][...], preferred_element_type=F32))

    pending = project(0)
    for c in range(ngroup):
        nxt = project(c + 1) if c + 1 < ngroup else None
        gate, rec = pending
        gg_ref[:, c * GW:(c + 1) * GW] = jax.nn.gelu(
            gate[8:8 + t.tm], approximate=True).astype(BF16)
        for s2 in range(GW // LANES):
            s = c * (GW // LANES) + s2
            ls = slice(s * LANES, (s + 1) * LANES)
            rec_scr[s] = rec[:, s2 * LANES:(s2 + 1) * LANES]
            taps = [(k - 2, cw[k:k + 1, ls]) for k in range(4)]
            xr_ref[:, ls] = _slab_conv(rec_scr, s, taps, cb[:, ls], pos, t)
        pending = nxt


def _lru_in_call(xs, mods, layer, gain, w, j, cw, cb):
    t = WIDE
    return pl.pallas_call(
        functools.partial(_lru_in_kernel, t=t),
        grid=(NROWS // t.tm,),
        in_specs=_halo_specs(t) + [
            _mods_spec(layer, 0), _mods_spec(layer, 1), _const_spec((1, D)),
            _const_spec((4, D)), _const_spec((1, D)),
        ] + _col_chunk_specs(j, GW, 2 * D // GW),
        out_specs=[_row_spec(t=t), _row_spec(t=t)],
        out_shape=[jax.ShapeDtypeStruct((NROWS, D), BF16),
                   jax.ShapeDtypeStruct((NROWS, D), F32)],
        scratch_shapes=[pltpu.VMEM((D // LANES, t.tm + 16, LANES), F32)],
        compiler_params=_cparams(1),
        name="lru_in_proj",
    )(xs, xs, xs, mods, mods, gain, cw, cb, *([w] * (2 * D // GW)))


def _chunk_index(g, reverse):
    if not reverse:
        return g
    return jnp.where(g < CCH, CCH - 1 - g, NCH + CCH - 1 - g)


def _scan_kernel(xr_ref, lam_ref, wg_ref, ba_ref, bx_ref, *rest, reverse):
    if reverse:
        o_ref, a_scr, b_scr, h_scr = rest
    else:
        hsb_ref, gg_ref, x_ref, wo_ref, g1_ref, o_ref, a_scr, b_scr, h_scr = rest
    g = pl.program_id(0)

    @pl.when(g == 0)
    def _():
        h_scr[...] = jnp.zeros_like(h_scr)

    lam = lam_ref[...]
    half_log_a = (-0.5 * LRU_C) * (jnp.maximum(-lam, 0.0) + jnp.log1p(jnp.exp(-jnp.abs(lam))))
    nslab = D // LANES
    group_slabs = GW // LANES

    def gates(gq):
        cols = slice(gq * GW, (gq + 1) * GW)
        xg = jnp.concatenate([xr_ref[b, :, cols] for b in range(B)], axis=0)
        pre = jnp.dot(xg.astype(BF16), wg_ref[gq], preferred_element_type=F32)
        t_r = jnp.tanh(pre[:, :GW] + ba_ref[:, cols])
        t_i = jnp.tanh(pre[:, GW:] + bx_ref[:, cols])
        log_a = half_log_a[:, cols] * t_r + half_log_a[:, cols]
        a = jnp.exp(log_a)
        y = (-1.0 - a * a) * jnp.tanh(log_a)
        root = jnp.where(y > 0.0, y * lax.rsqrt(y), 0.0)
        hx = 0.5 * xg
        bb = root * (hx * t_i + hx)
        for k2 in range(group_slabs):
            k = gq * group_slabs + k2
            for b in range(B):
                rows = slice(k * B * PITCH + b * PITCH, k * B * PITCH + b * PITCH + TS)
                a_scr[rows, :] = a[b * TS:(b + 1) * TS, k2 * LANES:(k2 + 1) * LANES]
                b_scr[rows, :] = bb[b * TS:(b + 1) * TS, k2 * LANES:(k2 + 1) * LANES]

    def recurrence(gq):
        slabs = range(gq * group_slabs, (gq + 1) * group_slabs)
        hs = {k: h_scr[k] for k in slabs}
        for t in (range(TS - 1, -1, -1) if reverse else range(TS)):
            for k in slabs:
                idx = pl.ds(k * B * PITCH + t, B, stride=PITCH)
                hs[k] = a_scr[idx, :] * hs[k] + b_scr[idx, :]
                b_scr[idx, :] = hs[k]
        for k in slabs:
            h_scr[k] = hs[k]

    def slab_rows(b, k):
        return slice(k * B * PITCH + b * PITCH, k * B * PITCH + b * PITCH + TS)

    gates(0)
    for gq in range(D // GW):
        if gq + 1 < D // GW:
            gates(gq + 1)
        recurrence(gq)

    if reverse:
        for b in range(B):
            for k in range(nslab):
                o_ref[b, :, k * LANES:(k + 1) * LANES] = b_scr[slab_rows(b, k), :].astype(BF16)
    else:
        z = jnp.concatenate([
            jnp.concatenate([
                ((b_scr[slab_rows(b, k), :] + hsb_ref[b, :, k * LANES:(k + 1) * LANES])
                 * gg_ref[b, :, k * LANES:(k + 1) * LANES].astype(F32)).astype(BF16)
                for k in range(nslab)], axis=1)
            for b in range(B)], axis=0)
        proj = jnp.dot(z, wo_ref[...], preferred_element_type=F32)
        is_ctx = g < CCH
        for b in range(B):
            g1 = jnp.where(is_ctx, g1_ref[8:9, :], g1_ref[b:b + 1, :])
            o_ref[b] = x_ref[b] + g1 * proj[b * TS:(b + 1) * TS]


def _scan_call(xr3, lam, wg, ba, bx, reverse, fwd_args=None):
    main = pl.BlockSpec((B, TS, D), lambda g: (0, _chunk_index(g, reverse), 0))
    in_specs = [
        main, _const_spec((1, D)),
        _const_spec((D // GW, GW, 2 * GW)), _const_spec((1, D)), _const_spec((1, D)),
    ]
    args = [xr3, lam, wg, 0.5 * ba, 0.5 * bx]
    if not reverse:
        hsb, gg, xs3, wo, mods, layer = fwd_args
        in_specs += [main, main, main, _const_spec((D, D)), _mods_spec(layer, 2)]
        args += [hsb, gg, xs3, wo, mods]
    slab = pltpu.VMEM((D // LANES * B * PITCH, LANES), F32)
    return pl.pallas_call(
        functools.partial(_scan_kernel, reverse=reverse),
        grid=(NCH,),
        in_specs=in_specs,
        out_specs=main,
        out_shape=jax.ShapeDtypeStruct((B, LT, D), BF16 if reverse else F32),
        scratch_shapes=[slab, slab, pltpu.VMEM((D // LANES, B, LANES), F32)],
        compiler_params=_cparams(1),
        name="lru_scan_bwd" if reverse else "lru_scan_fwd",
    )(*args)


def _gate_weights(wa, wx):
    def bd(w):
        w4 = w.reshape(D // GW, GW // LRU_BLOCK_W, LRU_BLOCK_W, LRU_BLOCK_W)
        eye = jnp.eye(GW // LRU_BLOCK_W, dtype=w.dtype)
        return (w4[:, :, :, None, :] * eye[None, :, None, :, None]).reshape(D // GW, GW, GW)
    return (0.5 * jnp.concatenate([bd(wa), bd(wx)], axis=-1)).astype(BF16)


FSLAB = FC // LANES


def _ffn_kernel(*refs, t):
    nchunk = 2 * D_FF // FC
    xp_ref, *x_refs = refs[:1 + t.main_blocks]
    (xn_ref, sh_ref, sc_ref, g2_ref, gain_ref, cw_ref, cb_ref,
     wd_ref) = refs[1 + t.main_blocks:9 + t.main_blocks]
    wu_ref = refs[9 + t.main_blocks:9 + t.main_blocks + nchunk]
    o_ref, uv_scr, ug_scr = refs[9 + t.main_blocks + nchunk:]
    tm = t.tm
    b, pos = _tile_pos(tm, 0, t)
    _, pos_e = _tile_pos(tm + 16, -8, t)
    is_ctx_e = pos_e < C
    x = jnp.concatenate([r[...] for r in x_refs], axis=0)
    x_e = jnp.concatenate([xp_ref[...], x, xn_ref[...]], axis=0)
    h = _modulate(x_e, gain_ref[...], _mod_rows(sh_ref, b, is_ctx_e),
                  _mod_rows(sc_ref, b, is_ctx_e))
    h = jnp.where(_dead_halo_rows(1, t), 0.0, h).astype(BF16)

    cw = cw_ref[...]
    cb = cb_ref[...]
    def up_project(ci):
        uv = jnp.dot(h, wu_ref[ci][...], preferred_element_type=F32)
        ug = jnp.dot(h, wu_ref[D_FF // FC + ci][...], preferred_element_type=F32)
        base = (ci % 2) * FSLAB
        for s in range(FSLAB):
            uv_scr[base + s] = uv[:, s * LANES:(s + 1) * LANES]
            ug_scr[base + s] = ug[:, s * LANES:(s + 1) * LANES]

    acc = jnp.zeros((tm, D), F32)
    up_project(0)
    for ci in range(D_FF // FC):
        if ci + 1 < D_FF // FC:
            up_project(ci + 1)
        base = (ci % 2) * FSLAB
        acts = []
        for s in range(FSLAB):
            lv = slice(ci * FC + s * LANES, ci * FC + (s + 1) * LANES)
            lg = slice(D_FF + lv.start, D_FF + lv.stop)
            val = _slab_conv(uv_scr, base + s, [(k - 1, cw[k:k + 1, lv]) for k in range(3)],
                             cb[:, lv], pos, t)
            hg = _slab_conv(ug_scr, base + s,
                            [(k - 1, 0.5 * cw[k:k + 1, lg]) for k in range(3)],
                            0.5 * cb[:, lg], pos, t)
            acts.append((val * (hg + hg * jnp.tanh(hg))).astype(BF16))
        acc = acc + jnp.dot(jnp.concatenate(acts, axis=1), wd_ref[ci],
                            preferred_element_type=F32)
    o_ref[...] = x + _mod_rows(g2_ref, b, pos < C) * acc


def _ffn_call(xs, mods, layer, gain, wu, cw, cb, wd, t):
    nchunk = 2 * D_FF // FC

    def layer_spec(shape):
        return pl.BlockSpec((None,) + shape, lambda i: (layer,) + (0,) * len(shape),
                            pipeline_mode=pl.Buffered(1))

    slab = pltpu.VMEM((2 * FSLAB, t.tm + 16, LANES), F32)
    if t is STREAM:
        x_specs, out_rows = _halo_specs(), NROWS
    else:
        def row0(i):
            return (i // t.tpb) * LT + t.base + (i % t.tpb) * t.tm
        half = t.tm // t.main_blocks
        x_specs = (
            [pl.BlockSpec((8, D), lambda i: (row0(i) // 8 - 1, 0))]
            + [pl.BlockSpec((half, D), lambda i, k=k: (row0(i) // half + k, 0))
               for k in range(t.main_blocks)]
            + [pl.BlockSpec((8, D), lambda i: (jnp.minimum((row0(i) + t.tm) // 8,
                                                            NROWS // 8 - 1), 0))])
        out_rows = B * S
    return pl.pallas_call(
        functools.partial(_ffn_kernel, t=t),
        grid=(out_rows // t.tm,),
        in_specs=x_specs + [
            _mods_spec(layer, 3), _mods_spec(layer, 4), _mods_spec(layer, 5),
            _const_spec((1, D)), layer_spec((3, 2 * D_FF)),
            layer_spec((1, 2 * D_FF)), layer_spec((D_FF // FC, FC, D)),
        ] + _col_chunk_specs(layer, FC, nchunk),
        out_specs=pl.BlockSpec((t.tm, D), lambda i: (i, 0)),
        out_shape=jax.ShapeDtypeStruct((out_rows, D), F32),
        scratch_shapes=[slab, slab],
        compiler_params=_cparams(1),
        name="conv_ffn",
    )(*([xs] * (2 + t.main_blocks)), mods, mods, mods, gain, cw, cb, wd, *([wu] * nchunk))


def kernel(x, c, ctx, c_ctx, ada_w, ada_b, norm_mix, norm_ffn, na_w_qkv, na_q_gain, na_k_gain,
           na_rpb, na_w_out, lru_w_in, lru_conv_w, lru_conv_b, lru_ga_w, lru_ga_b, lru_gx_w,
           lru_gx_b, lru_lambda, lru_w_out, ffn_w_up, ffn_conv_w, ffn_conv_b, ffn_w_down):
    xs = jnp.concatenate([ctx, x], axis=1).reshape(NROWS, D)
    cc = jnp.concatenate([c, c_ctx[None], jnp.zeros((7, D), F32)], axis=0)
    mods = _mods_call(cc, ada_w, ada_b)

    head_of_lane = jnp.arange(D) // HD
    g1 = (head_of_lane[:, None] == jnp.arange(LANES)[None, :]).astype(BF16)

    w_qkv = na_w_qkv.astype(BF16)
    w_in = lru_w_in.astype(BF16)
    w_up = ffn_w_up.astype(BF16)
    w_down = ffn_w_down.astype(BF16).reshape(DEPTH, D_FF // FC, FC, D)
    conv_b = ffn_conv_b[:, None]

    for i in range(DEPTH):
        j = i // 2
        gain_mix = norm_mix[i][None]
        if i % 2 == 0:
            qg = (jnp.tile(na_q_gain[j], H) * (HD ** -0.5 * LOG2_E))[None]
            kg = jnp.tile(na_k_gain[j], H)[None]
            q, k, v = _qkv_call(xs, mods, i, gain_mix, w_qkv, j, qg, kg, g1)
            o = _attn_call(q, k, v, _bias_table(na_rpb[j] * LOG2_E))
            xs = _proj_call(xs, o, mods, i, na_w_out[j].astype(BF16))
        else:
            gg, xr = _lru_in_call(xs, mods, i, gain_mix, w_in, j,
                                  lru_conv_w[j], lru_conv_b[j][None])
            xr3 = xr.reshape(B, LT, D)
            hsb = _scan_call(xr3, lru_lambda[j, 1][None],
                             _gate_weights(lru_ga_w[j, 1], lru_gx_w[j, 1]),
                             lru_ga_b[j, 1][None], lru_gx_b[j, 1][None], True)
            xs = _scan_call(xr3, lru_lambda[j, 0][None],
                            _gate_weights(lru_ga_w[j, 0], lru_gx_w[j, 0]),
                            lru_ga_b[j, 0][None], lru_gx_b[j, 0][None], False,
                            (hsb, gg.reshape(B, LT, D), xs.reshape(B, LT, D),
                             lru_w_out[j].astype(BF16), mods, i)).reshape(NROWS, D)
        xs = _ffn_call(xs, mods, i, norm_ffn[i][None], w_up, ffn_conv_w, conv_b, w_down,
                       LATENT if i == DEPTH - 1 else STREAM)
    return xs.reshape(B, S, D)
```

```python
import functools
from typing import NamedTuple

import jax
import jax.numpy as jnp
import numpy as np
from jax import lax
from jax.experimental import pallas as pl
from jax.experimental.pallas import tpu as pltpu

F32 = jnp.float32
BF16 = jnp.bfloat16

D = 1024
B = 8
S = 4096
DEPTH = 4
C = 256
LT = C + S
NROWS = B * LT
GRID_W = 64
IMG_ROWS = S // GRID_W
H = 16
HD = 64
NPAIR = H // 2
LANES = 128
WIN_ROWS = 8
WIN_COLS = 16
D_FF = 3 * D
LRU_BLOCK_W = 64
LRU_C = 8.0
EPS = 1e-6
NEG_INF = float("-inf")
LOG2_E = 1.4426950408889634

TM = 544
TPB = LT // TM
FC = 512
QB = 256
NQB = LT // QB
TS = 64
NCH = LT // TS
CCH = C // TS
PITCH = TS + 4
GW = 256
VMEM_LIMIT = 56 * 1024 * 1024

T2_LEFT_DEAD = 16
T2_RIGHT_DEAD = 17
T2_DEAD = 18
T2_ENTRIES = 19


def _cparams(n_axes):
    return pltpu.CompilerParams(
        dimension_semantics=("arbitrary",) * n_axes, vmem_limit_bytes=VMEM_LIMIT)


def _const_spec(shape):
    nd = len(shape)
    return pl.BlockSpec(shape, lambda *_: (0,) * nd, pipeline_mode=pl.Buffered(1))


def _mods_kernel(cc_ref, w_ref, b_ref, o_ref):
    c = cc_ref[...]
    s = c * jax.nn.sigmoid(c)
    o_ref[...] = jnp.dot(s.astype(BF16), w_ref[...].astype(BF16),
                         preferred_element_type=F32) + b_ref[...]


def _mods_call(cc, ada_w, ada_b):
    return pl.pallas_call(
        _mods_kernel,
        grid=(DEPTH, 6),
        in_specs=[
            pl.BlockSpec((16, D), lambda l, k: (0, 0)),
            pl.BlockSpec((None, D, D), lambda l, k: (l, 0, k)),
            pl.BlockSpec((None, None, 1, D), lambda l, k: (l, k, 0, 0)),
        ],
        out_specs=pl.BlockSpec((None, None, 16, D), lambda l, k: (l, k, 0, 0)),
        out_shape=jax.ShapeDtypeStruct((DEPTH, 6, 16, D), F32),
        compiler_params=_cparams(2),
        name="adaln_mods",
    )(cc, ada_w, ada_b.reshape(DEPTH, 6, 1, D))


class Tiling(NamedTuple):
    tm: int
    tpb: int
    base: int
    main_blocks: int


STREAM = Tiling(TM, TPB, 0, 1)
LATENT = Tiling(512, S // 512, C, 2)
WIDE = Tiling(2 * TM, TPB // 2, 0, 1)


def _tile_pos(n, lo, t=STREAM):
    i = pl.program_id(0)
    b = i // t.tpb
    j = i % t.tpb
    pos = t.base + j * t.tm + lo + lax.broadcasted_iota(jnp.int32, (n, 1), 0)
    return b, pos


def _mod_rows(m_ref, b, is_ctx):
    return jnp.where(is_ctx, m_ref[8:9, :], m_ref[pl.ds(b, 1), :])


def _modulate(x, gain, shift, scale):
    y = x * lax.rsqrt(jnp.mean(x * x, axis=-1, keepdims=True) + EPS)
    return (y * gain) * (1.0 + scale) + shift


def _col_chunk_specs(j, width, n):
    return [pl.BlockSpec((None, D, width), lambda *_, c=c: (j, 0, c),
                         pipeline_mode=pl.Buffered(1)) for c in range(n)]


def _mods_spec(layer, kind):
    return pl.BlockSpec((None, None, 16, D), lambda *_: (layer, kind, 0, 0))


def _row_spec(width=D, t=STREAM):
    return pl.BlockSpec((t.tm, width), lambda i: (i, 0))


def _pair_spec(t=STREAM):
    return pl.BlockSpec((NPAIR, t.tm, LANES), lambda i: (0, i, 0))


def _qkv_kernel(x_ref, sh_ref, sc_ref, gain_ref, qg_ref, kg_ref, g1_ref, wq_ref, wk_ref, wv_ref,
                q_ref, k_ref, v_ref, *, t):
    b, pos = _tile_pos(t.tm, 0, t)
    is_ctx = pos < C
    h = _modulate(x_ref[...], gain_ref[...], _mod_rows(sh_ref, b, is_ctx),
                  _mod_rows(sc_ref, b, is_ctx)).astype(BF16)
    head_col = lax.shift_right_logical(lax.broadcasted_iota(jnp.int32, (t.tm, LANES), 1), 6)
    def head_norm(acc, o_ref, gn_ref):
        ss = jnp.dot((acc * acc).astype(BF16), g1_ref[...], preferred_element_type=F32)
        rstd = lax.rsqrt(ss * (1.0 / HD) + EPS)
        for p in range(NPAIR):
            ls = slice(p * LANES, (p + 1) * LANES)
            rb = jnp.take_along_axis(rstd, head_col + 2 * p, axis=1, mode="promise_in_bounds")
            o_ref[p] = (acc[:, ls] * rb * gn_ref[:, ls]).astype(BF16)

    head_norm(jnp.dot(h, wq_ref[...], preferred_element_type=F32), q_ref, qg_ref)
    head_norm(jnp.dot(h, wk_ref[...], preferred_element_type=F32), k_ref, kg_ref)
    v = jnp.dot(h, wv_ref[...], preferred_element_type=F32).astype(BF16)
    first_head = lax.broadcasted_iota(jnp.int32, (1, LANES), 1) < HD
    one = jnp.ones((), BF16)
    for p in range(NPAIR):
        vp = v[:, p * LANES:(p + 1) * LANES]
        v_ref[0, p] = jnp.where(first_head, vp, one)
        v_ref[1, p] = jnp.where(first_head, one, vp)


def _qkv_call(xs, mods, layer, gain, w, j, qg, kg, g1):
    t = WIDE
    out = jax.ShapeDtypeStruct((NPAIR, NROWS, LANES), BF16)
    return pl.pallas_call(
        functools.partial(_qkv_kernel, t=t),
        grid=(NROWS // t.tm,),
        in_specs=[
            _row_spec(t=t), _mods_spec(layer, 0), _mods_spec(layer, 1), _const_spec((1, D)),
            _const_spec((1, D)), _const_spec((1, D)), _const_spec((D, LANES)),
        ] + _col_chunk_specs(j, D, 3),
        out_specs=[_pair_spec(t), _pair_spec(t),
                   pl.BlockSpec((2, NPAIR, t.tm, LANES), lambda i: (0, 0, i, 0))],
        out_shape=[out, out, jax.ShapeDtypeStruct((2, NPAIR, NROWS, LANES), BF16)],
        compiler_params=_cparams(1),
        name="na_qkv",
    )(xs, mods, mods, gain, qg, kg, g1, w, w, w)


def _dot_t(a, b):
    return lax.dot_general(a, b, (((1,), (1,)), ((), ())), preferred_element_type=F32)


def _bias_tile_table():
    tab = np.zeros((NQB, 24), np.int32)
    for jj in range(NQB):
        rg = jj - 1
        win0 = min(max(rg, 1), NQB - 3) - 1
        off = 4 * win0 - 4 * rg + (WIN_ROWS - 1)
        for i in range(4):
            rs = min(max(4 * rg + i - WIN_ROWS // 2, 0), IMG_ROWS - WIN_ROWS)
            for pu in range(6):
                kl = 4 * win0 + 2 * pu
                lv = jj > 0 and rs <= kl < rs + WIN_ROWS
                rv = jj > 0 and rs <= kl + 1 < rs + WIN_ROWS
                d = 2 * pu - i + off
                if lv and rv:
                    assert 0 <= d <= 2 * WIN_ROWS - 3
                    tab[jj, i * 6 + pu] = d + 1
                elif rv:
                    assert d + 1 == WIN_ROWS // 2 - 1
                    tab[jj, i * 6 + pu] = T2_LEFT_DEAD
                elif lv:
                    assert d == WIN_ROWS // 2 - 1 + WIN_ROWS - 1
                    tab[jj, i * 6 + pu] = T2_RIGHT_DEAD
                else:
                    tab[jj, i * 6 + pu] = T2_DEAD
    return tab


def _attn_kernel(idx_ref, q_ref, kc_ref, vc_ref, k0_ref, k1_ref, k2_ref, v0_ref, v1_ref, v2_ref,
                 t2_ref, o_ref):
    jj = pl.program_id(1)
    tile_idx = [[idx_ref[jj, i * 6 + pu] for pu in range(6)] for i in range(4)]
    lane = lax.broadcasted_iota(jnp.int32, (1, LANES), 1)
    k_refs = (k0_ref, k1_ref, k2_ref)
    v_refs = (v0_ref, v1_ref, v2_ref)

    def scores(head, n_lat):
        p, hh = divmod(head, 2)
        qp = q_ref[p]
        sel = (lane < HD) if hh == 0 else (lane >= HD)
        qm = jnp.where(sel, qp, jnp.zeros_like(qp))
        s_all = [_dot_t(qm, kc_ref[p])]
        for j in range(n_lat):
            bias = jnp.concatenate(
                [jnp.concatenate([t2_ref[head, tile_idx[i][2 * j + cp]] for cp in range(2)],
                                 axis=1) for i in range(4)], axis=0)
            s_all.append(_dot_t(qm, k_refs[j][p]) + bias)
        mm = s_all[0]
        if n_lat:
            mm = jnp.maximum(jnp.maximum(mm, s_all[1]), jnp.maximum(s_all[2], s_all[3]))
        m = jnp.max(jnp.maximum(mm[:, :LANES], mm[:, LANES:]), axis=-1, keepdims=True)
        return s_all, m

    def weighted_values(head, s_all, m):
        p, hh = divmod(head, 2)
        acc = jnp.dot(jnp.exp2(s_all[0] - m).astype(BF16), vc_ref[hh, p],
                      preferred_element_type=F32)
        for j in range(len(s_all) - 1):
            acc = acc + jnp.dot(jnp.exp2(s_all[j + 1] - m).astype(BF16), v_refs[j][hh, p],
                                preferred_element_type=F32)
        return acc / pltpu.roll(acc, HD, axis=1)

    def all_heads(n_lat):
        pending = scores(0, n_lat)
        outs = []
        for head in range(H):
            nxt = scores(head + 1, n_lat) if head + 1 < H else None
            outs.append(weighted_values(head, *pending))
            pending = nxt
            if head % 2 == 1:
                o_ref[head // 2] = jnp.where(lane < HD, outs[0], outs[1]).astype(BF16)
                outs = []

    @pl.when(jj == 0)
    def _():
        all_heads(0)

    @pl.when(jj > 0)
    def _():
        all_heads(3)


def _attn_call(q, k, v, t2):
    def blk(fn):
        return pl.BlockSpec((NPAIR, QB, LANES), lambda b, jj, idx: (0, fn(b, jj), 0))

    def vblk(fn):
        return pl.BlockSpec((2, NPAIR, QB, LANES), lambda b, jj, idx: (0, 0, fn(b, jj), 0))

    def win(o):
        return lambda b, jj: b * NQB + jnp.clip(jj - 1, 1, NQB - 3) + o

    return pl.pallas_call(
        _attn_kernel,
        grid_spec=pltpu.PrefetchScalarGridSpec(
            num_scalar_prefetch=1,
            grid=(B, NQB),
            in_specs=[
                blk(lambda b, jj: b * NQB + jj),
                blk(lambda b, jj: b * NQB), vblk(lambda b, jj: b * NQB),
                blk(win(0)), blk(win(1)), blk(win(2)),
                vblk(win(0)), vblk(win(1)), vblk(win(2)),
                _const_spec((H, T2_ENTRIES, GRID_W, LANES)),
            ],
            out_specs=blk(lambda b, jj: b * NQB + jj),
        ),
        out_shape=jax.ShapeDtypeStruct((NPAIR, NROWS, LANES), BF16),
        compiler_params=_cparams(2),
        name="na_attention",
    )(jnp.asarray(_bias_tile_table()), q, k, v, k, k, k, v, v, v, t2)


def _proj_kernel(x_ref, y_ref, gate_ref, w_ref, o_ref, *, t):
    b, pos = _tile_pos(t.tm, 0, t)
    y = jnp.concatenate([y_ref[p] for p in range(NPAIR)], axis=1)
    o_ref[...] = x_ref[...] + _mod_rows(gate_ref, b, pos < C) * jnp.dot(
        y, w_ref[...], preferred_element_type=F32)


def _proj_call(xs, y, mods, layer, w):
    t = WIDE
    return pl.pallas_call(
        functools.partial(_proj_kernel, t=t),
        grid=(NROWS // t.tm,),
        in_specs=[_row_spec(t=t), _pair_spec(t), _mods_spec(layer, 2), _const_spec((D, D))],
        out_specs=_row_spec(t=t),
        out_shape=jax.ShapeDtypeStruct((NROWS, D), F32),
        compiler_params=_cparams(1),
        name="na_out_proj",
    )(xs, y, mods, w)


def _bias_table(rpb):
    qc = jnp.arange(GRID_W)[:, None]
    kc = jnp.arange(GRID_W)[None, :]
    ws = jnp.clip(qc - WIN_COLS // 2, 0, GRID_W - WIN_COLS)
    col_ok = (kc >= ws) & (kc < ws + WIN_COLS)
    onehot = ((kc - qc + (WIN_COLS - 1))[None] == jnp.arange(2 * WIN_COLS - 1)[:, None, None])
    t = jnp.einsum("hdc,cqk->hdqk", rpb, onehot.astype(F32), precision=lax.Precision.HIGHEST)
    t = jnp.where(col_ok[None, None], t, NEG_INF)
    dead = jnp.full((H, 1, GRID_W, GRID_W), NEG_INF, F32)
    t_ext = jnp.concatenate([dead, t, dead], axis=1)
    pairs = jnp.concatenate([t_ext[:, :-1], t_ext[:, 1:]], axis=-1)
    lo, hi = WIN_ROWS // 2 - 1, WIN_ROWS // 2 - 1 + WIN_ROWS - 1
    left_dead = jnp.concatenate([dead, t[:, lo:lo + 1]], axis=-1)
    right_dead = jnp.concatenate([t[:, hi:hi + 1], dead], axis=-1)
    all_dead = jnp.concatenate([dead, dead], axis=-1)
    return jnp.concatenate([pairs, left_dead, right_dead, all_dead], axis=1)


assert 8 < C % TM < TM - 8
BND = C % TM
BND_LO = BND - 8


def _halo_specs():
    t8 = TM // 8
    return [
        pl.BlockSpec((8, D), lambda i: (jnp.maximum(i * t8 - 1, 0), 0)),
        _row_spec(),
        pl.BlockSpec((8, D), lambda i: (jnp.minimum((i + 1) * t8, NROWS // 8 - 1), 0)),
    ]


def _dead_halo_rows(reach_back, t=STREAM):
    start = t.base + (pl.program_id(0) % t.tpb) * t.tm
    e = lax.broadcasted_iota(jnp.int32, (t.tm + 16, 1), 0)
    at_start = (start == 0) | (start == C)
    at_end = (start + t.tm == C) | (start + t.tm == LT)
    return ((e >= 8 - reach_back) & (e < 8) & at_start) | ((e == t.tm + 8) & at_end)


def _slab_conv(u_scr, s, taps, cb, pos, t=STREAM):
    def seg(lo, n, masked):
        y = cb
        for off, w in taps:
            u = u_scr[s, pl.ds(lo + 8 + off, n), :]
            if masked and off != 0:
                p = pos[lo:lo + n]
                crosses = (p >= C) & (p < C - off) if off < 0 else (p < C) & (p >= C - off)
                u = jnp.where(crosses, 0.0, u)
            y = y + u * w
        return y
    if t is LATENT:
        return seg(0, t.tm, False)
    return jnp.concatenate([seg(0, BND_LO, False), seg(BND_LO, 16, True),
                            seg(BND_LO + 16, TM - BND_LO - 16, False)], axis=0)


def _lru_in_kernel(xp_ref, x_ref, xn_ref, sh_ref, sc_ref, gain_ref, cw_ref, cb_ref, *rest):
    w_ref = rest[:2 * D // GW]
    gg_ref, xr_ref, rec_scr = rest[2 * D // GW:]
    b, pos = _tile_pos(TM, 0)
    _, pos_e = _tile_pos(TM + 16, -8)
    is_ctx_e = pos_e < C
    x_e = jnp.concatenate([xp_ref[...], x_ref[...], xn_ref[...]], axis=0)
    h = _modulate(x_e, gain_ref[...], _mod_rows(sh_ref, b, is_ctx_e),
                  _mod_rows(sc_ref, b, is_ctx_e))
    h = jnp.where(_dead_halo_rows(2), 0.0, h).astype(BF16)
    cw = cw_ref[...]
    cb = cb_ref[...]
    ngroup = D // GW

    def project(c):
        return (jnp.dot(h, w_ref[c][...], preferred_element_type=F32),
                jnp.dot(h, w_ref[ngroup + c][...], preferred_element_type=F32))

    pending = project(0)
    for c in range(ngroup):
        nxt = project(c + 1) if c + 1 < ngroup else None
        gate, rec = pending
        gg_ref[:, c * GW:(c + 1) * GW] = jax.nn.gelu(
            gate[8:8 + TM], approximate=True).astype(BF16)
        for s2 in range(GW // LANES):
            s = c * (GW // LANES) + s2
            ls = slice(s * LANES, (s + 1) * LANES)
            rec_scr[s] = rec[:, s2 * LANES:(s2 + 1) * LANES]
            taps = [(k - 2, cw[k:k + 1, ls]) for k in range(4)]
            xr_ref[:, ls] = _slab_conv(rec_scr, s, taps, cb[:, ls], pos)
        pending = nxt


def _lru_in_call(xs, mods, layer, gain, w, j, cw, cb):
    return pl.pallas_call(
        _lru_in_kernel,
        grid=(NROWS // TM,),
        in_specs=_halo_specs() + [
            _mods_spec(layer, 0), _mods_spec(layer, 1), _const_spec((1, D)),
            _const_spec((4, D)), _const_spec((1, D)),
        ] + _col_chunk_specs(j, GW, 2 * D // GW),
        out_specs=[_row_spec(), _row_spec()],
        out_shape=[jax.ShapeDtypeStruct((NROWS, D), BF16),
                   jax.ShapeDtypeStruct((NROWS, D), F32)],
        scratch_shapes=[pltpu.VMEM((D // LANES, TM + 16, LANES), F32)],
        compiler_params=_cparams(1),
        name="lru_in_proj",
    )(xs, xs, xs, mods, mods, gain, cw, cb, *([w] * (2 * D // GW)))


def _chunk_index(g, reverse):
    if not reverse:
        return g
    return jnp.where(g < CCH, CCH - 1 - g, NCH + CCH - 1 - g)


def _scan_kernel(xr_ref, lam_ref, wg_ref, ba_ref, bx_ref, *rest, reverse):
    if reverse:
        o_ref, a_scr, b_scr, h_scr = rest
    else:
        hsb_ref, gg_ref, x_ref, wo_ref, g1_ref, o_ref, a_scr, b_scr, h_scr = rest
    g = pl.program_id(0)

    @pl.when(g == 0)
    def _():
        h_scr[...] = jnp.zeros_like(h_scr)

    lam = lam_ref[...]
    half_log_a = (-0.5 * LRU_C) * (jnp.maximum(-lam, 0.0) + jnp.log1p(jnp.exp(-jnp.abs(lam))))
    nslab = D // LANES
    group_slabs = GW // LANES

    def gates(gq):
        cols = slice(gq * GW, (gq + 1) * GW)
        xg = jnp.concatenate([xr_ref[b, :, cols] for b in range(B)], axis=0)
        pre = jnp.dot(xg.astype(BF16), wg_ref[gq], preferred_element_type=F32)
        t_r = jnp.tanh(pre[:, :GW] + ba_ref[:, cols])
        t_i = jnp.tanh(pre[:, GW:] + bx_ref[:, cols])
        log_a = half_log_a[:, cols] * t_r + half_log_a[:, cols]
        a = jnp.exp(log_a)
        y = (-1.0 - a * a) * jnp.tanh(log_a)
        root = jnp.where(y > 0.0, y * lax.rsqrt(y), 0.0)
        hx = 0.5 * xg
        bb = root * (hx * t_i + hx)
        for k2 in range(group_slabs):
            k = gq * group_slabs + k2
            for b in range(B):
                rows = slice(k * B * PITCH + b * PITCH, k * B * PITCH + b * PITCH + TS)
                a_scr[rows, :] = a[b * TS:(b + 1) * TS, k2 * LANES:(k2 + 1) * LANES]
                b_scr[rows, :] = bb[b * TS:(b + 1) * TS, k2 * LANES:(k2 + 1) * LANES]

    def recurrence(gq):
        slabs = range(gq * group_slabs, (gq + 1) * group_slabs)
        hs = {k: h_scr[k] for k in slabs}
        for t in (range(TS - 1, -1, -1) if reverse else range(TS)):
            for k in slabs:
                idx = pl.ds(k * B * PITCH + t, B, stride=PITCH)
                hs[k] = a_scr[idx, :] * hs[k] + b_scr[idx, :]
                b_scr[idx, :] = hs[k]
        for k in slabs:
            h_scr[k] = hs[k]

    def slab_rows(b, k):
        return slice(k * B * PITCH + b * PITCH, k * B * PITCH + b * PITCH + TS)

    gates(0)
    for gq in range(D // GW):
        if gq + 1 < D // GW:
            gates(gq + 1)
        recurrence(gq)

    if reverse:
        for b in range(B):
            for k in range(nslab):
                o_ref[b, :, k * LANES:(k + 1) * LANES] = b_scr[slab_rows(b, k), :].astype(BF16)
    else:
        z = jnp.concatenate([
            jnp.concatenate([
                ((b_scr[slab_rows(b, k), :] + hsb_ref[b, :, k * LANES:(k + 1) * LANES])
                 * gg_ref[b, :, k * LANES:(k + 1) * LANES].astype(F32)).astype(BF16)
                for k in range(nslab)], axis=1)
            for b in range(B)], axis=0)
        proj = jnp.dot(z, wo_ref[...], preferred_element_type=F32)
        is_ctx = g < CCH
        for b in range(B):
            g1 = jnp.where(is_ctx, g1_ref[8:9, :], g1_ref[b:b + 1, :])
            o_ref[b] = x_ref[b] + g1 * proj[b * TS:(b + 1) * TS]


def _scan_call(xr3, lam, wg, ba, bx, reverse, fwd_args=None):
    main = pl.BlockSpec((B, TS, D), lambda g: (0, _chunk_index(g, reverse), 0))
    in_specs = [
        main, _const_spec((1, D)),
        _const_spec((D // GW, GW, 2 * GW)), _const_spec((1, D)), _const_spec((1, D)),
    ]
    args = [xr3, lam, wg, 0.5 * ba, 0.5 * bx]
    if not reverse:
        hsb, gg, xs3, wo, mods, layer = fwd_args
        in_specs += [main, main, main, _const_spec((D, D)), _mods_spec(layer, 2)]
        args += [hsb, gg, xs3, wo, mods]
    slab = pltpu.VMEM((D // LANES * B * PITCH, LANES), F32)
    return pl.pallas_call(
        functools.partial(_scan_kernel, reverse=reverse),
        grid=(NCH,),
        in_specs=in_specs,
        out_specs=main,
        out_shape=jax.ShapeDtypeStruct((B, LT, D), BF16 if reverse else F32),
        scratch_shapes=[slab, slab, pltpu.VMEM((D // LANES, B, LANES), F32)],
        compiler_params=_cparams(1),
        name="lru_scan_bwd" if reverse else "lru_scan_fwd",
    )(*args)


def _gate_weights(wa, wx):
    def bd(w):
        w4 = w.reshape(D // GW, GW // LRU_BLOCK_W, LRU_BLOCK_W, LRU_BLOCK_W)
        eye = jnp.eye(GW // LRU_BLOCK_W, dtype=w.dtype)
        return (w4[:, :, :, None, :] * eye[None, :, None, :, None]).reshape(D // GW, GW, GW)
    return (0.5 * jnp.concatenate([bd(wa), bd(wx)], axis=-1)).astype(BF16)


FSLAB = FC // LANES


def _ffn_kernel(*refs, t):
    nchunk = 2 * D_FF // FC
    xp_ref, *x_refs = refs[:1 + t.main_blocks]
    (xn_ref, sh_ref, sc_ref, g2_ref, gain_ref, cw_ref, cb_ref,
     wd_ref) = refs[1 + t.main_blocks:9 + t.main_blocks]
    wu_ref = refs[9 + t.main_blocks:9 + t.main_blocks + nchunk]
    o_ref, uv_scr, ug_scr = refs[9 + t.main_blocks + nchunk:]
    tm = t.tm
    b, pos = _tile_pos(tm, 0, t)
    _, pos_e = _tile_pos(tm + 16, -8, t)
    is_ctx_e = pos_e < C
    x = jnp.concatenate([r[...] for r in x_refs], axis=0)
    x_e = jnp.concatenate([xp_ref[...], x, xn_ref[...]], axis=0)
    h = _modulate(x_e, gain_ref[...], _mod_rows(sh_ref, b, is_ctx_e),
                  _mod_rows(sc_ref, b, is_ctx_e))
    h = jnp.where(_dead_halo_rows(1, t), 0.0, h).astype(BF16)

    cw = cw_ref[...]
    cb = cb_ref[...]
    def up_project(ci):
        uv = jnp.dot(h, wu_ref[ci][...], preferred_element_type=F32)
        ug = jnp.dot(h, wu_ref[D_FF // FC + ci][...], preferred_element_type=F32)
        base = (ci % 2) * FSLAB
        for s in range(FSLAB):
            uv_scr[base + s] = uv[:, s * LANES:(s + 1) * LANES]
            ug_scr[base + s] = ug[:, s * LANES:(s + 1) * LANES]

    acc = jnp.zeros((tm, D), F32)
    up_project(0)
    for ci in range(D_FF // FC):
        if ci + 1 < D_FF // FC:
            up_project(ci + 1)
        base = (ci % 2) * FSLAB
        acts = []
        for s in range(FSLAB):
            lv = slice(ci * FC + s * LANES, ci * FC + (s + 1) * LANES)
            lg = slice(D_FF + lv.start, D_FF + lv.stop)
            val = _slab_conv(uv_scr, base + s, [(k - 1, cw[k:k + 1, lv]) for k in range(3)],
                             cb[:, lv], pos, t)
            hg = _slab_conv(ug_scr, base + s,
                            [(k - 1, 0.5 * cw[k:k + 1, lg]) for k in range(3)],
                            0.5 * cb[:, lg], pos, t)
            acts.append((val * (hg + hg * jnp.tanh(hg))).astype(BF16))
        acc = acc + jnp.dot(jnp.concatenate(acts, axis=1), wd_ref[ci],
                            preferred_element_type=F32)
    o_ref[...] = x + _mod_rows(g2_ref, b, pos < C) * acc


def _ffn_call(xs, mods, layer, gain, wu, cw, cb, wd, t):
    nchunk = 2 * D_FF // FC

    def layer_spec(shape):
        return pl.BlockSpec((None,) + shape, lambda i: (layer,) + (0,) * len(shape),
                            pipeline_mode=pl.Buffered(1))

    slab = pltpu.VMEM((2 * FSLAB, t.tm + 16, LANES), F32)
    if t is STREAM:
        x_specs, out_rows = _halo_specs(), NROWS
    else:
        def row0(i):
            return (i // t.tpb) * LT + t.base + (i % t.tpb) * t.tm
        half = t.tm // t.main_blocks
        x_specs = (
            [pl.BlockSpec((8, D), lambda i: (row0(i) // 8 - 1, 0))]
            + [pl.BlockSpec((half, D), lambda i, k=k: (row0(i) // half + k, 0))
               for k in range(t.main_blocks)]
            + [pl.BlockSpec((8, D), lambda i: (jnp.minimum((row0(i) + t.tm) // 8,
                                                            NROWS // 8 - 1), 0))])
        out_rows = B * S
    return pl.pallas_call(
        functools.partial(_ffn_kernel, t=t),
        grid=(out_rows // t.tm,),
        in_specs=x_specs + [
            _mods_spec(layer, 3), _mods_spec(layer, 4), _mods_spec(layer, 5),
            _const_spec((1, D)), layer_spec((3, 2 * D_FF)),
            layer_spec((1, 2 * D_FF)), layer_spec((D_FF // FC, FC, D)),
        ] + _col_chunk_specs(layer, FC, nchunk),
        out_specs=pl.BlockSpec((t.tm, D), lambda i: (i, 0)),
        out_shape=jax.ShapeDtypeStruct((out_rows, D), F32),
        scratch_shapes=[slab, slab],
        compiler_params=_cparams(1),
        name="conv_ffn",
    )(*([xs] * (2 + t.main_blocks)), mods, mods, mods, gain, cw, cb, wd, *([wu] * nchunk))


def kernel(x, c, ctx, c_ctx, ada_w, ada_b, norm_mix, norm_ffn, na_w_qkv, na_q_gain, na_k_gain,
           na_rpb, na_w_out, lru_w_in, lru_conv_w, lru_conv_b, lru_ga_w, lru_ga_b, lru_gx_w,
           lru_gx_b, lru_lambda, lru_w_out, ffn_w_up, ffn_conv_w, ffn_conv_b, ffn_w_down):
    xs = jnp.concatenate([ctx, x], axis=1).reshape(NROWS, D)
    cc = jnp.concatenate([c, c_ctx[None], jnp.zeros((7, D), F32)], axis=0)
    mods = _mods_call(cc, ada_w, ada_b)

    head_of_lane = jnp.arange(D) // HD
    g1 = (head_of_lane[:, None] == jnp.arange(LANES)[None, :]).astype(BF16)

    w_qkv = na_w_qkv.astype(BF16)
    w_in = lru_w_in.astype(BF16)
    w_up = ffn_w_up.astype(BF16)
    w_down = ffn_w_down.astype(BF16).reshape(DEPTH, D_FF // FC, FC, D)
    conv_b = ffn_conv_b[:, None]

    for i in range(DEPTH):
        j = i // 2
        gain_mix = norm_mix[i][None]
        if i % 2 == 0:
            qg = (jnp.tile(na_q_gain[j], H) * (HD ** -0.5 * LOG2_E))[None]
            kg = jnp.tile(na_k_gain[j], H)[None]
            q, k, v = _qkv_call(xs, mods, i, gain_mix, w_qkv, j, qg, kg, g1)
            o = _attn_call(q, k, v, _bias_table(na_rpb[j] * LOG2_E))
            xs = _proj_call(xs, o, mods, i, na_w_out[j].astype(BF16))
        else:
            gg, xr = _lru_in_call(xs, mods, i, gain_mix, w_in, j,
                                  lru_conv_w[j], lru_conv_b[j][None])
            xr3 = xr.reshape(B, LT, D)
            hsb = _scan_call(xr3, lru_lambda[j, 1][None],
                             _gate_weights(lru_ga_w[j, 1], lru_gx_w[j, 1]),
                             lru_ga_b[j, 1][None], lru_gx_b[j, 1][None], True)
            xs = _scan_call(xr3, lru_lambda[j, 0][None],
                            _gate_weights(lru_ga_w[j, 0], lru_gx_w[j, 0]),
                            lru_ga_b[j, 0][None], lru_gx_b[j, 0][None], False,
                            (hsb, gg.reshape(B, LT, D), xs.reshape(B, LT, D),
                             lru_w_out[j].astype(BF16), mods, i)).reshape(NROWS, D)
        xs = _ffn_call(xs, mods, i, norm_ffn[i][None], w_up, ffn_conv_w, conv_b, w_down,
                       LATENT if i == DEPTH - 1 else STREAM)
    return xs.reshape(B, S, D)
```

```python
import functools
from typing import NamedTuple

import jax
import jax.numpy as jnp
import numpy as np
from jax import lax
from jax.experimental import pallas as pl
from jax.experimental.pallas import tpu as pltpu

F32 = jnp.float32
BF16 = jnp.bfloat16

D = 1024
B = 8
S = 4096
DEPTH = 4
C = 256
LT = C + S
NROWS = B * LT
GRID_W = 64
IMG_ROWS = S // GRID_W
H = 16
HD = 64
NPAIR = H // 2
LANES = 128
WIN_ROWS = 8
WIN_COLS = 16
D_FF = 3 * D
LRU_BLOCK_W = 64
LRU_C = 8.0
EPS = 1e-6
NEG_INF = float("-inf")
LOG2_E = 1.4426950408889634

TM = 544
TPB = LT // TM
FC = 512
QB = 256
NQB = LT // QB
TS = 64
NCH = LT // TS
CCH = C // TS
PITCH = TS + 4
GW = 256
VMEM_LIMIT = 56 * 1024 * 1024

T2_LEFT_DEAD = 16
T2_RIGHT_DEAD = 17
T2_DEAD = 18
T2_ENTRIES = 19


def _cparams(n_axes, fuse_inputs=None):
    return pltpu.CompilerParams(
        dimension_semantics=("arbitrary",) * n_axes, vmem_limit_bytes=VMEM_LIMIT,
        allow_input_fusion=fuse_inputs)


def _const_spec(shape):
    nd = len(shape)
    return pl.BlockSpec(shape, lambda *_: (0,) * nd, pipeline_mode=pl.Buffered(1))


def _mods_kernel(cc_ref, w_ref, b_ref, o_ref):
    c = cc_ref[...]
    s = c * jax.nn.sigmoid(c)
    o_ref[...] = jnp.dot(s.astype(BF16), w_ref[...].astype(BF16),
                         preferred_element_type=F32) + b_ref[...]


def _mods_call(cc, ada_w, ada_b):
    return pl.pallas_call(
        _mods_kernel,
        grid=(DEPTH, 6),
        in_specs=[
            pl.BlockSpec((16, D), lambda l, k: (0, 0)),
            pl.BlockSpec((None, D, D), lambda l, k: (l, 0, k)),
            pl.BlockSpec((None, None, 1, D), lambda l, k: (l, k, 0, 0)),
        ],
        out_specs=pl.BlockSpec((None, None, 16, D), lambda l, k: (l, k, 0, 0)),
        out_shape=jax.ShapeDtypeStruct((DEPTH, 6, 16, D), F32),
        compiler_params=_cparams(2),
        name="adaln_mods",
    )(cc, ada_w, ada_b.reshape(DEPTH, 6, 1, D))


class Tiling(NamedTuple):
    tm: int
    tpb: int
    base: int
    main_blocks: int


STREAM = Tiling(TM, TPB, 0, 1)
LATENT = Tiling(512, S // 512, C, 2)
WIDE = Tiling(2 * TM, TPB // 2, 0, 1)


def _tile_pos(n, lo, t=STREAM):
    i = pl.program_id(0)
    b = i // t.tpb
    j = i % t.tpb
    pos = t.base + j * t.tm + lo + lax.broadcasted_iota(jnp.int32, (n, 1), 0)
    return b, pos


def _mod_rows(m_ref, b, is_ctx):
    return jnp.where(is_ctx, m_ref[8:9, :], m_ref[pl.ds(b, 1), :])


def _modulate(x, gain, shift, scale):
    y = x * lax.rsqrt(jnp.mean(x * x, axis=-1, keepdims=True) + EPS)
    return (y * gain) * (1.0 + scale) + shift


def _col_chunk_specs(j, width, n):
    return [pl.BlockSpec((None, D, width), lambda *_, c=c: (j, 0, c),
                         pipeline_mode=pl.Buffered(1)) for c in range(n)]


def _mods_spec(layer, kind):
    return pl.BlockSpec((None, None, 16, D), lambda *_: (layer, kind, 0, 0))


def _row_spec(width=D, t=STREAM):
    return pl.BlockSpec((t.tm, width), lambda i: (i, 0))


def _pair_spec(t=STREAM):
    return pl.BlockSpec((NPAIR, t.tm, LANES), lambda i: (0, i, 0))


def _qkv_kernel(x_ref, sh_ref, sc_ref, gain_ref, qg_ref, kg_ref, g1_ref, wq_ref, wk_ref, wv_ref,
                q_ref, k_ref, v_ref, *, t):
    b, pos = _tile_pos(t.tm, 0, t)
    is_ctx = pos < C
    h = _modulate(x_ref[...], gain_ref[...], _mod_rows(sh_ref, b, is_ctx),
                  _mod_rows(sc_ref, b, is_ctx)).astype(BF16)
    head_col = lax.shift_right_logical(lax.broadcasted_iota(jnp.int32, (t.tm, LANES), 1), 6)
    def head_norm(acc, o_ref, gn_ref):
        ss = jnp.dot((acc * acc).astype(BF16), g1_ref[...], preferred_element_type=F32)
        rstd = lax.rsqrt(ss * (1.0 / HD) + EPS)
        for p in range(NPAIR):
            ls = slice(p * LANES, (p + 1) * LANES)
            rb = jnp.take_along_axis(rstd, head_col + 2 * p, axis=1, mode="promise_in_bounds")
            o_ref[p] = (acc[:, ls] * rb * gn_ref[:, ls]).astype(BF16)

    head_norm(jnp.dot(h, wq_ref[...], preferred_element_type=F32), q_ref, qg_ref)
    head_norm(jnp.dot(h, wk_ref[...], preferred_element_type=F32), k_ref, kg_ref)
    v = jnp.dot(h, wv_ref[...], preferred_element_type=F32).astype(BF16)
    first_head = lax.broadcasted_iota(jnp.int32, (1, LANES), 1) < HD
    one = jnp.ones((), BF16)
    for p in range(NPAIR):
        vp = v[:, p * LANES:(p + 1) * LANES]
        v_ref[0, p] = jnp.where(first_head, vp, one)
        v_ref[1, p] = jnp.where(first_head, one, vp)


def _qkv_call(xs, mods, layer, gain, w, j, qg, kg, g1):
    t = WIDE
    out = jax.ShapeDtypeStruct((NPAIR, NROWS, LANES), BF16)
    return pl.pallas_call(
        functools.partial(_qkv_kernel, t=t),
        grid=(NROWS // t.tm,),
        in_specs=[
            _row_spec(t=t), _mods_spec(layer, 0), _mods_spec(layer, 1), _const_spec((1, D)),
            _const_spec((1, D)), _const_spec((1, D)), _const_spec((D, LANES)),
        ] + _col_chunk_specs(j, D, 3),
        out_specs=[_pair_spec(t), _pair_spec(t),
                   pl.BlockSpec((2, NPAIR, t.tm, LANES), lambda i: (0, 0, i, 0))],
        out_shape=[out, out, jax.ShapeDtypeStruct((2, NPAIR, NROWS, LANES), BF16)],
        compiler_params=_cparams(1, [False] * 7 + [True] * 3),
        name="na_qkv",
    )(xs, mods, mods, gain, qg, kg, g1, w, w, w)


def _dot_t(a, b):
    return lax.dot_general(a, b, (((1,), (1,)), ((), ())), preferred_element_type=F32)


def _bias_tile_table():
    tab = np.zeros((NQB, 24), np.int32)
    for jj in range(NQB):
        rg = jj - 1
        win0 = min(max(rg, 1), NQB - 3) - 1
        off = 4 * win0 - 4 * rg + (WIN_ROWS - 1)
        for i in range(4):
            rs = min(max(4 * rg + i - WIN_ROWS // 2, 0), IMG_ROWS - WIN_ROWS)
            for pu in range(6):
                kl = 4 * win0 + 2 * pu
                lv = jj > 0 and rs <= kl < rs + WIN_ROWS
                rv = jj > 0 and rs <= kl + 1 < rs + WIN_ROWS
                d = 2 * pu - i + off
                if lv and rv:
                    assert 0 <= d <= 2 * WIN_ROWS - 3
                    tab[jj, i * 6 + pu] = d + 1
                elif rv:
                    assert d + 1 == WIN_ROWS // 2 - 1
                    tab[jj, i * 6 + pu] = T2_LEFT_DEAD
                elif lv:
                    assert d == WIN_ROWS // 2 - 1 + WIN_ROWS - 1
                    tab[jj, i * 6 + pu] = T2_RIGHT_DEAD
                else:
                    tab[jj, i * 6 + pu] = T2_DEAD
    return tab


def _attn_kernel(idx_ref, q_ref, kc_ref, vc_ref, k0_ref, k1_ref, k2_ref, v0_ref, v1_ref, v2_ref,
                 t2_ref, o_ref):
    jj = pl.program_id(1)
    tile_idx = [[idx_ref[jj, i * 6 + pu] for pu in range(6)] for i in range(4)]
    lane = lax.broadcasted_iota(jnp.int32, (1, LANES), 1)
    k_refs = (k0_ref, k1_ref, k2_ref)
    v_refs = (v0_ref, v1_ref, v2_ref)

    def scores(head, n_lat):
        p, hh = divmod(head, 2)
        qp = q_ref[p]
        sel = (lane < HD) if hh == 0 else (lane >= HD)
        qm = jnp.where(sel, qp, jnp.zeros_like(qp))
        s_all = [_dot_t(qm, kc_ref[p])]
        for j in range(n_lat):
            bias = jnp.concatenate(
                [jnp.concatenate([t2_ref[head, tile_idx[i][2 * j + cp]] for cp in range(2)],
                                 axis=1) for i in range(4)], axis=0)
            s_all.append(_dot_t(qm, k_refs[j][p]) + bias)
        mm = s_all[0]
        if n_lat:
            mm = jnp.maximum(jnp.maximum(mm, s_all[1]), jnp.maximum(s_all[2], s_all[3]))
        m = jnp.max(jnp.maximum(mm[:, :LANES], mm[:, LANES:]), axis=-1, keepdims=True)
        return s_all, m

    def weighted_values(head, s_all, m):
        p, hh = divmod(head, 2)
        acc = jnp.dot(jnp.exp2(s_all[0] - m).astype(BF16), vc_ref[hh, p],
                      preferred_element_type=F32)
        for j in range(len(s_all) - 1):
            acc = acc + jnp.dot(jnp.exp2(s_all[j + 1] - m).astype(BF16), v_refs[j][hh, p],
                                preferred_element_type=F32)
        return acc / pltpu.roll(acc, HD, axis=1)

    def all_heads(n_lat):
        pending = scores(0, n_lat)
        outs = []
        for head in range(H):
            nxt = scores(head + 1, n_lat) if head + 1 < H else None
            outs.append(weighted_values(head, *pending))
            pending = nxt
            if head % 2 == 1:
                o_ref[head // 2] = jnp.where(lane < HD, outs[0], outs[1]).astype(BF16)
                outs = []

    @pl.when(jj == 0)
    def _():
        all_heads(0)

    @pl.when(jj > 0)
    def _():
        all_heads(3)


def _attn_call(q, k, v, t2):
    def blk(fn):
        return pl.BlockSpec((NPAIR, QB, LANES), lambda b, jj, idx: (0, fn(b, jj), 0))

    def vblk(fn):
        return pl.BlockSpec((2, NPAIR, QB, LANES), lambda b, jj, idx: (0, 0, fn(b, jj), 0))

    def win(o):
        return lambda b, jj: b * NQB + jnp.clip(jj - 1, 1, NQB - 3) + o

    return pl.pallas_call(
        _attn_kernel,
        grid_spec=pltpu.PrefetchScalarGridSpec(
            num_scalar_prefetch=1,
            grid=(B, NQB),
            in_specs=[
                blk(lambda b, jj: b * NQB + jj),
                blk(lambda b, jj: b * NQB), vblk(lambda b, jj: b * NQB),
                blk(win(0)), blk(win(1)), blk(win(2)),
                vblk(win(0)), vblk(win(1)), vblk(win(2)),
                _const_spec((H, T2_ENTRIES, GRID_W, LANES)),
            ],
            out_specs=blk(lambda b, jj: b * NQB + jj),
        ),
        out_shape=jax.ShapeDtypeStruct((NPAIR, NROWS, LANES), BF16),
        compiler_params=_cparams(2),
        name="na_attention",
    )(jnp.asarray(_bias_tile_table()), q, k, v, k, k, k, v, v, v, t2)


def _proj_kernel(x_ref, y_ref, gate_ref, w_ref, o_ref, *, t):
    b, pos = _tile_pos(t.tm, 0, t)
    y = jnp.concatenate([y_ref[p] for p in range(NPAIR)], axis=1)
    o_ref[...] = x_ref[...] + _mod_rows(gate_ref, b, pos < C) * jnp.dot(
        y, w_ref[...], preferred_element_type=F32)


def _proj_call(xs, y, mods, layer, w):
    t = WIDE
    return pl.pallas_call(
        functools.partial(_proj_kernel, t=t),
        grid=(NROWS // t.tm,),
        in_specs=[_row_spec(t=t), _pair_spec(t), _mods_spec(layer, 2), _const_spec((D, D))],
        out_specs=_row_spec(t=t),
        out_shape=jax.ShapeDtypeStruct((NROWS, D), F32),
        compiler_params=_cparams(1, [False, False, False, True]),
        name="na_out_proj",
    )(xs, y, mods, w)


def _bias_table(rpb):
    qc = jnp.arange(GRID_W)[:, None]
    kc = jnp.arange(GRID_W)[None, :]
    ws = jnp.clip(qc - WIN_COLS // 2, 0, GRID_W - WIN_COLS)
    col_ok = (kc >= ws) & (kc < ws + WIN_COLS)
    onehot = ((kc - qc + (WIN_COLS - 1))[None] == jnp.arange(2 * WIN_COLS - 1)[:, None, None])
    t = jnp.einsum("hdc,cqk->hdqk", rpb, onehot.astype(F32), precision=lax.Precision.HIGHEST)
    t = jnp.where(col_ok[None, None], t, NEG_INF)
    dead = jnp.full((H, 1, GRID_W, GRID_W), NEG_INF, F32)
    t_ext = jnp.concatenate([dead, t, dead], axis=1)
    pairs = jnp.concatenate([t_ext[:, :-1], t_ext[:, 1:]], axis=-1)
    lo, hi = WIN_ROWS // 2 - 1, WIN_ROWS // 2 - 1 + WIN_ROWS - 1
    left_dead = jnp.concatenate([dead, t[:, lo:lo + 1]], axis=-1)
    right_dead = jnp.concatenate([t[:, hi:hi + 1], dead], axis=-1)
    all_dead = jnp.concatenate([dead, dead], axis=-1)
    return jnp.concatenate([pairs, left_dead, right_dead, all_dead], axis=1)


assert 8 < C % TM < TM - 8
BND = C % TM
BND_LO = BND - 8


def _halo_specs():
    t8 = TM // 8
    return [
        pl.BlockSpec((8, D), lambda i: (jnp.maximum(i * t8 - 1, 0), 0)),
        _row_spec(),
        pl.BlockSpec((8, D), lambda i: (jnp.minimum((i + 1) * t8, NROWS // 8 - 1), 0)),
    ]


def _dead_halo_rows(reach_back, t=STREAM):
    start = t.base + (pl.program_id(0) % t.tpb) * t.tm
    e = lax.broadcasted_iota(jnp.int32, (t.tm + 16, 1), 0)
    at_start = (start == 0) | (start == C)
    at_end = (start + t.tm == C) | (start + t.tm == LT)
    return ((e >= 8 - reach_back) & (e < 8) & at_start) | ((e == t.tm + 8) & at_end)


def _slab_conv(u_scr, s, taps, cb, pos, t=STREAM):
    def seg(lo, n, masked):
        y = cb
        for off, w in taps:
            u = u_scr[s, pl.ds(lo + 8 + off, n), :]
            if masked and off != 0:
                p = pos[lo:lo + n]
                crosses = (p >= C) & (p < C - off) if off < 0 else (p < C) & (p >= C - off)
                u = jnp.where(crosses, 0.0, u)
            y = y + u * w
        return y
    if t is LATENT:
        return seg(0, t.tm, False)
    return jnp.concatenate([seg(0, BND_LO, False), seg(BND_LO, 16, True),
                            seg(BND_LO + 16, TM - BND_LO - 16, False)], axis=0)


def _lru_in_kernel(xp_ref, x_ref, xn_ref, sh_ref, sc_ref, gain_ref, cw_ref, cb_ref, *rest):
    w_ref = rest[:2 * D // GW]
    gg_ref, xr_ref, rec_scr = rest[2 * D // GW:]
    b, pos = _tile_pos(TM, 0)
    _, pos_e = _tile_pos(TM + 16, -8)
    is_ctx_e = pos_e < C
    x_e = jnp.concatenate([xp_ref[...], x_ref[...], xn_ref[...]], axis=0)
    h = _modulate(x_e, gain_ref[...], _mod_rows(sh_ref, b, is_ctx_e),
                  _mod_rows(sc_ref, b, is_ctx_e))
    h = jnp.where(_dead_halo_rows(2), 0.0, h).astype(BF16)
    cw = cw_ref[...]
    cb = cb_ref[...]
    ngroup = D // GW

    def project(c):
        return (jnp.dot(h, w_ref[c][...], preferred_element_type=F32),
                jnp.dot(h, w_ref[ngroup + c][...], preferred_element_type=F32))

    pending = project(0)
    for c in range(ngroup):
        nxt = project(c + 1) if c + 1 < ngroup else None
        gate, rec = pending
        gg_ref[:, c * GW:(c + 1) * GW] = jax.nn.gelu(
            gate[8:8 + TM], approximate=True).astype(BF16)
        for s2 in range(GW // LANES):
            s = c * (GW // LANES) + s2
            ls = slice(s * LANES, (s + 1) * LANES)
            rec_scr[s] = rec[:, s2 * LANES:(s2 + 1) * LANES]
            taps = [(k - 2, cw[k:k + 1, ls]) for k in range(4)]
            xr_ref[:, ls] = _slab_conv(rec_scr, s, taps, cb[:, ls], pos)
        pending = nxt


def _lru_in_call(xs, mods, layer, gain, w, j, cw, cb):
    return pl.pallas_call(
        _lru_in_kernel,
        grid=(NROWS // TM,),
        in_specs=_halo_specs() + [
            _mods_spec(layer, 0), _mods_spec(layer, 1), _const_spec((1, D)),
            _const_spec((4, D)), _const_spec((1, D)),
        ] + _col_chunk_specs(j, GW, 2 * D // GW),
        out_specs=[_row_spec(), _row_spec()],
        out_shape=[jax.ShapeDtypeStruct((NROWS, D), BF16),
                   jax.ShapeDtypeStruct((NROWS, D), F32)],
        scratch_shapes=[pltpu.VMEM((D // LANES, TM + 16, LANES), F32)],
        compiler_params=_cparams(1, [False] * 8 + [True] * (2 * D // GW)),
        name="lru_in_proj",
    )(xs, xs, xs, mods, mods, gain, cw, cb, *([w] * (2 * D // GW)))


def _chunk_index(g, reverse):
    if not reverse:
        return g
    return jnp.where(g < CCH, CCH - 1 - g, NCH + CCH - 1 - g)


def _scan_kernel(xr_ref, lam_ref, wg_ref, ba_ref, bx_ref, *rest, reverse):
    if reverse:
        o_ref, a_scr, b_scr, h_scr = rest
    else:
        hsb_ref, gg_ref, x_ref, wo_ref, g1_ref, o_ref, a_scr, b_scr, h_scr = rest
    g = pl.program_id(0)

    @pl.when(g == 0)
    def _():
        h_scr[...] = jnp.zeros_like(h_scr)

    lam = lam_ref[...]
    half_log_a = (-0.5 * LRU_C) * (jnp.maximum(-lam, 0.0) + jnp.log1p(jnp.exp(-jnp.abs(lam))))
    nslab = D // LANES
    group_slabs = GW // LANES

    def gates(gq):
        cols = slice(gq * GW, (gq + 1) * GW)
        xg = jnp.concatenate([xr_ref[b, :, cols] for b in range(B)], axis=0)
        pre = jnp.dot(xg.astype(BF16), wg_ref[gq], preferred_element_type=F32)
        t_r = jnp.tanh(pre[:, :GW] + ba_ref[:, cols])
        t_i = jnp.tanh(pre[:, GW:] + bx_ref[:, cols])
        log_a = half_log_a[:, cols] * t_r + half_log_a[:, cols]
        a = jnp.exp(log_a)
        y = (-1.0 - a * a) * jnp.tanh(log_a)
        root = jnp.where(y > 0.0, y * lax.rsqrt(y), 0.0)
        hx = 0.5 * xg
        bb = root * (hx * t_i + hx)
        for k2 in range(group_slabs):
            k = gq * group_slabs + k2
            for b in range(B):
                rows = slice(k * B * PITCH + b * PITCH, k * B * PITCH + b * PITCH + TS)
                a_scr[rows, :] = a[b * TS:(b + 1) * TS, k2 * LANES:(k2 + 1) * LANES]
                b_scr[rows, :] = bb[b * TS:(b + 1) * TS, k2 * LANES:(k2 + 1) * LANES]

    def recurrence(gq):
        slabs = range(gq * group_slabs, (gq + 1) * group_slabs)
        hs = {k: h_scr[k] for k in slabs}
        for t in (range(TS - 1, -1, -1) if reverse else range(TS)):
            for k in slabs:
                idx = pl.ds(k * B * PITCH + t, B, stride=PITCH)
                hs[k] = a_scr[idx, :] * hs[k] + b_scr[idx, :]
                b_scr[idx, :] = hs[k]
        for k in slabs:
            h_scr[k] = hs[k]

    def slab_rows(b, k):
        return slice(k * B * PITCH + b * PITCH, k * B * PITCH + b * PITCH + TS)

    gates(0)
    for gq in range(D // GW):
        if gq + 1 < D // GW:
            gates(gq + 1)
        recurrence(gq)

    if reverse:
        for b in range(B):
            for k in range(nslab):
                o_ref[b, :, k * LANES:(k + 1) * LANES] = b_scr[slab_rows(b, k), :].astype(BF16)
    else:
        z = jnp.concatenate([
            jnp.concatenate([
                ((b_scr[slab_rows(b, k), :] + hsb_ref[b, :, k * LANES:(k + 1) * LANES])
                 * gg_ref[b, :, k * LANES:(k + 1) * LANES].astype(F32)).astype(BF16)
                for k in range(nslab)], axis=1)
            for b in range(B)], axis=0)
        proj = jnp.dot(z, wo_ref[...], preferred_element_type=F32)
        is_ctx = g < CCH
        for b in range(B):
            g1 = jnp.where(is_ctx, g1_ref[8:9, :], g1_ref[b:b + 1, :])
            o_ref[b] = x_ref[b] + g1 * proj[b * TS:(b + 1) * TS]


def _scan_call(xr3, lam, wg, ba, bx, reverse, fwd_args=None):
    main = pl.BlockSpec((B, TS, D), lambda g: (0, _chunk_index(g, reverse), 0))
    in_specs = [
        main, _const_spec((1, D)),
        _const_spec((D // GW, GW, 2 * GW)), _const_spec((1, D)), _const_spec((1, D)),
    ]
    args = [xr3, lam, wg, 0.5 * ba, 0.5 * bx]
    if not reverse:
        hsb, gg, xs3, wo, mods, layer = fwd_args
        in_specs += [main, main, main, _const_spec((D, D)), _mods_spec(layer, 2)]
        args += [hsb, gg, xs3, wo, mods]
    slab = pltpu.VMEM((D // LANES * B * PITCH, LANES), F32)
    return pl.pallas_call(
        functools.partial(_scan_kernel, reverse=reverse),
        grid=(NCH,),
        in_specs=in_specs,
        out_specs=main,
        out_shape=jax.ShapeDtypeStruct((B, LT, D), BF16 if reverse else F32),
        scratch_shapes=[slab, slab, pltpu.VMEM((D // LANES, B, LANES), F32)],
        compiler_params=_cparams(1),
        name="lru_scan_bwd" if reverse else "lru_scan_fwd",
    )(*args)


def _gate_weights(wa, wx):
    def bd(w):
        w4 = w.reshape(D // GW, GW // LRU_BLOCK_W, LRU_BLOCK_W, LRU_BLOCK_W)
        eye = jnp.eye(GW // LRU_BLOCK_W, dtype=w.dtype)
        return (w4[:, :, :, None, :] * eye[None, :, None, :, None]).reshape(D // GW, GW, GW)
    return (0.5 * jnp.concatenate([bd(wa), bd(wx)], axis=-1)).astype(BF16)


FSLAB = FC // LANES


def _ffn_kernel(*refs, t):
    nchunk = 2 * D_FF // FC
    xp_ref, *x_refs = refs[:1 + t.main_blocks]
    (xn_ref, sh_ref, sc_ref, g2_ref, gain_ref, cw_ref, cb_ref,
     wd_ref) = refs[1 + t.main_blocks:9 + t.main_blocks]
    wu_ref = refs[9 + t.main_blocks:9 + t.main_blocks + nchunk]
    o_ref, uv_scr, ug_scr = refs[9 + t.main_blocks + nchunk:]
    tm = t.tm
    b, pos = _tile_pos(tm, 0, t)
    _, pos_e = _tile_pos(tm + 16, -8, t)
    is_ctx_e = pos_e < C
    x = jnp.concatenate([r[...] for r in x_refs], axis=0)
    x_e = jnp.concatenate([xp_ref[...], x, xn_ref[...]], axis=0)
    h = _modulate(x_e, gain_ref[...], _mod_rows(sh_ref, b, is_ctx_e),
                  _mod_rows(sc_ref, b, is_ctx_e))
    h = jnp.where(_dead_halo_rows(1, t), 0.0, h).astype(BF16)

    cw = cw_ref[...]
    cb = cb_ref[...]
    def up_project(ci):
        uv = jnp.dot(h, wu_ref[ci][...], preferred_element_type=F32)
        ug = jnp.dot(h, wu_ref[D_FF // FC + ci][...], preferred_element_type=F32)
        base = (ci % 2) * FSLAB
        for s in range(FSLAB):
            uv_scr[base + s] = uv[:, s * LANES:(s + 1) * LANES]
            ug_scr[base + s] = ug[:, s * LANES:(s + 1) * LANES]

    acc = jnp.zeros((tm, D), F32)
    up_project(0)
    for ci in range(D_FF // FC):
        if ci + 1 < D_FF // FC:
            up_project(ci + 1)
        base = (ci % 2) * FSLAB
        acts = []
        for s in range(FSLAB):
            lv = slice(ci * FC + s * LANES, ci * FC + (s + 1) * LANES)
            lg = slice(D_FF + lv.start, D_FF + lv.stop)
            val = _slab_conv(uv_scr, base + s, [(k - 1, cw[k:k + 1, lv]) for k in range(3)],
                             cb[:, lv], pos, t)
            hg = _slab_conv(ug_scr, base + s,
                            [(k - 1, 0.5 * cw[k:k + 1, lg]) for k in range(3)],
                            0.5 * cb[:, lg], pos, t)
            acts.append((val * (hg + hg * jnp.tanh(hg))).astype(BF16))
        acc = acc + jnp.dot(jnp.concatenate(acts, axis=1), wd_ref[ci],
                            preferred_element_type=F32)
    o_ref[...] = x + _mod_rows(g2_ref, b, pos < C) * acc


def _ffn_call(xs, mods, layer, gain, wu, cw, cb, wd, t):
    nchunk = 2 * D_FF // FC

    def layer_spec(shape):
        return pl.BlockSpec((None,) + shape, lambda i: (layer,) + (0,) * len(shape),
                            pipeline_mode=pl.Buffered(1))

    slab = pltpu.VMEM((2 * FSLAB, t.tm + 16, LANES), F32)
    if t is STREAM:
        x_specs, out_rows = _halo_specs(), NROWS
    else:
        def row0(i):
            return (i // t.tpb) * LT + t.base + (i % t.tpb) * t.tm
        half = t.tm // t.main_blocks
        x_specs = (
            [pl.BlockSpec((8, D), lambda i: (row0(i) // 8 - 1, 0))]
            + [pl.BlockSpec((half, D), lambda i, k=k: (row0(i) // half + k, 0))
               for k in range(t.main_blocks)]
            + [pl.BlockSpec((8, D), lambda i: (jnp.minimum((row0(i) + t.tm) // 8,
                                                            NROWS // 8 - 1), 0))])
        out_rows = B * S
    return pl.pallas_call(
        functools.partial(_ffn_kernel, t=t),
        grid=(out_rows // t.tm,),
        in_specs=x_specs + [
            _mods_spec(layer, 3), _mods_spec(layer, 4), _mods_spec(layer, 5),
            _const_spec((1, D)), layer_spec((3, 2 * D_FF)),
            layer_spec((1, 2 * D_FF)), layer_spec((D_FF // FC, FC, D)),
        ] + _col_chunk_specs(layer, FC, nchunk),
        out_specs=pl.BlockSpec((t.tm, D), lambda i: (i, 0)),
        out_shape=jax.ShapeDtypeStruct((out_rows, D), F32),
        scratch_shapes=[slab, slab],
        compiler_params=_cparams(1, [False] * (len(x_specs) + 6) + [True] * (1 + nchunk)),
        name="conv_ffn",
    )(*([xs] * (2 + t.main_blocks)), mods, mods, mods, gain, cw, cb, wd, *([wu] * nchunk))


def kernel(x, c, ctx, c_ctx, ada_w, ada_b, norm_mix, norm_ffn, na_w_qkv, na_q_gain, na_k_gain,
           na_rpb, na_w_out, lru_w_in, lru_conv_w, lru_conv_b, lru_ga_w, lru_ga_b, lru_gx_w,
           lru_gx_b, lru_lambda, lru_w_out, ffn_w_up, ffn_conv_w, ffn_conv_b, ffn_w_down):
    xs = jnp.concatenate([ctx, x], axis=1).reshape(NROWS, D)
    cc = jnp.concatenate([c, c_ctx[None], jnp.zeros((7, D), F32)], axis=0)
    mods = _mods_call(cc, ada_w, ada_b)

    head_of_lane = jnp.arange(D) // HD
    g1 = (head_of_lane[:, None] == jnp.arange(LANES)[None, :]).astype(BF16)

    w_qkv = na_w_qkv.astype(BF16)
    w_in = lru_w_in.astype(BF16)
    w_up = ffn_w_up.astype(BF16)
    w_down = ffn_w_down.astype(BF16).reshape(DEPTH, D_FF // FC, FC, D)
    conv_b = ffn_conv_b[:, None]

    for i in range(DEPTH):
        j = i // 2
        gain_mix = norm_mix[i][None]
        if i % 2 == 0:
            qg = (jnp.tile(na_q_gain[j], H) * (HD ** -0.5 * LOG2_E))[None]
            kg = jnp.tile(na_k_gain[j], H)[None]
            q, k, v = _qkv_call(xs, mods, i, gain_mix, w_qkv, j, qg, kg, g1)
            o = _attn_call(q, k, v, _bias_table(na_rpb[j] * LOG2_E))
            xs = _proj_call(xs, o, mods, i, na_w_out[j].astype(BF16))
        else:
            gg, xr = _lru_in_call(xs, mods, i, gain_mix, w_in, j,
                                  lru_conv_w[j], lru_conv_b[j][None])
            xr3 = xr.reshape(B, LT, D)
            hsb = _scan_call(xr3, lru_lambda[j, 1][None],
                             _gate_weights(lru_ga_w[j, 1], lru_gx_w[j, 1]),
                             lru_ga_b[j, 1][None], lru_gx_b[j, 1][None], True)
            xs = _scan_call(xr3, lru_lambda[j, 0][None],
                            _gate_weights(lru_ga_w[j, 0], lru_gx_w[j, 0]),
                            lru_ga_b[j, 0][None], lru_gx_b[j, 0][None], False,
                            (hsb, gg.reshape(B, LT, D), xs.reshape(B, LT, D),
                             lru_w_out[j].astype(BF16), mods, i)).reshape(NROWS, D)
        xs = _ffn_call(xs, mods, i, norm_ffn[i][None], w_up, ffn_conv_w, conv_b, w_down,
                       LATENT if i == DEPTH - 1 else STREAM)
    return xs.reshape(B, S, D)
```
